```python
import math
import jax, jax.numpy as jnp
from jax import lax
import numpy as np

D_MODEL = 2048
BATCH = 8
SEQ = 2048
DEPTH = 1

HEAD_DIM = 64
ROPE_THETA = 10000.0
RMS_EPS = 1e-6
NEG_INF = -1e30
SWA_Q_HEADS = 16
SWA_KV_HEADS = 2
SWA_GROUP = SWA_Q_HEADS // SWA_KV_HEADS
WINDOW = 128
SWA_BLOCK = WINDOW
DIFF_HEADS = 8
DIFF_V_DIM = 2 * HEAD_DIM
Q_BLOCK = 128
SWA_Q_COLS = SWA_Q_HEADS * HEAD_DIM
SWA_KV_COLS = SWA_KV_HEADS * HEAD_DIM
DIFF_QK_COLS = DIFF_HEADS * 2 * HEAD_DIM
DIFF_V_COLS = DIFF_HEADS * DIFF_V_DIM
GATE_COLS = 2 * D_MODEL
IN_COLS = SWA_Q_COLS + 2 * SWA_KV_COLS + 2 * DIFF_QK_COLS + DIFF_V_COLS + GATE_COLS
N_GROUPS = 8
EXPERTS_PER_GROUP = 8
N_EXPERTS = N_GROUPS * EXPERTS_PER_GROUP
TOP_K = 2
D_EXPERT = D_MODEL // 2
MOE_BLOCK = 128

kernel_name = "hybrid_swa_sink_diffattn_hiermoe_adaln"


def rmsnorm(x, g):
    xf = x.astype(jnp.float32)
    y = xf * lax.rsqrt(jnp.mean(xf * xf, axis=-1, keepdims=True) + RMS_EPS)
    return (y * g.astype(jnp.float32)).astype(x.dtype)


def rope_tables(positions):
    inv_freq = ROPE_THETA ** (-jnp.arange(0, HEAD_DIM, 2, dtype=jnp.float32) / HEAD_DIM)
    ang = positions.astype(jnp.float32)[..., None] * inv_freq
    return jnp.cos(ang), jnp.sin(ang)


def apply_rope(x, cos, sin):
    shp = cos.shape[:2] + (1,) * (x.ndim - 3) + cos.shape[-1:]
    cos, sin = cos.reshape(shp), sin.reshape(shp)
    xf = x.astype(jnp.float32)
    x1, x2 = jnp.split(xf, 2, axis=-1)
    return jnp.concatenate([x1 * cos - x2 * sin, x2 * cos + x1 * sin], axis=-1).astype(x.dtype)


def swa_sink_attention(q, k, v, sinks):
    B, S = q.shape[:2]
    nb = S // SWA_BLOCK
    qb = q.reshape(B, nb, SWA_BLOCK, SWA_KV_HEADS, SWA_GROUP, HEAD_DIM)

    def band(t):
        tb = t.reshape(B, nb, SWA_BLOCK, SWA_KV_HEADS, HEAD_DIM)
        prev = jnp.concatenate([jnp.zeros_like(tb[:, :1]), tb[:, :-1]], axis=1)
        return jnp.concatenate([prev, tb], axis=2)

    kk, vv = band(k), band(v)
    s = jnp.einsum('bnqhgd,bnkhd->bnhgqk', qb, kk).astype(jnp.float32) / math.sqrt(HEAD_DIM)
    qi = jnp.arange(SWA_BLOCK)[:, None]
    kj = jnp.arange(2 * SWA_BLOCK)[None, :]
    rel = qi + SWA_BLOCK - kj
    local = (rel >= 0) & (rel < WINDOW)
    kpos = jnp.arange(nb)[:, None] * SWA_BLOCK - SWA_BLOCK + kj
    mask = local[None, None, None] & (kpos >= 0)[:, None, None, None, :]
    s = jnp.where(mask, s, NEG_INF)
    sink = jnp.broadcast_to(sinks.astype(jnp.float32).reshape(1, 1, SWA_KV_HEADS, SWA_GROUP, 1, 1),
                            s.shape[:-1] + (1,))
    p = jax.nn.softmax(jnp.concatenate([s, sink], axis=-1), axis=-1)[..., :-1]
    o = jnp.einsum('bnhgqk,bnkhd->bnqhgd', p.astype(v.dtype), vv)
    return o.reshape(B, S, SWA_Q_HEADS * HEAD_DIM)


def diff_attention(q, k, v, lam, lambda_init, subln_g):
    B, S = q.shape[:2]
    nb = S // Q_BLOCK
    qb = jnp.moveaxis(q.reshape(B, nb, Q_BLOCK, DIFF_HEADS, 2, HEAD_DIM), 1, 0)
    kpos = jnp.arange(S)

    def block(args):
        qblk, n = args
        s = jnp.einsum('bqhcd,bkhcd->bhcqk', qblk, k).astype(jnp.float32) / math.sqrt(HEAD_DIM)
        qpos = n * Q_BLOCK + jnp.arange(Q_BLOCK)
        s = jnp.where(kpos[None, :] <= qpos[:, None], s, NEG_INF)
        p = jax.nn.softmax(s, axis=-1)
        a = p[:, :, 0] - lam * p[:, :, 1]
        return jnp.einsum('bhqk,bkhe->bqhe', a.astype(v.dtype), v)

    o = lax.map(block, (qb, jnp.arange(nb)))
    o = jnp.moveaxis(o, 0, 1).reshape(B, S, DIFF_HEADS, DIFF_V_DIM)
    o = rmsnorm(o, subln_g) * (1.0 - lambda_init)
    return o.reshape(B, S, DIFF_HEADS * DIFF_V_DIM)


def hier_moe(h, w_rg, b_rg, w_re, b_re, w_gu, w_dn):
    N = h.shape[0]
    glog = (h @ w_rg).astype(jnp.float32) + b_rg.astype(jnp.float32)
    gprob = jax.nn.softmax(glog, axis=-1)
    gsel = jnp.argmax(glog, axis=-1)
    gweight = jnp.take_along_axis(gprob, gsel[:, None], axis=1)[:, 0]
    elog = ((h @ w_re).astype(jnp.float32) + b_re.astype(jnp.float32)).reshape(N, N_GROUPS, EXPERTS_PER_GROUP)
    elog_sel = jnp.take_along_axis(elog, gsel[:, None, None], axis=1)[:, 0]
    top_v, top_i = lax.top_k(elog_sel, TOP_K)
    weights = gweight[:, None] * jax.nn.softmax(top_v, axis=-1)
    expert_ids = (gsel[:, None] * EXPERTS_PER_GROUP + top_i).astype(jnp.int32)

    A = N * TOP_K
    flat_e = expert_ids.reshape(-1)
    flat_tok = jnp.repeat(jnp.arange(N, dtype=jnp.int32), TOP_K)
    flat_w = weights.reshape(-1)
    order = jnp.argsort(flat_e)
    sorted_e = flat_e[order]
    counts = jnp.bincount(flat_e, length=N_EXPERTS)
    padded = ((counts + MOE_BLOCK - 1) // MOE_BLOCK) * MOE_BLOCK
    pend = jnp.cumsum(padded)
    pstart = pend - padded
    start = jnp.cumsum(counts) - counts
    dest = pstart[sorted_e] + jnp.arange(A) - start[sorted_e]
    P = A + N_EXPERTS * MOE_BLOCK
    n_blocks = P // MOE_BLOCK
    row_tok = jnp.full((P,), N, dtype=jnp.int32).at[dest].set(flat_tok[order])
    row_w = jnp.zeros((P,), jnp.float32).at[dest].set(flat_w[order])
    block_e = jnp.minimum(jnp.searchsorted(pend, jnp.arange(n_blocks) * MOE_BLOCK, side='right'),
                          N_EXPERTS - 1).astype(jnp.int32)
    h_pad = jnp.concatenate([h, jnp.zeros((1, h.shape[1]), h.dtype)], axis=0)
    xs = h_pad[row_tok].reshape(n_blocks, MOE_BLOCK, h.shape[1])

    def run_block(args):
        xb, e = args
        g, u = jnp.split(xb @ w_gu[e], 2, axis=-1)
        return (jax.nn.silu(g) * u) @ w_dn[e]

    yb = lax.map(run_block, (xs, block_e)).reshape(P, h.shape[1])
    yb = (yb.astype(jnp.float32) * row_w[:, None]).astype(h.dtype)
    return jnp.zeros((N + 1, h.shape[1]), h.dtype).at[row_tok].add(yb)[:N]


def setup_inputs(seed: int = 0) -> dict:
    key = jax.random.key(seed)
    ks = jax.random.split(key, 24)
    D, L = D_MODEL, DEPTH
    nrm = lambda k, shp, s: jax.random.normal(k, shp, jnp.float32) * s
    return {
        "x": nrm(ks[0], (BATCH, SEQ, D), 1.0),
        "c": nrm(ks[1], (BATCH, D), 1.0),
        "positions": jnp.broadcast_to(jnp.arange(SEQ, dtype=jnp.int32)[None, :], (BATCH, SEQ)),
        "w_ada": nrm(ks[2], (L, D, 6 * D), 0.5 * D ** -0.5),
        "b_ada": nrm(ks[3], (L, 6 * D), 0.02),
        "g_mix": 1.0 + nrm(ks[4], (L, D), 0.02),
        "w_in": nrm(ks[5], (L, D, IN_COLS), D ** -0.5),
        "swa_sinks": nrm(ks[6], (L, SWA_Q_HEADS), 0.5),
        "diff_lambda_q1": nrm(ks[7], (L, HEAD_DIM), 0.1),
        "diff_lambda_k1": nrm(ks[8], (L, HEAD_DIM), 0.1),
        "diff_lambda_q2": nrm(ks[9], (L, HEAD_DIM), 0.1),
        "diff_lambda_k2": nrm(ks[10], (L, HEAD_DIM), 0.1),
        "diff_subln_g": 1.0 + nrm(ks[11], (L, DIFF_V_DIM), 0.02),
        "w_branch_a": nrm(ks[12], (L, SWA_Q_COLS, D), SWA_Q_COLS ** -0.5),
        "w_branch_b": nrm(ks[13], (L, DIFF_V_COLS, D), DIFF_V_COLS ** -0.5),
        "w_out": nrm(ks[14], (L, D, D), D ** -0.5),
        "g_ffn": 1.0 + nrm(ks[15], (L, D), 0.02),
        "w_router_group": nrm(ks[16], (L, D, N_GROUPS), D ** -0.5),
        "b_router_group": nrm(ks[17], (L, N_GROUPS), 0.01),
        "w_router_expert": nrm(ks[18], (L, D, N_EXPERTS), D ** -0.5),
        "b_router_expert": nrm(ks[19], (L, N_EXPERTS), 0.01),
        "w_exp_gate_up": nrm(ks[20], (L, N_EXPERTS, D, 2 * D_EXPERT), D ** -0.5),
        "w_exp_down": nrm(ks[21], (L, N_EXPERTS, D_EXPERT, D), D_EXPERT ** -0.5),
        "g_final": 1.0 + nrm(ks[22], (D,), 0.02),
    }


def reference(x, c, positions, w_ada, b_ada, g_mix, w_in, swa_sinks, diff_lambda_q1, diff_lambda_k1,
              diff_lambda_q2, diff_lambda_k2, diff_subln_g, w_branch_a, w_branch_b, w_out, g_ffn,
              w_router_group, b_router_group, w_router_expert, b_router_expert, w_exp_gate_up,
              w_exp_down, g_final):
    B, S, D = x.shape
    cos, sin = rope_tables(positions)
    c_act = jax.nn.silu(c)
    splits = np.cumsum([SWA_Q_COLS, SWA_KV_COLS, SWA_KV_COLS, DIFF_QK_COLS, DIFF_QK_COLS, DIFF_V_COLS, D_MODEL])
    for layer in range(DEPTH):
        lambda_init = 0.8 - 0.6 * math.exp(-0.3 * layer)
        mod = c_act @ w_ada[layer] + b_ada[layer]
        sh_m, sc_m, gt_m, sh_f, sc_f, gt_f = [m[:, None, :] for m in jnp.split(mod, 6, axis=-1)]

        h = rmsnorm(x, g_mix[layer]) * (1.0 + sc_m) + sh_m
        proj = h @ w_in[layer]
        qa, ka, va, qd, kd, vd, ga, gb = jnp.split(proj, splits, axis=-1)
        qa = apply_rope(qa.reshape(B, S, SWA_Q_HEADS, HEAD_DIM), cos, sin)
        ka = apply_rope(ka.reshape(B, S, SWA_KV_HEADS, HEAD_DIM), cos, sin)
        va = va.reshape(B, S, SWA_KV_HEADS, HEAD_DIM)
        qd = apply_rope(qd.reshape(B, S, DIFF_HEADS, 2, HEAD_DIM), cos, sin)
        kd = apply_rope(kd.reshape(B, S, DIFF_HEADS, 2, HEAD_DIM), cos, sin)
        vd = vd.reshape(B, S, DIFF_HEADS, DIFF_V_DIM)
        lam = (jnp.exp(jnp.sum(diff_lambda_q1[layer].astype(jnp.float32) * diff_lambda_k1[layer].astype(jnp.float32)))
               - jnp.exp(jnp.sum(diff_lambda_q2[layer].astype(jnp.float32) * diff_lambda_k2[layer].astype(jnp.float32)))
               + lambda_init)
        o_a = swa_sink_attention(qa, ka, va, swa_sinks[layer])
        o_b = diff_attention(qd, kd, vd, lam, lambda_init, diff_subln_g[layer])
        merged = jax.nn.sigmoid(ga) * (o_a @ w_branch_a[layer]) + jax.nn.sigmoid(gb) * (o_b @ w_branch_b[layer])
        x = x + gt_m * (merged @ w_out[layer])

        h2 = rmsnorm(x, g_ffn[layer]) * (1.0 + sc_f) + sh_f
        ffn = hier_moe(h2.reshape(B * S, D), w_router_group[layer], b_router_group[layer],
                       w_router_expert[layer], b_router_expert[layer],
                       w_exp_gate_up[layer], w_exp_down[layer]).reshape(B, S, D)
        x = x + gt_f * ffn
    return rmsnorm(x, g_final)
```

```python
import functools
import math

import jax
import jax.numpy as jnp
import numpy as np
from jax import lax
from jax.experimental import pallas as pl
from jax.experimental.pallas import tpu as pltpu

D_MODEL = 2048
BATCH = 8
SEQ = 2048
N_TOK = BATCH * SEQ
HEAD_DIM = 64
ROPE_THETA = 10000.0
RMS_EPS = 1e-6
NEG_INF = -1e30
SWA_Q_HEADS = 16
SWA_KV_HEADS = 2
SWA_GROUP = SWA_Q_HEADS // SWA_KV_HEADS
WINDOW = 128
DIFF_HEADS = 8
N_GROUPS = 8
EXPERTS_PER_GROUP = 8
N_EXPERTS = N_GROUPS * EXPERTS_PER_GROUP
TOP_K = 2
D_EXPERT = D_MODEL // 2
LAMBDA_INIT = 0.8 - 0.6 * math.exp(-0.3 * 0)

LANES = 128
VMEM_LIMIT = 56 * 1024 * 1024

PROJ_TN = 512
PROJ_COLS = 17 * PROJ_TN
COL_QA, COL_QD, COL_KD, COL_VD, COL_GA, COL_GB, COL_KVA = 0, 1024, 2048, 3072, 4096, 6144, 8192

MOE_TM = 256
MOE_ROWS = N_TOK * TOP_K + N_EXPERTS * MOE_TM
MOE_NBLK = MOE_ROWS // MOE_TM
ROUTE_T = 2048
ROUTER_ROWS = 128

f32 = jnp.float32
bf16 = jnp.bfloat16
i32 = jnp.int32


def _cparams(sem, vmem=VMEM_LIMIT):
    return pltpu.CompilerParams(dimension_semantics=sem, vmem_limit_bytes=vmem)


def _dot(a, b):
    return jnp.dot(a, b, preferred_element_type=f32)


def _dot_nt(a, b):
    return lax.dot_general(a, b, (((1,), (1,)), ((), ())), preferred_element_type=f32)


def _ada_kernel(c_ref, w_ref, b_ref, o_ref):
    c = c_ref[...]
    ca = (c * jax.nn.sigmoid(c)).astype(bf16)
    o_ref[...] = _dot(ca, w_ref[...].astype(bf16)) + b_ref[...]


def _ada(c, w_ada, b_ada):
    tn = 512
    n = w_ada.shape[1]
    return pl.pallas_call(
        _ada_kernel,
        grid=(n // tn,),
        in_specs=[pl.BlockSpec((BATCH, D_MODEL), lambda j: (0, 0)),
                  pl.BlockSpec((D_MODEL, tn), lambda j: (0, j)),
                  pl.BlockSpec((1, tn), lambda j: (0, j))],
        out_specs=pl.BlockSpec((BATCH, tn), lambda j: (0, j)),
        out_shape=jax.ShapeDtypeStruct((BATCH, n), f32),
        compiler_params=_cparams(("arbitrary",)),
        name="ada",
    )(c, w_ada, b_ada.reshape(1, n))


PROJ_TM = 1024


def _rope128(xc, cos, sin_signed, first_half):
    r_prev = pltpu.roll(xc, 32, 1)
    r_next = pltpu.roll(xc, 96, 1)
    return xc * cos + jnp.where(first_half, r_next, r_prev) * sin_signed


def _proj_kernel(x_ref, sc_ref, sh_ref, g_ref, pos_ref, invf_ref, w_ref, o_ref, h_scr, cos_scr, sin_scr):
    j = pl.program_id(1)
    lane = lax.broadcasted_iota(i32, (1, LANES), 1)
    first_half = (lane % HEAD_DIM) < (HEAD_DIM // 2)

    @pl.when(j == 0)
    def _():
        rc = 256
        for r in range(PROJ_TM // rc):
            x = x_ref[r * rc:(r + 1) * rc, :]
            ms = jnp.mean(x * x, axis=-1, keepdims=True)
            y = x * lax.rsqrt(ms + RMS_EPS) * g_ref[...]
            h = y * (1.0 + sc_ref[...]) + sh_ref[...]
            h_scr[r * rc:(r + 1) * rc, :] = h.astype(bf16)
        ang = pos_ref[...].astype(f32) * invf_ref[...]
        cos_scr[...] = jnp.cos(ang)
        sin_scr[...] = jnp.sin(ang) * jnp.where(first_half, -1.0, 1.0)

    acc = _dot(h_scr[...], w_ref[...])
    nch = PROJ_TN // LANES

    def rope_tile(scale):
        for c in range(nch):
            xc = acc[:, c * LANES:(c + 1) * LANES]
            o_ref[:, c * LANES:(c + 1) * LANES] = (
                _rope128(xc, cos_scr[...], sin_scr[...], first_half) * scale).astype(bf16)

    @pl.when(j < 4)
    def _():
        rope_tile(1.0 / math.sqrt(HEAD_DIM))

    @pl.when((j >= 4) & (j < 6))
    def _():
        rope_tile(1.0)

    @pl.when((j >= 6) & (j < 8))
    def _():
        o_ref[...] = acc.astype(bf16)

    @pl.when((j >= 8) & (j < 16))
    def _():
        o_ref[...] = jax.nn.sigmoid(acc).astype(bf16)

    @pl.when(j == 16)
    def _():
        o_ref[:, 0:LANES] = _rope128(acc[:, 0:LANES], cos_scr[...], sin_scr[...], first_half).astype(bf16)
        o_ref[:, LANES:] = acc[:, LANES:].astype(bf16)


def _proj(x2, mod3, g_mix, pos_col, invf, w_in_p):
    tm = PROJ_TM
    per_b = SEQ // tm
    return pl.pallas_call(
        _proj_kernel,
        grid=(N_TOK // tm, PROJ_COLS // PROJ_TN),
        in_specs=[pl.BlockSpec((tm, D_MODEL), lambda i, j: (i, 0)),
                  pl.BlockSpec((None, 1, D_MODEL), lambda i, j: (i // per_b, 0, 1)),
                  pl.BlockSpec((None, 1, D_MODEL), lambda i, j: (i // per_b, 0, 0)),
                  pl.BlockSpec((1, D_MODEL), lambda i, j: (0, 0)),
                  pl.BlockSpec((tm, 1), lambda i, j: (i, 0)),
                  pl.BlockSpec((1, LANES), lambda i, j: (0, 0)),
                  pl.BlockSpec((D_MODEL, PROJ_TN), lambda i, j: (0, j))],
        out_specs=pl.BlockSpec((tm, PROJ_TN), lambda i, j: (i, j)),
        out_shape=jax.ShapeDtypeStruct((N_TOK, PROJ_COLS), bf16),
        scratch_shapes=[pltpu.VMEM((tm, D_MODEL), bf16),
                        pltpu.VMEM((tm, LANES), f32),
                        pltpu.VMEM((tm, LANES), f32)],
        compiler_params=_cparams(("arbitrary", "arbitrary")),
        name="proj",
    )(x2, mod3, mod3, g_mix, pos_col, invf, w_in_p)


def _swa_kernel(sink_ref, q_ref, kv_ref, kvp_ref, o_ref):
    n = pl.program_id(1)
    qi = lax.broadcasted_iota(i32, (WINDOW, 2 * WINDOW), 0)
    kj = lax.broadcasted_iota(i32, (WINDOW, 2 * WINDOW), 1)
    rel = qi + WINDOW - kj
    first_key = jnp.where(n > 0, 0, WINDOW)
    valid = (rel >= 0) & (rel < WINDOW) & (kj >= first_key)
    for kvh in range(SWA_KV_HEADS):
        ks = slice(kvh * HEAD_DIM, (kvh + 1) * HEAD_DIM)
        vs = slice(LANES + kvh * HEAD_DIM, LANES + (kvh + 1) * HEAD_DIM)
        kk = jnp.concatenate([kvp_ref[:, ks], kv_ref[:, ks]], axis=0)
        vv = jnp.concatenate([kvp_ref[:, vs], kv_ref[:, vs]], axis=0)
        for g in range(SWA_GROUP):
            head = kvh * SWA_GROUP + g
            hs = slice(head * HEAD_DIM, (head + 1) * HEAD_DIM)
            s = _dot_nt(q_ref[:, hs], kk)
            s = jnp.where(valid, s, NEG_INF)
            sink = sink_ref[head]
            m = jnp.maximum(jnp.max(s, axis=-1, keepdims=True), sink)
            p = jnp.exp(s - m)
            den = jnp.sum(p, axis=-1, keepdims=True) + jnp.exp(sink - m)
            o = _dot(p.astype(bf16), vv) / den
            o_ref[:, hs] = o.astype(bf16)


def _swa(sinks, proj):
    nb = SEQ // WINDOW
    kv_col = COL_KVA // (2 * LANES)
    return pl.pallas_call(
        _swa_kernel,
        grid=(BATCH, nb),
        in_specs=[pl.BlockSpec(memory_space=pltpu.SMEM),
                  pl.BlockSpec((WINDOW, SWA_Q_HEADS * HEAD_DIM), lambda b, n: (b * nb + n, 0)),
                  pl.BlockSpec((WINDOW, 2 * LANES), lambda b, n: (b * nb + n, kv_col)),
                  pl.BlockSpec((WINDOW, 2 * LANES), lambda b, n: (jnp.maximum(b * nb + n - 1, 0), kv_col))],
        out_specs=pl.BlockSpec((WINDOW, SWA_Q_HEADS * HEAD_DIM), lambda b, n: (b * nb + n, 0)),
        out_shape=jax.ShapeDtypeStruct((N_TOK, SWA_Q_HEADS * HEAD_DIM), bf16),
        compiler_params=_cparams(("arbitrary", "arbitrary")),
        name="swa",
    )(sinks, proj, proj, proj)


DIFF_TQ = 256


def _diff_kernel(lq1_ref, lk1_ref, lq2_ref, lk2_ref, g_ref, q_ref, k_ref, v_ref, o_ref, m_scr, l_scr, acc_scr):
    qi = pl.program_id(2)
    tq = DIFF_TQ
    lane = lax.broadcasted_iota(i32, (tq, LANES), 1)
    q = q_ref[...]
    zero = jnp.zeros_like(q)
    q2 = jnp.concatenate([jnp.where(lane < HEAD_DIM, q, zero), jnp.where(lane >= HEAD_DIM, q, zero)], axis=0)

    m_scr[...] = jnp.full_like(m_scr, NEG_INF)
    l_scr[...] = jnp.zeros_like(l_scr)
    acc_scr[...] = jnp.zeros_like(acc_scr)

    def step(j, masked):
        kj = k_ref[pl.ds(pl.multiple_of(j * tq, tq), tq), :]
        vj = v_ref[pl.ds(pl.multiple_of(j * tq, tq), tq), :]
        s = _dot_nt(q2, kj)
        if masked:
            r = lax.broadcasted_iota(i32, (2 * tq, tq), 0) % tq
            c = lax.broadcasted_iota(i32, (2 * tq, tq), 1)
            s = jnp.where(c <= r, s, NEG_INF)
        m_old = m_scr[...]
        m_new = jnp.maximum(m_old, jnp.max(s, axis=-1, keepdims=True))
        alpha = jnp.exp(m_old - m_new)
        p = jnp.exp(s - m_new)
        l_scr[...] = alpha * l_scr[...] + jnp.sum(p, axis=-1, keepdims=True)
        acc_scr[...] = alpha * acc_scr[...] + _dot(p.astype(bf16), vj)
        m_scr[...] = m_new

    def body(j, carry):
        step(j, False)
        return carry

    lax.fori_loop(0, qi, body, 0)
    step(qi, True)

    o_all = acc_scr[...] / l_scr[...]
    lam = (jnp.exp(jnp.sum(lq1_ref[...] * lk1_ref[...], axis=-1, keepdims=True))
           - jnp.exp(jnp.sum(lq2_ref[...] * lk2_ref[...], axis=-1, keepdims=True)) + LAMBDA_INIT)
    o = o_all[0:tq, :] - lam * o_all[tq:2 * tq, :]
    y = o * lax.rsqrt(jnp.mean(o * o, axis=-1, keepdims=True) + RMS_EPS) * g_ref[...]
    o_ref[...] = (y * (1.0 - LAMBDA_INIT)).astype(bf16)


def _diff(lq1, lk1, lq2, lk2, subln_g, proj):
    tq = DIFF_TQ
    nq = SEQ // tq
    vec = lambda n: pl.BlockSpec((1, n), lambda b, h, i: (0, 0))
    return pl.pallas_call(
        _diff_kernel,
        grid=(BATCH, DIFF_HEADS, nq),
        in_specs=[vec(HEAD_DIM), vec(HEAD_DIM), vec(HEAD_DIM), vec(HEAD_DIM), vec(LANES),
                  pl.BlockSpec((tq, LANES), lambda b, h, i: (b * nq + i, COL_QD // LANES + h)),
                  pl.BlockSpec((SEQ, LANES), lambda b, h, i: (b, COL_KD // LANES + h)),
                  pl.BlockSpec((SEQ, LANES), lambda b, h, i: (b, COL_VD // LANES + h))],
        out_specs=pl.BlockSpec((tq, LANES), lambda b, h, i: (b * nq + i, h)),
        out_shape=jax.ShapeDtypeStruct((N_TOK, DIFF_HEADS * LANES), bf16),
        scratch_shapes=[pltpu.VMEM((2 * tq, 1), f32),
                        pltpu.VMEM((2 * tq, 1), f32),
                        pltpu.VMEM((2 * tq, LANES), f32)],
        compiler_params=_cparams(("arbitrary", "arbitrary", "arbitrary")),
        name="diff",
    )(lq1, lk1, lq2, lk2, subln_g, proj, proj, proj)


MERGE_TM = 256


def _split_bf16(v):
    hi = v.astype(bf16)
    lo = (v - hi.astype(f32)).astype(bf16)
    return hi, lo


def _merge_kernel(oa_ref, ob_ref, sga_ref, sgb_ref, x_ref, gt_ref, sc_ref, sh_ref, g_ref,
                  wa_ref, wb_ref, wo_ref, wr_ref, br_ref, x1_ref, hp_ref, lt_ref):
    a = _dot(oa_ref[...], wa_ref[...])
    b = _dot(ob_ref[...], wb_ref[...])
    merged = sga_ref[...].astype(f32) * a + sgb_ref[...].astype(f32) * b
    y = _dot(merged.astype(bf16), wo_ref[...])
    x1 = x_ref[...] + gt_ref[...] * y
    x1_ref[...] = x1
    ms = jnp.mean(x1 * x1, axis=-1, keepdims=True)
    h2 = (x1 * lax.rsqrt(ms + RMS_EPS) * g_ref[...]) * (1.0 + sc_ref[...]) + sh_ref[...]
    hb = lax.bitcast_convert_type(h2.astype(bf16).astype(f32), jnp.uint32)
    half = D_MODEL // 2
    hp_ref[...] = (hb[:, :half] >> 16) | (hb[:, half:] & jnp.uint32(0xFFFF0000))
    h_hi, h_lo = _split_bf16(h2)
    w_hi, w_lo = _split_bf16(wr_ref[...])
    lt_ref[...] = _dot_nt(w_hi, h_hi) + _dot_nt(w_hi, h_lo) + _dot_nt(w_lo, h_hi) + br_ref[...]


def _merge(o_a, o_b, proj, x2, mod3, g_ffn, wa, wb, wo, wr_t, br_col):
    tm = MERGE_TM
    per_b = SEQ // tm
    const = lambda shape: pl.BlockSpec(shape, lambda i: (0,) * len(shape), pipeline_mode=pl.Buffered(1))
    modspec = lambda chunk: pl.BlockSpec((None, 1, D_MODEL), lambda i: (i // per_b, 0, chunk))
    return pl.pallas_call(
        _merge_kernel,
        grid=(N_TOK // tm,),
        in_specs=[pl.BlockSpec((tm, 1024), lambda i: (i, 0)),
                  pl.BlockSpec((tm, 1024), lambda i: (i, 0)),
                  pl.BlockSpec((tm, D_MODEL), lambda i: (i, COL_GA // D_MODEL)),
                  pl.BlockSpec((tm, D_MODEL), lambda i: (i, COL_GB // D_MODEL)),
                  pl.BlockSpec((tm, D_MODEL), lambda i: (i, 0)),
                  modspec(2), modspec(4), modspec(3),
                  const((1, D_MODEL)),
                  const((1024, D_MODEL)), const((1024, D_MODEL)), const((D_MODEL, D_MODEL)),
                  const((ROUTER_ROWS, D_MODEL)), const((ROUTER_ROWS, 1))],
        out_specs=[pl.BlockSpec((tm, D_MODEL), lambda i: (i, 0)),
                   pl.BlockSpec((tm, D_MODEL // 2), lambda i: (i, 0)),
                   pl.BlockSpec((ROUTER_ROWS, tm), lambda i: (0, i))],
        out_shape=[jax.ShapeDtypeStruct((N_TOK, D_MODEL), f32),
                   jax.ShapeDtypeStruct((N_TOK, D_MODEL // 2), jnp.uint32),
                   jax.ShapeDtypeStruct((ROUTER_ROWS, N_TOK), f32)],
        compiler_params=_cparams(("arbitrary",)),
        name="merge",
    )(o_a, o_b, proj, proj, x2, mod3, mod3, mod3, g_ffn, wa, wb, wo, wr_t, br_col)


def _first_argmax(v, iota_f):
    vmax = jnp.max(v, axis=0, keepdims=True)
    idx = jnp.min(jnp.where(v == vmax, iota_f, float(v.shape[0])), axis=0, keepdims=True)
    return vmax, idx


def _route_kernel(lt_ref, ints_ref, w_ref, cnt_ref, carry):
    @pl.when(pl.program_id(0) == 0)
    def _():
        carry[...] = jnp.zeros_like(carry)

    cw = 256
    io8 = lax.broadcasted_iota(i32, (N_GROUPS, cw), 0).astype(f32)
    io64 = lax.broadcasted_iota(i32, (N_EXPERTS, cw), 0).astype(f32)
    tri = jnp.where(lax.broadcasted_iota(i32, (cw, cw), 0) < lax.broadcasted_iota(i32, (cw, cw), 1),
                    1.0, 0.0).astype(bf16)
    ints_ref[4:8, :] = jnp.zeros((4, ROUTE_T), i32)
    w_ref[2:8, :] = jnp.zeros((6, ROUTE_T), f32)
    for c in range(ROUTE_T // cw):
        sl = slice(c * cw, (c + 1) * cw)
        gl = lt_ref[0:N_GROUPS, sl]
        gmax, gsel = _first_argmax(gl, io8)
        gw = 1.0 / jnp.sum(jnp.exp(gl - gmax), axis=0, keepdims=True)
        es = jnp.zeros((EXPERTS_PER_GROUP, cw), f32)
        for g in range(N_GROUPS):
            lo = N_GROUPS + g * EXPERTS_PER_GROUP
            es = jnp.where(gsel == float(g), lt_ref[lo:lo + EXPERTS_PER_GROUP, sl], es)
        v1, i1 = _first_argmax(es, io8)
        es2 = jnp.where(io8 == i1, -jnp.inf, es)
        v2, i2 = _first_argmax(es2, io8)
        e = jnp.exp(v2 - v1)
        w_ref[0:1, sl] = gw / (1.0 + e)
        w_ref[1:2, sl] = gw * e / (1.0 + e)
        e1 = gsel * float(EXPERTS_PER_GROUP) + i1
        e2 = gsel * float(EXPERTS_PER_GROUP) + i2
        oh1 = io64 == e1
        oh2 = io64 == e2
        oc = jnp.where(oh1, 1.0, jnp.where(oh2, 1.0, 0.0))
        pre = _dot(oc.astype(bf16), tri) + carry[...]
        ints_ref[0:1, sl] = e1.astype(i32)
        ints_ref[1:2, sl] = e2.astype(i32)
        ints_ref[2:3, sl] = jnp.sum(jnp.where(oh1, pre, 0.0), axis=0, keepdims=True).astype(i32)
        ints_ref[3:4, sl] = jnp.sum(jnp.where(oh2, pre, 0.0), axis=0, keepdims=True).astype(i32)
        carry[...] = carry[...] + jnp.sum(oc, axis=1, keepdims=True)
    cnt_ref[...] = jnp.broadcast_to(carry[...], (N_EXPERTS, LANES)).astype(i32)


def _route(logits_t):
    t = ROUTE_T
    return pl.pallas_call(
        _route_kernel,
        grid=(N_TOK // t,),
        in_specs=[pl.BlockSpec((ROUTER_ROWS, t), lambda c: (0, c))],
        out_specs=[pl.BlockSpec((8, t), lambda c: (0, c)),
                   pl.BlockSpec((8, t), lambda c: (0, c)),
                   pl.BlockSpec((N_EXPERTS, LANES), lambda c: (0, 0))],
        out_shape=[jax.ShapeDtypeStruct((8, N_TOK), i32),
                   jax.ShapeDtypeStruct((8, N_TOK), f32),
                   jax.ShapeDtypeStruct((N_EXPERTS, LANES), i32)],
        scratch_shapes=[pltpu.VMEM((N_EXPERTS, 1), f32)],
        compiler_params=_cparams(("arbitrary",)),
        name="route",
    )(logits_t)


def _plan_kernel(cnt_ref, ints_ref, pos_ref, blk_e_ref, nused_ref, pstart_s):
    @pl.when(pl.program_id(0) == 0)
    def _():
        def per_expert(e, b):
            nb = (cnt_ref[e] + (MOE_TM - 1)) // MOE_TM
            pstart_s[e] = b * MOE_TM

            def mark(k, carry):
                blk_e_ref[b + k] = e
                return carry

            lax.fori_loop(0, nb, mark, 0)
            return b + nb

        nb_total = lax.fori_loop(0, N_EXPERTS, per_expert, 0)
        nused_ref[0] = nb_total
        last_e = blk_e_ref[nb_total - 1]

        def fill(k, carry):
            blk_e_ref[k] = last_e
            return carry

        lax.fori_loop(nb_total, MOE_NBLK, fill, 0)

    t = ROUTE_T
    iocol = lax.broadcasted_iota(i32, (N_EXPERTS, 1), 0)

    def build(e, col):
        return jnp.where(iocol == e, pstart_s[e], col)

    pcol = lax.fori_loop(0, N_EXPERTS, build, jnp.zeros((N_EXPERTS, 1), i32)).astype(f32)
    io64 = lax.broadcasted_iota(i32, (N_EXPERTS, t), 0)
    e1 = ints_ref[0:1, :]
    e2 = ints_ref[1:2, :]
    pos_ref[0:1, :] = (jnp.sum(jnp.where(io64 == e1, pcol, 0.0), axis=0, keepdims=True).astype(i32)
                       + ints_ref[2:3, :])
    pos_ref[1:2, :] = (jnp.sum(jnp.where(io64 == e2, pcol, 0.0), axis=0, keepdims=True).astype(i32)
                       + ints_ref[3:4, :])
    pos_ref[2:8, :] = jnp.zeros((6, t), i32)


def _plan(counts, ints):
    t = ROUTE_T
    return pl.pallas_call(
        _plan_kernel,
        grid_spec=pltpu.PrefetchScalarGridSpec(
            num_scalar_prefetch=1,
            grid=(N_TOK // t,),
            in_specs=[pl.BlockSpec((8, t), lambda c, cnt: (0, c))],
            out_specs=[pl.BlockSpec((8, t), lambda c, cnt: (0, c)),
                       pl.BlockSpec(memory_space=pltpu.SMEM),
                       pl.BlockSpec(memory_space=pltpu.SMEM)],
            scratch_shapes=[pltpu.SMEM((N_EXPERTS,), i32)]),
        out_shape=[jax.ShapeDtypeStruct((8, N_TOK), i32),
                   jax.ShapeDtypeStruct((MOE_NBLK,), i32),
                   jax.ShapeDtypeStruct((1,), i32)],
        compiler_params=_cparams(("arbitrary",)),
        name="plan",
    )(counts, ints)


DISP_TM = 256


def _dispatch_kernel(pos_ref, hp_ref, xs_in_ref, xs_ref, sem):
    del xs_in_ref
    i = pl.program_id(0)
    base = i * DISP_TM

    def row_copy(r, k):
        dst = pos_ref[k * N_TOK + base + r]
        return pltpu.make_async_copy(hp_ref.at[pl.ds(r, 1), :], xs_ref.at[pl.ds(dst, 1), :], sem)

    def issue(r, carry):
        row_copy(r, 0).start()
        row_copy(r, 1).start()
        return carry

    lax.fori_loop(0, DISP_TM, issue, 0)

    def drain(r, carry):
        row_copy(r, 0).wait()
        row_copy(r, 1).wait()
        return carry

    lax.fori_loop(0, DISP_TM, drain, 0)


def _dispatch(pos_flat, hp, xs_zero):
    tm = DISP_TM
    return pl.pallas_call(
        _dispatch_kernel,
        grid_spec=pltpu.PrefetchScalarGridSpec(
            num_scalar_prefetch=1,
            grid=(N_TOK // tm,),
            in_specs=[pl.BlockSpec((tm, D_MODEL // 2), lambda i, pos: (i, 0)),
                      pl.BlockSpec(memory_space=pl.ANY)],
            out_specs=pl.BlockSpec(memory_space=pl.ANY),
            scratch_shapes=[pltpu.SemaphoreType.DMA(())]),
        out_shape=jax.ShapeDtypeStruct((MOE_ROWS, D_MODEL // 2), jnp.uint32),
        input_output_aliases={2: 0},
        compiler_params=_cparams(("arbitrary",)),
        name="dispatch",
    )(pos_flat, hp, xs_zero)


def _blk(i, nused):
    return jnp.minimum(i, nused[0] - 1)


def _new_expert(i, blk_e):
    prev = blk_e[jnp.maximum(i - 1, 0)]
    return (i == 0) | (blk_e[i] != prev)


def _gate_up_kernel(blk_e, nused, xs_ref, w_ref, a_ref, wb_scr):
    i = pl.program_id(0)

    @pl.when(_new_expert(i, blk_e))
    def _():
        rc = 256
        for r in range(D_MODEL // rc):
            wb_scr[r * rc:(r + 1) * rc, :] = w_ref[r * rc:(r + 1) * rc, :].astype(bf16)

    @pl.when(i < nused[0])
    def _():
        packed = xs_ref[...]
        x_lo = lax.bitcast_convert_type(packed << 16, f32).astype(bf16)
        x_hi = lax.bitcast_convert_type(packed & jnp.uint32(0xFFFF0000), f32).astype(bf16)
        half = D_MODEL // 2
        gu = _dot(x_lo, wb_scr[0:half, :]) + _dot(x_hi, wb_scr[half:, :])
        g = gu[:, :D_EXPERT]
        u = gu[:, D_EXPERT:]
        a_ref[...] = (g * jax.nn.sigmoid(g) * u).astype(bf16)


def _gate_up(blk_e, nused, xs, w_gu):
    tm = MOE_TM
    return pl.pallas_call(
        _gate_up_kernel,
        grid_spec=pltpu.PrefetchScalarGridSpec(
            num_scalar_prefetch=2,
            grid=(MOE_NBLK,),
            in_specs=[pl.BlockSpec((tm, D_MODEL // 2), lambda i, be, nu: (_blk(i, nu), 0)),
                      pl.BlockSpec((None, D_MODEL, 2 * D_EXPERT), lambda i, be, nu: (be[i], 0, 0))],
            out_specs=pl.BlockSpec((tm, D_EXPERT), lambda i, be, nu: (_blk(i, nu), 0)),
            scratch_shapes=[pltpu.VMEM((D_MODEL, 2 * D_EXPERT), bf16)]),
        out_shape=jax.ShapeDtypeStruct((MOE_ROWS, D_EXPERT), bf16),
        compiler_params=_cparams(("arbitrary",)),
        name="gate_up",
    )(blk_e, nused, xs, w_gu)


def _down_kernel(blk_e, nused, a_ref, w_ref, y_ref, wb_scr):
    i = pl.program_id(0)

    @pl.when(_new_expert(i, blk_e))
    def _():
        rc = 256
        for r in range(D_EXPERT // rc):
            wb_scr[r * rc:(r + 1) * rc, :] = w_ref[r * rc:(r + 1) * rc, :].astype(bf16)

    @pl.when(i < nused[0])
    def _():
        y_ref[...] = _dot(a_ref[...], wb_scr[...])


def _down(blk_e, nused, a, w_dn):
    tm = MOE_TM
    return pl.pallas_call(
        _down_kernel,
        grid_spec=pltpu.PrefetchScalarGridSpec(
            num_scalar_prefetch=2,
            grid=(MOE_NBLK,),
            in_specs=[pl.BlockSpec((tm, D_EXPERT), lambda i, be, nu: (_blk(i, nu), 0)),
                      pl.BlockSpec((None, D_EXPERT, D_MODEL), lambda i, be, nu: (be[i], 0, 0))],
            out_specs=pl.BlockSpec((tm, D_MODEL), lambda i, be, nu: (_blk(i, nu), 0)),
            scratch_shapes=[pltpu.VMEM((D_EXPERT, D_MODEL), bf16)]),
        out_shape=jax.ShapeDtypeStruct((MOE_ROWS, D_MODEL), f32),
        compiler_params=_cparams(("arbitrary",)),
        name="down",
    )(blk_e, nused, a, w_dn)


COMB_TM = 256


def _combine_kernel(pos_ref, x1_ref, gt_ref, wc_ref, g_ref, y_ref, o_ref, ybuf, sem):
    i = pl.program_id(0)
    base = i * COMB_TM

    def row_copy(r, k):
        src = pos_ref[k * N_TOK + base + r]
        return pltpu.make_async_copy(y_ref.at[pl.ds(src, 1), :], ybuf.at[k, pl.ds(r, 1), :], sem)

    def issue(r, carry):
        row_copy(r, 0).start()
        row_copy(r, 1).start()
        return carry

    lax.fori_loop(0, COMB_TM, issue, 0)

    def drain(r, carry):
        row_copy(r, 0).wait()
        row_copy(r, 1).wait()
        return carry

    lax.fori_loop(0, COMB_TM, drain, 0)

    ffn = wc_ref[:, 0:1] * ybuf[0] + wc_ref[:, 1:2] * ybuf[1]
    xo = x1_ref[...] + gt_ref[...] * ffn
    ms = jnp.mean(xo * xo, axis=-1, keepdims=True)
    o_ref[...] = xo * lax.rsqrt(ms + RMS_EPS) * g_ref[...]


def _combine(pos_flat, x1, mod3, w_col, g_final, y):
    tm = COMB_TM
    per_b = SEQ // tm
    return pl.pallas_call(
        _combine_kernel,
        grid_spec=pltpu.PrefetchScalarGridSpec(
            num_scalar_prefetch=1,
            grid=(N_TOK // tm,),
            in_specs=[pl.BlockSpec((tm, D_MODEL), lambda i, pos: (i, 0)),
                      pl.BlockSpec((None, 1, D_MODEL), lambda i, pos: (i // per_b, 0, 5)),
                      pl.BlockSpec((tm, 8), lambda i, pos: (i, 0)),
                      pl.BlockSpec((1, D_MODEL), lambda i, pos: (0, 0)),
                      pl.BlockSpec(memory_space=pl.ANY)],
            out_specs=pl.BlockSpec((tm, D_MODEL), lambda i, pos: (i, 0)),
            scratch_shapes=[pltpu.VMEM((TOP_K, tm, D_MODEL), f32),
                            pltpu.SemaphoreType.DMA(())]),
        out_shape=jax.ShapeDtypeStruct((N_TOK, D_MODEL), f32),
        compiler_params=_cparams(("arbitrary",)),
        name="combine",
    )(pos_flat, x1, mod3, w_col, g_final, y)


def kernel(x, c, positions, w_ada, b_ada, g_mix, w_in, swa_sinks, diff_lambda_q1, diff_lambda_k1, diff_lambda_q2, diff_lambda_k2, diff_subln_g, w_branch_a, w_branch_b, w_out, g_ffn, w_router_group, b_router_group, w_router_expert, b_router_expert, w_exp_gate_up, w_exp_down, g_final):
    layer = 0
    x2 = x.reshape(N_TOK, D_MODEL)

    mod = _ada(c, w_ada[layer], b_ada[layer])
    mod3 = mod.reshape(BATCH, 1, 6 * D_MODEL)

    w = w_in[layer]
    s_qa, s_ka, s_va, s_qd, s_kd, s_vd, s_ga = np.cumsum([1024, 128, 128, 1024, 1024, 1024, 2048])
    w_in_p = jnp.concatenate(
        [w[:, :s_qa], w[:, s_va:s_qd], w[:, s_qd:s_kd], w[:, s_kd:s_vd], w[:, s_vd:s_ga], w[:, s_ga:],
         w[:, s_qa:s_ka], w[:, s_ka:s_va], jnp.zeros((D_MODEL, PROJ_COLS - w.shape[1]), f32)],
        axis=1).astype(bf16)
    inv_freq = ROPE_THETA ** (-jnp.arange(0, HEAD_DIM, 2, dtype=f32) / HEAD_DIM)
    invf = jnp.tile(inv_freq, LANES // (HEAD_DIM // 2)).reshape(1, LANES)
    pos_col = positions.reshape(N_TOK, 1).astype(i32)

    proj = _proj(x2, mod3, g_mix[layer].reshape(1, D_MODEL), pos_col, invf, w_in_p)
    o_a = _swa(swa_sinks[layer], proj)
    o_b = _diff(diff_lambda_q1[layer].reshape(1, HEAD_DIM), diff_lambda_k1[layer].reshape(1, HEAD_DIM),
                diff_lambda_q2[layer].reshape(1, HEAD_DIM), diff_lambda_k2[layer].reshape(1, HEAD_DIM),
                diff_subln_g[layer].reshape(1, LANES), proj)

    wr = jnp.concatenate([w_router_group[layer], w_router_expert[layer]], axis=1)
    wr_t = jnp.pad(wr.T, ((0, ROUTER_ROWS - wr.shape[1]), (0, 0)))
    br = jnp.concatenate([b_router_group[layer], b_router_expert[layer]])
    br_col = jnp.pad(br, (0, ROUTER_ROWS - br.shape[0])).reshape(ROUTER_ROWS, 1)
    x1, hp, logits_t = _merge(o_a, o_b, proj, x2, mod3, g_ffn[layer].reshape(1, D_MODEL),
                              w_branch_a[layer].astype(bf16), w_branch_b[layer].astype(bf16),
                              w_out[layer].astype(bf16), wr_t, br_col)

    ints, wts, counts = _route(logits_t)
    pos, blk_e, nused = _plan(counts[:, 0], ints)
    pos_flat = pos[0:TOP_K].reshape(TOP_K * N_TOK)
    xs = _dispatch(pos_flat, hp, jnp.zeros((MOE_ROWS, D_MODEL // 2), jnp.uint32))
    act = _gate_up(blk_e, nused, xs, w_exp_gate_up[layer])
    y = _down(blk_e, nused, act, w_exp_down[layer])
    out = _combine(pos_flat, x1, mod3, wts.T, g_final.reshape(1, D_MODEL), y)
    return out.reshape(BATCH, SEQ, D_MODEL)
```

```python
import functools
import math

import jax
import jax.numpy as jnp
import numpy as np
from jax import lax
from jax.experimental import pallas as pl
from jax.experimental.pallas import tpu as pltpu

D_MODEL = 2048
BATCH = 8
SEQ = 2048
N_TOK = BATCH * SEQ
HEAD_DIM = 64
ROPE_THETA = 10000.0
RMS_EPS = 1e-6
NEG_INF = -1e30
SWA_Q_HEADS = 16
SWA_KV_HEADS = 2
SWA_GROUP = SWA_Q_HEADS // SWA_KV_HEADS
WINDOW = 128
DIFF_HEADS = 8
N_GROUPS = 8
EXPERTS_PER_GROUP = 8
N_EXPERTS = N_GROUPS * EXPERTS_PER_GROUP
TOP_K = 2
D_EXPERT = D_MODEL // 2
LAMBDA_INIT = 0.8 - 0.6 * math.exp(-0.3 * 0)

LANES = 128
VMEM_LIMIT = 56 * 1024 * 1024

PROJ_TN = 512
PROJ_COLS = 17 * PROJ_TN
COL_QA, COL_QD, COL_KD, COL_VD, COL_GA, COL_GB, COL_KVA = 0, 1024, 2048, 3072, 4096, 6144, 8192

MOE_TM = 256
MOE_ROWS = N_TOK * TOP_K + N_EXPERTS * MOE_TM
MOE_NBLK = MOE_ROWS // MOE_TM
ROUTE_T = 2048
ROUTER_ROWS = 128

f32 = jnp.float32
bf16 = jnp.bfloat16
i32 = jnp.int32


def _cparams(sem, vmem=VMEM_LIMIT):
    return pltpu.CompilerParams(dimension_semantics=sem, vmem_limit_bytes=vmem)


def _dot(a, b):
    return jnp.dot(a, b, preferred_element_type=f32)


def _dot_nt(a, b):
    return lax.dot_general(a, b, (((1,), (1,)), ((), ())), preferred_element_type=f32)


def _ada_kernel(c_ref, w_ref, b_ref, o_ref):
    c = c_ref[...]
    ca = (c * jax.nn.sigmoid(c)).astype(bf16)
    o_ref[...] = _dot(ca, w_ref[...].astype(bf16)) + b_ref[...]


def _ada(c, w_ada, b_ada):
    tn = 512
    n = w_ada.shape[1]
    return pl.pallas_call(
        _ada_kernel,
        grid=(n // tn,),
        in_specs=[pl.BlockSpec((BATCH, D_MODEL), lambda j: (0, 0)),
                  pl.BlockSpec((D_MODEL, tn), lambda j: (0, j)),
                  pl.BlockSpec((1, tn), lambda j: (0, j))],
        out_specs=pl.BlockSpec((BATCH, tn), lambda j: (0, j)),
        out_shape=jax.ShapeDtypeStruct((BATCH, n), f32),
        compiler_params=_cparams(("arbitrary",)),
        name="ada",
    )(c, w_ada, b_ada.reshape(1, n))


PROJ_TM = 1024


def _rope128(xc, cos, sin_signed, first_half):
    r_prev = pltpu.roll(xc, 32, 1)
    r_next = pltpu.roll(xc, 96, 1)
    return xc * cos + jnp.where(first_half, r_next, r_prev) * sin_signed


def _proj_kernel(x_ref, sc_ref, sh_ref, g_ref, pos_ref, invf_ref, w_ref, o_ref, h_scr, cos_scr, sin_scr):
    j = pl.program_id(1)
    lane = lax.broadcasted_iota(i32, (1, LANES), 1)
    first_half = (lane % HEAD_DIM) < (HEAD_DIM // 2)

    @pl.when(j == 0)
    def _():
        rc = 256
        for r in range(PROJ_TM // rc):
            x = x_ref[r * rc:(r + 1) * rc, :]
            ms = jnp.mean(x * x, axis=-1, keepdims=True)
            y = x * lax.rsqrt(ms + RMS_EPS) * g_ref[...]
            h = y * (1.0 + sc_ref[...]) + sh_ref[...]
            h_scr[r * rc:(r + 1) * rc, :] = h.astype(bf16)
        ang = pos_ref[...].astype(f32) * invf_ref[...]
        cos_scr[...] = jnp.cos(ang)
        sin_scr[...] = jnp.sin(ang) * jnp.where(first_half, -1.0, 1.0)

    acc = _dot(h_scr[...], w_ref[...])
    nch = PROJ_TN // LANES

    def rope_tile(scale):
        for c in range(nch):
            xc = acc[:, c * LANES:(c + 1) * LANES]
            o_ref[:, c * LANES:(c + 1) * LANES] = (
                _rope128(xc, cos_scr[...], sin_scr[...], first_half) * scale).astype(bf16)

    @pl.when(j < 4)
    def _():
        rope_tile(1.0 / math.sqrt(HEAD_DIM))

    @pl.when((j >= 4) & (j < 6))
    def _():
        rope_tile(1.0)

    @pl.when((j >= 6) & (j < 8))
    def _():
        o_ref[...] = acc.astype(bf16)

    @pl.when((j >= 8) & (j < 16))
    def _():
        o_ref[...] = jax.nn.sigmoid(acc).astype(bf16)

    @pl.when(j == 16)
    def _():
        o_ref[:, 0:LANES] = _rope128(acc[:, 0:LANES], cos_scr[...], sin_scr[...], first_half).astype(bf16)
        o_ref[:, LANES:] = acc[:, LANES:].astype(bf16)


def _proj(x2, mod3, g_mix, pos_col, invf, w_in_p):
    tm = PROJ_TM
    per_b = SEQ // tm
    return pl.pallas_call(
        _proj_kernel,
        grid=(N_TOK // tm, PROJ_COLS // PROJ_TN),
        in_specs=[pl.BlockSpec((tm, D_MODEL), lambda i, j: (i, 0)),
                  pl.BlockSpec((None, 1, D_MODEL), lambda i, j: (i // per_b, 0, 1)),
                  pl.BlockSpec((None, 1, D_MODEL), lambda i, j: (i // per_b, 0, 0)),
                  pl.BlockSpec((1, D_MODEL), lambda i, j: (0, 0)),
                  pl.BlockSpec((tm, 1), lambda i, j: (i, 0)),
                  pl.BlockSpec((1, LANES), lambda i, j: (0, 0)),
                  pl.BlockSpec((D_MODEL, PROJ_TN), lambda i, j: (0, j))],
        out_specs=pl.BlockSpec((tm, PROJ_TN), lambda i, j: (i, j)),
        out_shape=jax.ShapeDtypeStruct((N_TOK, PROJ_COLS), bf16),
        scratch_shapes=[pltpu.VMEM((tm, D_MODEL), bf16),
                        pltpu.VMEM((tm, LANES), f32),
                        pltpu.VMEM((tm, LANES), f32)],
        compiler_params=_cparams(("arbitrary", "arbitrary")),
        name="proj",
    )(x2, mod3, mod3, g_mix, pos_col, invf, w_in_p)


def _swa_kernel(sink_ref, q_ref, kv_ref, kvp_ref, o_ref):
    n = pl.program_id(1)
    qi = lax.broadcasted_iota(i32, (WINDOW, 2 * WINDOW), 0)
    kj = lax.broadcasted_iota(i32, (WINDOW, 2 * WINDOW), 1)
    rel = qi + WINDOW - kj
    first_key = jnp.where(n > 0, 0, WINDOW)
    valid = (rel >= 0) & (rel < WINDOW) & (kj >= first_key)
    for kvh in range(SWA_KV_HEADS):
        ks = slice(kvh * HEAD_DIM, (kvh + 1) * HEAD_DIM)
        vs = slice(LANES + kvh * HEAD_DIM, LANES + (kvh + 1) * HEAD_DIM)
        kk = jnp.concatenate([kvp_ref[:, ks], kv_ref[:, ks]], axis=0)
        vv = jnp.concatenate([kvp_ref[:, vs], kv_ref[:, vs]], axis=0)
        for g in range(SWA_GROUP):
            head = kvh * SWA_GROUP + g
            hs = slice(head * HEAD_DIM, (head + 1) * HEAD_DIM)
            s = _dot_nt(q_ref[:, hs], kk)
            s = jnp.where(valid, s, NEG_INF)
            sink = sink_ref[head]
            m = jnp.maximum(jnp.max(s, axis=-1, keepdims=True), sink)
            p = jnp.exp(s - m)
            den = jnp.sum(p, axis=-1, keepdims=True) + jnp.exp(sink - m)
            o = _dot(p.astype(bf16), vv) / den
            o_ref[:, hs] = o.astype(bf16)


def _swa(sinks, proj):
    nb = SEQ // WINDOW
    kv_col = COL_KVA // (2 * LANES)
    return pl.pallas_call(
        _swa_kernel,
        grid=(BATCH, nb),
        in_specs=[pl.BlockSpec(memory_space=pltpu.SMEM),
                  pl.BlockSpec((WINDOW, SWA_Q_HEADS * HEAD_DIM), lambda b, n: (b * nb + n, 0)),
                  pl.BlockSpec((WINDOW, 2 * LANES), lambda b, n: (b * nb + n, kv_col)),
                  pl.BlockSpec((WINDOW, 2 * LANES), lambda b, n: (jnp.maximum(b * nb + n - 1, 0), kv_col))],
        out_specs=pl.BlockSpec((WINDOW, SWA_Q_HEADS * HEAD_DIM), lambda b, n: (b * nb + n, 0)),
        out_shape=jax.ShapeDtypeStruct((N_TOK, SWA_Q_HEADS * HEAD_DIM), bf16),
        compiler_params=_cparams(("arbitrary", "arbitrary")),
        name="swa",
    )(sinks, proj, proj, proj)


DIFF_T = 256


def _diff_kernel(lq1_ref, lk1_ref, lq2_ref, lk2_ref, gcol_ref, q_ref, k_ref, v_ref, o_ref, vt_scr):
    t = DIFF_T
    nt = SEQ // t
    for j in range(nt):
        vt_scr[:, j * t:(j + 1) * t] = v_ref[j * t:(j + 1) * t, :].T
    lam = (jnp.exp(jnp.sum(lq1_ref[...] * lk1_ref[...], axis=-1, keepdims=True))
           - jnp.exp(jnp.sum(lq2_ref[...] * lk2_ref[...], axis=-1, keepdims=True)) + LAMBDA_INIT)
    lane = lax.broadcasted_iota(i32, (t, LANES), 1)
    causal = lax.broadcasted_iota(i32, (t, t), 0) <= lax.broadcasted_iota(i32, (t, t), 1)

    def one_map(qz, qi):
        lo = qi * t
        s_d = jnp.where(causal, _dot_nt(k_ref[lo:lo + t, :], qz), NEG_INF)
        m = jnp.max(s_d, axis=0, keepdims=True)
        if qi > 0:
            s_u = _dot_nt(k_ref[0:lo, :], qz)
            m = jnp.maximum(m, jnp.max(s_u, axis=0, keepdims=True))
        p_d = jnp.exp(s_d - m)
        l = jnp.sum(p_d, axis=0, keepdims=True)
        acc = _dot(vt_scr[:, lo:lo + t], p_d.astype(bf16))
        if qi > 0:
            p_u = jnp.exp(s_u - m)
            l = l + jnp.sum(p_u, axis=0, keepdims=True)
            acc = acc + _dot(vt_scr[:, 0:lo], p_u.astype(bf16))
        return acc / l

    for qi in range(nt):
        q = q_ref[qi * t:(qi + 1) * t, :]
        zero = jnp.zeros_like(q)
        o1 = one_map(jnp.where(lane < HEAD_DIM, q, zero), qi)
        o2 = one_map(jnp.where(lane >= HEAD_DIM, q, zero), qi)
        o = o1 - lam * o2
        y = o * lax.rsqrt(jnp.mean(o * o, axis=0, keepdims=True) + RMS_EPS) * gcol_ref[...]
        o_ref[qi * t:(qi + 1) * t, :] = (y * (1.0 - LAMBDA_INIT)).T.astype(bf16)


def _diff(lq1, lk1, lq2, lk2, subln_gcol, proj):
    vec = lambda n: pl.BlockSpec((1, n), lambda b, h: (0, 0))
    return pl.pallas_call(
        _diff_kernel,
        grid=(BATCH, DIFF_HEADS),
        in_specs=[vec(HEAD_DIM), vec(HEAD_DIM), vec(HEAD_DIM), vec(HEAD_DIM),
                  pl.BlockSpec((LANES, 1), lambda b, h: (0, 0)),
                  pl.BlockSpec((SEQ, LANES), lambda b, h: (b, COL_QD // LANES + h)),
                  pl.BlockSpec((SEQ, LANES), lambda b, h: (b, COL_KD // LANES + h)),
                  pl.BlockSpec((SEQ, LANES), lambda b, h: (b, COL_VD // LANES + h))],
        out_specs=pl.BlockSpec((SEQ, LANES), lambda b, h: (b, h)),
        out_shape=jax.ShapeDtypeStruct((N_TOK, DIFF_HEADS * LANES), bf16),
        scratch_shapes=[pltpu.VMEM((LANES, SEQ), bf16)],
        compiler_params=_cparams(("arbitrary", "arbitrary")),
        name="diff",
    )(lq1, lk1, lq2, lk2, subln_gcol, proj, proj, proj)


MERGE_TM = 256


def _split_bf16(v):
    hi = v.astype(bf16)
    lo = (v - hi.astype(f32)).astype(bf16)
    return hi, lo


def _merge_kernel(oa_ref, ob_ref, sga_ref, sgb_ref, x_ref, gt_ref, sc_ref, sh_ref, g_ref,
                  wa_ref, wb_ref, wo_ref, wr_ref, br_ref, x1_ref, hp_ref, lt_ref):
    a = _dot(oa_ref[...], wa_ref[...])
    b = _dot(ob_ref[...], wb_ref[...])
    merged = sga_ref[...].astype(f32) * a + sgb_ref[...].astype(f32) * b
    y = _dot(merged.astype(bf16), wo_ref[...])
    x1 = x_ref[...] + gt_ref[...] * y
    x1_ref[...] = x1
    ms = jnp.mean(x1 * x1, axis=-1, keepdims=True)
    h2 = (x1 * lax.rsqrt(ms + RMS_EPS) * g_ref[...]) * (1.0 + sc_ref[...]) + sh_ref[...]
    hb = lax.bitcast_convert_type(h2.astype(bf16).astype(f32), jnp.uint32)
    half = D_MODEL // 2
    hp_ref[...] = (hb[:, :half] >> 16) | (hb[:, half:] & jnp.uint32(0xFFFF0000))
    h_hi, h_lo = _split_bf16(h2)
    w_hi, w_lo = _split_bf16(wr_ref[...])
    lt_ref[...] = _dot_nt(w_hi, h_hi) + _dot_nt(w_hi, h_lo) + _dot_nt(w_lo, h_hi) + br_ref[...]


def _merge(o_a, o_b, proj, x2, mod3, g_ffn, wa, wb, wo, wr_t, br_col):
    tm = MERGE_TM
    per_b = SEQ // tm
    const = lambda shape: pl.BlockSpec(shape, lambda i: (0,) * len(shape), pipeline_mode=pl.Buffered(1))
    modspec = lambda chunk: pl.BlockSpec((None, 1, D_MODEL), lambda i: (i // per_b, 0, chunk))
    return pl.pallas_call(
        _merge_kernel,
        grid=(N_TOK // tm,),
        in_specs=[pl.BlockSpec((tm, 1024), lambda i: (i, 0)),
                  pl.BlockSpec((tm, 1024), lambda i: (i, 0)),
                  pl.BlockSpec((tm, D_MODEL), lambda i: (i, COL_GA // D_MODEL)),
                  pl.BlockSpec((tm, D_MODEL), lambda i: (i, COL_GB // D_MODEL)),
                  pl.BlockSpec((tm, D_MODEL), lambda i: (i, 0)),
                  modspec(2), modspec(4), modspec(3),
                  const((1, D_MODEL)),
                  const((1024, D_MODEL)), const((1024, D_MODEL)), const((D_MODEL, D_MODEL)),
                  const((ROUTER_ROWS, D_MODEL)), const((ROUTER_ROWS, 1))],
        out_specs=[pl.BlockSpec((tm, D_MODEL), lambda i: (i, 0)),
                   pl.BlockSpec((tm, D_MODEL // 2), lambda i: (i, 0)),
                   pl.BlockSpec((ROUTER_ROWS, tm), lambda i: (0, i))],
        out_shape=[jax.ShapeDtypeStruct((N_TOK, D_MODEL), f32),
                   jax.ShapeDtypeStruct((N_TOK, D_MODEL // 2), jnp.uint32),
                   jax.ShapeDtypeStruct((ROUTER_ROWS, N_TOK), f32)],
        compiler_params=_cparams(("arbitrary",)),
        name="merge",
    )(o_a, o_b, proj, proj, x2, mod3, mod3, mod3, g_ffn, wa, wb, wo, wr_t, br_col)


def _first_argmax(v, iota_f):
    vmax = jnp.max(v, axis=0, keepdims=True)
    idx = jnp.min(jnp.where(v == vmax, iota_f, float(v.shape[0])), axis=0, keepdims=True)
    return vmax, idx


def _route_kernel(lt_ref, ints_ref, w_ref, cnt_ref, carry):
    @pl.when(pl.program_id(0) == 0)
    def _():
        carry[...] = jnp.zeros_like(carry)

    cw = 256
    io8 = lax.broadcasted_iota(i32, (N_GROUPS, cw), 0).astype(f32)
    io64 = lax.broadcasted_iota(i32, (N_EXPERTS, cw), 0).astype(f32)
    tri = jnp.where(lax.broadcasted_iota(i32, (cw, cw), 0) < lax.broadcasted_iota(i32, (cw, cw), 1),
                    1.0, 0.0).astype(bf16)
    ints_ref[4:8, :] = jnp.zeros((4, ROUTE_T), i32)
    w_ref[2:8, :] = jnp.zeros((6, ROUTE_T), f32)
    for c in range(ROUTE_T // cw):
        sl = slice(c * cw, (c + 1) * cw)
        gl = lt_ref[0:N_GROUPS, sl]
        gmax, gsel = _first_argmax(gl, io8)
        gw = 1.0 / jnp.sum(jnp.exp(gl - gmax), axis=0, keepdims=True)
        es = jnp.zeros((EXPERTS_PER_GROUP, cw), f32)
        for g in range(N_GROUPS):
            lo = N_GROUPS + g * EXPERTS_PER_GROUP
            es = jnp.where(gsel == float(g), lt_ref[lo:lo + EXPERTS_PER_GROUP, sl], es)
        v1, i1 = _first_argmax(es, io8)
        es2 = jnp.where(io8 == i1, -jnp.inf, es)
        v2, i2 = _first_argmax(es2, io8)
        e = jnp.exp(v2 - v1)
        w_ref[0:1, sl] = gw / (1.0 + e)
        w_ref[1:2, sl] = gw * e / (1.0 + e)
        e1 = gsel * float(EXPERTS_PER_GROUP) + i1
        e2 = gsel * float(EXPERTS_PER_GROUP) + i2
        oh1 = io64 == e1
        oh2 = io64 == e2
        oc = jnp.where(oh1, 1.0, jnp.where(oh2, 1.0, 0.0))
        pre = _dot(oc.astype(bf16), tri) + carry[...]
        ints_ref[0:1, sl] = e1.astype(i32)
        ints_ref[1:2, sl] = e2.astype(i32)
        ints_ref[2:3, sl] = jnp.sum(jnp.where(oh1, pre, 0.0), axis=0, keepdims=True).astype(i32)
        ints_ref[3:4, sl] = jnp.sum(jnp.where(oh2, pre, 0.0), axis=0, keepdims=True).astype(i32)
        carry[...] = carry[...] + jnp.sum(oc, axis=1, keepdims=True)
    cnt_ref[...] = jnp.broadcast_to(carry[...], (N_EXPERTS, LANES)).astype(i32)


def _route(logits_t):
    t = ROUTE_T
    return pl.pallas_call(
        _route_kernel,
        grid=(N_TOK // t,),
        in_specs=[pl.BlockSpec((ROUTER_ROWS, t), lambda c: (0, c))],
        out_specs=[pl.BlockSpec((8, t), lambda c: (0, c)),
                   pl.BlockSpec((8, t), lambda c: (0, c)),
                   pl.BlockSpec((N_EXPERTS, LANES), lambda c: (0, 0))],
        out_shape=[jax.ShapeDtypeStruct((8, N_TOK), i32),
                   jax.ShapeDtypeStruct((8, N_TOK), f32),
                   jax.ShapeDtypeStruct((N_EXPERTS, LANES), i32)],
        scratch_shapes=[pltpu.VMEM((N_EXPERTS, 1), f32)],
        compiler_params=_cparams(("arbitrary",)),
        name="route",
    )(logits_t)


def _plan_kernel(cnt_ref, ints_ref, pos_ref, blk_u_ref, used_e_ref, meta_ref, pstart_s):
    @pl.when(pl.program_id(0) == 0)
    def _():
        def per_expert(e, carry):
            b, u = carry
            nb = (cnt_ref[e] + (MOE_TM - 1)) // MOE_TM
            pstart_s[e] = b * MOE_TM
            used_e_ref[u] = e

            def mark(k, c):
                blk_u_ref[b + k] = u
                return c

            lax.fori_loop(0, nb, mark, 0)
            return b + nb, u + jnp.minimum(nb, 1)

        nb_total, nu = lax.fori_loop(0, N_EXPERTS, per_expert, (0, 0))
        meta_ref[0] = nb_total
        meta_ref[1] = nu
        last_e = used_e_ref[nu - 1]

        def fill_e(k, c):
            used_e_ref[k] = last_e
            return c

        lax.fori_loop(nu, N_EXPERTS, fill_e, 0)

        def fill_b(k, c):
            blk_u_ref[k] = nu - 1
            return c

        lax.fori_loop(nb_total, MOE_NBLK, fill_b, 0)

    t = ROUTE_T
    iocol = lax.broadcasted_iota(i32, (N_EXPERTS, 1), 0)

    def build(e, col):
        return jnp.where(iocol == e, pstart_s[e], col)

    pcol = lax.fori_loop(0, N_EXPERTS, build, jnp.zeros((N_EXPERTS, 1), i32)).astype(f32)
    io64 = lax.broadcasted_iota(i32, (N_EXPERTS, t), 0)
    e1 = ints_ref[0:1, :]
    e2 = ints_ref[1:2, :]
    pos_ref[0:1, :] = (jnp.sum(jnp.where(io64 == e1, pcol, 0.0), axis=0, keepdims=True).astype(i32)
                       + ints_ref[2:3, :])
    pos_ref[1:2, :] = (jnp.sum(jnp.where(io64 == e2, pcol, 0.0), axis=0, keepdims=True).astype(i32)
                       + ints_ref[3:4, :])
    pos_ref[2:8, :] = jnp.zeros((6, t), i32)


def _plan(counts, ints):
    t = ROUTE_T
    return pl.pallas_call(
        _plan_kernel,
        grid_spec=pltpu.PrefetchScalarGridSpec(
            num_scalar_prefetch=1,
            grid=(N_TOK // t,),
            in_specs=[pl.BlockSpec((8, t), lambda c, cnt: (0, c))],
            out_specs=[pl.BlockSpec((8, t), lambda c, cnt: (0, c)),
                       pl.BlockSpec(memory_space=pltpu.SMEM),
                       pl.BlockSpec(memory_space=pltpu.SMEM),
                       pl.BlockSpec(memory_space=pltpu.SMEM)],
            scratch_shapes=[pltpu.SMEM((N_EXPERTS,), i32)]),
        out_shape=[jax.ShapeDtypeStruct((8, N_TOK), i32),
                   jax.ShapeDtypeStruct((MOE_NBLK,), i32),
                   jax.ShapeDtypeStruct((N_EXPERTS,), i32),
                   jax.ShapeDtypeStruct((2,), i32)],
        compiler_params=_cparams(("arbitrary",)),
        name="plan",
    )(counts, ints)


DISP_CHUNK = 512
DMA_UNROLL = 8


def _dispatch_kernel(pos_ref, hp_ref, xs_in_ref, xs_ref, sem):
    del xs_in_ref
    base = pl.program_id(0) * DISP_CHUNK

    def row_copy(r, k):
        dst = pos_ref[k * N_TOK + base + r]
        return pltpu.make_async_copy(hp_ref.at[pl.ds(r, 1), :], xs_ref.at[pl.ds(dst, 1), :], sem)

    def issue(g, carry):
        for u in range(DMA_UNROLL):
            r = g * DMA_UNROLL + u
            row_copy(r, 0).start()
            row_copy(r, 1).start()
        return carry

    lax.fori_loop(0, DISP_CHUNK // DMA_UNROLL, issue, 0)

    def drain(g, carry):
        for _ in range(DMA_UNROLL):
            row_copy(0, 0).wait()
            row_copy(0, 0).wait()
        return carry

    lax.fori_loop(0, DISP_CHUNK // DMA_UNROLL, drain, 0)


def _dispatch(pos_flat, hp, xs_zero):
    return pl.pallas_call(
        _dispatch_kernel,
        grid_spec=pltpu.PrefetchScalarGridSpec(
            num_scalar_prefetch=1,
            grid=(N_TOK // DISP_CHUNK,),
            in_specs=[pl.BlockSpec((DISP_CHUNK, D_MODEL // 2), lambda i, pos: (i, 0)),
                      pl.BlockSpec(memory_space=pl.ANY)],
            out_specs=pl.BlockSpec(memory_space=pl.ANY),
            scratch_shapes=[pltpu.SemaphoreType.DMA(())]),
        out_shape=jax.ShapeDtypeStruct((MOE_ROWS, D_MODEL // 2), jnp.uint32),
        input_output_aliases={2: 0},
        compiler_params=_cparams(("arbitrary",)),
        name="dispatch",
    )(pos_flat, hp, xs_zero)


MOE_KC = 512
N_GU_CH = D_MODEL // MOE_KC
N_DN_CH = D_EXPERT // MOE_KC
N_CH = N_GU_CH + N_DN_CH


def _moe_kernel(blk_u, used_e, meta, xs_ref, wgu_hbm, wdn_hbm, y_ref, ring, wgu_b, wdn_b, sems):
    i = pl.program_id(0)
    u = blk_u[i]
    first = (i == 0) | (u != blk_u[jnp.maximum(i - 1, 0)])
    active = i < meta[0]

    def chunk_copy(e, c):
        if c < N_GU_CH:
            src = wgu_hbm.at[e, pl.ds(c * MOE_KC, MOE_KC), :]
        else:
            src = wdn_hbm.at[e, pl.ds((c - N_GU_CH) * MOE_KC, MOE_KC), :]
        return pltpu.make_async_copy(src, ring.at[c], sems.at[c])

    @pl.when(i == 0)
    def _():
        for c in range(N_CH):
            chunk_copy(used_e[0], c).start()

    def unpack_x():
        packed = xs_ref[...]
        x_lo = lax.bitcast_convert_type(packed << 16, f32).astype(bf16)
        x_hi = lax.bitcast_convert_type(packed & jnp.uint32(0xFFFF0000), f32).astype(bf16)
        return x_lo, x_hi

    def swiglu(gu):
        g = gu[:, :D_EXPERT]
        return (g * jax.nn.sigmoid(g) * gu[:, D_EXPERT:]).astype(bf16)

    def rows(c):
        return slice(c * MOE_KC, (c + 1) * MOE_KC)

    @pl.when(active & first)
    def _():
        e_next = used_e[jnp.minimum(u + 1, meta[1] - 1)]

        def take(c):
            chunk_copy(e_next, c).wait()
            wb = ring[c].astype(bf16)
            if c < N_GU_CH:
                wgu_b[rows(c), :] = wb
            else:
                wdn_b[rows(c - N_GU_CH), :] = wb
            chunk_copy(e_next, c).start()

        x_lo, x_hi = unpack_x()
        per_half = N_GU_CH // 2
        x_chunks = [xh[:, rows(c)] for xh in (x_lo, x_hi) for c in range(per_half)]
        take(0)
        gu = None
        for c in range(N_GU_CH):
            take(c + 1)
            d = _dot(x_chunks[c], wgu_b[rows(c), :])
            gu = d if gu is None else gu + d
        a = swiglu(gu)
        y = None
        for c in range(N_DN_CH):
            if c + 1 < N_DN_CH:
                take(N_GU_CH + c + 1)
            d = _dot(a[:, rows(c)], wdn_b[rows(c), :])
            y = d if y is None else y + d
        y_ref[...] = y

    @pl.when(active & jnp.logical_not(first))
    def _():
        x_lo, x_hi = unpack_x()
        half = D_MODEL // 2
        gu = _dot(x_lo, wgu_b[0:half, :]) + _dot(x_hi, wgu_b[half:, :])
        y_ref[...] = _dot(swiglu(gu), wdn_b[...])

    @pl.when(i == pl.num_programs(0) - 1)
    def _():
        for c in range(N_CH):
            chunk_copy(used_e[0], c).wait()


def _moe(blk_u, used_e, meta, xs, w_gu, w_dn):
    tm = MOE_TM
    blk = lambda i, bu, ue, mt: (jnp.minimum(i, mt[0] - 1), 0)
    return pl.pallas_call(
        _moe_kernel,
        grid_spec=pltpu.PrefetchScalarGridSpec(
            num_scalar_prefetch=3,
            grid=(MOE_NBLK,),
            in_specs=[pl.BlockSpec((tm, D_MODEL // 2), blk),
                      pl.BlockSpec(memory_space=pl.ANY),
                      pl.BlockSpec(memory_space=pl.ANY)],
            out_specs=pl.BlockSpec((tm, D_MODEL), blk),
            scratch_shapes=[pltpu.VMEM((N_CH, MOE_KC, D_MODEL), f32),
                            pltpu.VMEM((D_MODEL, 2 * D_EXPERT), bf16),
                            pltpu.VMEM((D_EXPERT, D_MODEL), bf16),
                            pltpu.SemaphoreType.DMA((N_CH,))]),
        out_shape=jax.ShapeDtypeStruct((MOE_ROWS, D_MODEL), f32),
        compiler_params=_cparams(("arbitrary",)),
        name="moe",
    )(blk_u, used_e, meta, xs, w_gu, w_dn)


COMB_TM = 256


def _combine_kernel(pos_ref, x1_ref, gt_ref, wc_ref, g_ref, y_ref, o_ref, ybuf, sems):
    i = pl.program_id(0)
    slot = i % 2

    def row_copy(tile, r, k, s):
        src = pos_ref[k * N_TOK + tile * COMB_TM + r]
        return pltpu.make_async_copy(y_ref.at[pl.ds(src, 1), :], ybuf.at[s, k, pl.ds(r, 1), :], sems.at[s])

    def issue_tile(tile, s):
        def body(g, carry):
            for u in range(DMA_UNROLL):
                r = g * DMA_UNROLL + u
                row_copy(tile, r, 0, s).start()
                row_copy(tile, r, 1, s).start()
            return carry
        lax.fori_loop(0, COMB_TM // DMA_UNROLL, body, 0)

    @pl.when(i == 0)
    def _():
        issue_tile(0, 0)

    @pl.when(i + 1 < pl.num_programs(0))
    def _():
        issue_tile(i + 1, 1 - slot)

    def drain(g, carry):
        for _ in range(DMA_UNROLL):
            row_copy(0, 0, 0, slot).wait()
            row_copy(0, 0, 0, slot).wait()
        return carry

    lax.fori_loop(0, COMB_TM // DMA_UNROLL, drain, 0)

    ffn = wc_ref[:, 0:1] * ybuf[slot, 0] + wc_ref[:, 1:2] * ybuf[slot, 1]
    xo = x1_ref[...] + gt_ref[...] * ffn
    ms = jnp.mean(xo * xo, axis=-1, keepdims=True)
    o_ref[...] = xo * lax.rsqrt(ms + RMS_EPS) * g_ref[...]


def _combine(pos_flat, x1, mod3, w_col, g_final, y):
    tm = COMB_TM
    per_b = SEQ // tm
    return pl.pallas_call(
        _combine_kernel,
        grid_spec=pltpu.PrefetchScalarGridSpec(
            num_scalar_prefetch=1,
            grid=(N_TOK // tm,),
            in_specs=[pl.BlockSpec((tm, D_MODEL), lambda i, pos: (i, 0)),
                      pl.BlockSpec((None, 1, D_MODEL), lambda i, pos: (i // per_b, 0, 5)),
                      pl.BlockSpec((tm, 8), lambda i, pos: (i, 0)),
                      pl.BlockSpec((1, D_MODEL), lambda i, pos: (0, 0)),
                      pl.BlockSpec(memory_space=pl.ANY)],
            out_specs=pl.BlockSpec((tm, D_MODEL), lambda i, pos: (i, 0)),
            scratch_shapes=[pltpu.VMEM((2, TOP_K, tm, D_MODEL), f32),
                            pltpu.SemaphoreType.DMA((2,))]),
        out_shape=jax.ShapeDtypeStruct((N_TOK, D_MODEL), f32),
        compiler_params=_cparams(("arbitrary",)),
        name="combine",
    )(pos_flat, x1, mod3, w_col, g_final, y)


def kernel(x, c, positions, w_ada, b_ada, g_mix, w_in, swa_sinks, diff_lambda_q1, diff_lambda_k1, diff_lambda_q2, diff_lambda_k2, diff_subln_g, w_branch_a, w_branch_b, w_out, g_ffn, w_router_group, b_router_group, w_router_expert, b_router_expert, w_exp_gate_up, w_exp_down, g_final):
    layer = 0
    x2 = x.reshape(N_TOK, D_MODEL)

    mod = _ada(c, w_ada[layer], b_ada[layer])
    mod3 = mod.reshape(BATCH, 1, 6 * D_MODEL)

    w = w_in[layer]
    s_qa, s_ka, s_va, s_qd, s_kd, s_vd, s_ga = np.cumsum([1024, 128, 128, 1024, 1024, 1024, 2048])
    w_in_p = jnp.concatenate(
        [w[:, :s_qa], w[:, s_va:s_qd], w[:, s_qd:s_kd], w[:, s_kd:s_vd], w[:, s_vd:s_ga], w[:, s_ga:],
         w[:, s_qa:s_ka], w[:, s_ka:s_va], jnp.zeros((D_MODEL, PROJ_COLS - w.shape[1]), f32)],
        axis=1).astype(bf16)
    inv_freq = ROPE_THETA ** (-jnp.arange(0, HEAD_DIM, 2, dtype=f32) / HEAD_DIM)
    invf = jnp.tile(inv_freq, LANES // (HEAD_DIM // 2)).reshape(1, LANES)
    pos_col = positions.reshape(N_TOK, 1).astype(i32)

    proj = _proj(x2, mod3, g_mix[layer].reshape(1, D_MODEL), pos_col, invf, w_in_p)
    o_a = _swa(swa_sinks[layer], proj)
    o_b = _diff(diff_lambda_q1[layer].reshape(1, HEAD_DIM), diff_lambda_k1[layer].reshape(1, HEAD_DIM),
                diff_lambda_q2[layer].reshape(1, HEAD_DIM), diff_lambda_k2[layer].reshape(1, HEAD_DIM),
                diff_subln_g[layer].reshape(LANES, 1), proj)

    wr = jnp.concatenate([w_router_group[layer], w_router_expert[layer]], axis=1)
    wr_t = jnp.pad(wr.T, ((0, ROUTER_ROWS - wr.shape[1]), (0, 0)))
    br = jnp.concatenate([b_router_group[layer], b_router_expert[layer]])
    br_col = jnp.pad(br, (0, ROUTER_ROWS - br.shape[0])).reshape(ROUTER_ROWS, 1)
    x1, hp, logits_t = _merge(o_a, o_b, proj, x2, mod3, g_ffn[layer].reshape(1, D_MODEL),
                              w_branch_a[layer].astype(bf16), w_branch_b[layer].astype(bf16),
                              w_out[layer].astype(bf16), wr_t, br_col)

    ints, wts, counts = _route(logits_t)
    pos, blk_u, used_e, meta = _plan(counts[:, 0], ints)
    pos_flat = pos[0:TOP_K].reshape(TOP_K * N_TOK)
    xs = _dispatch(pos_flat, hp, jnp.zeros((MOE_ROWS, D_MODEL // 2), jnp.uint32))
    y = _moe(blk_u, used_e, meta, xs, w_exp_gate_up[layer], w_exp_down[layer])
    out = _combine(pos_flat, x1, mod3, wts.T, g_final.reshape(1, D_MODEL), y)
    return out.reshape(BATCH, SEQ, D_MODEL)
```

```python
import functools
import math

import jax
import jax.numpy as jnp
import numpy as np
from jax import lax
from jax.experimental import pallas as pl
from jax.experimental.pallas import tpu as pltpu

D_MODEL = 2048
BATCH = 8
SEQ = 2048
N_TOK = BATCH * SEQ
HEAD_DIM = 64
ROPE_THETA = 10000.0
RMS_EPS = 1e-6
NEG_INF = -1e30
SWA_Q_HEADS = 16
SWA_KV_HEADS = 2
SWA_GROUP = SWA_Q_HEADS // SWA_KV_HEADS
WINDOW = 128
DIFF_HEADS = 8
N_GROUPS = 8
EXPERTS_PER_GROUP = 8
N_EXPERTS = N_GROUPS * EXPERTS_PER_GROUP
TOP_K = 2
D_EXPERT = D_MODEL // 2
LAMBDA_INIT = 0.8 - 0.6 * math.exp(-0.3 * 0)

LANES = 128
VMEM_LIMIT = 56 * 1024 * 1024

PROJ_TN = 512
PROJ_COLS = 17 * PROJ_TN
COL_QA, COL_QD, COL_KD, COL_VD, COL_GA, COL_GB, COL_KVA = 0, 1024, 2048, 3072, 4096, 6144, 8192

MOE_TM = 256
MOE_ROWS = N_TOK * TOP_K + N_EXPERTS * MOE_TM
MOE_NBLK = MOE_ROWS // MOE_TM
ROUTE_T = 2048
ROUTER_ROWS = 128

f32 = jnp.float32
bf16 = jnp.bfloat16
i32 = jnp.int32


def _cparams(sem, vmem=VMEM_LIMIT):
    return pltpu.CompilerParams(dimension_semantics=sem, vmem_limit_bytes=vmem)


def _dot(a, b):
    return jnp.dot(a, b, preferred_element_type=f32)


def _dot_nt(a, b):
    return lax.dot_general(a, b, (((1,), (1,)), ((), ())), preferred_element_type=f32)


def _ada_kernel(c_ref, w_ref, b_ref, o_ref):
    c = c_ref[...]
    ca = (c * jax.nn.sigmoid(c)).astype(bf16)
    o_ref[...] = _dot(ca, w_ref[...].astype(bf16)) + b_ref[...]


def _ada(c, w_ada, b_ada):
    tn = 512
    n = w_ada.shape[1]
    return pl.pallas_call(
        _ada_kernel,
        grid=(n // tn,),
        in_specs=[pl.BlockSpec((BATCH, D_MODEL), lambda j: (0, 0)),
                  pl.BlockSpec((D_MODEL, tn), lambda j: (0, j)),
                  pl.BlockSpec((1, tn), lambda j: (0, j))],
        out_specs=pl.BlockSpec((BATCH, tn), lambda j: (0, j)),
        out_shape=jax.ShapeDtypeStruct((BATCH, n), f32),
        compiler_params=_cparams(("arbitrary",)),
        name="ada",
    )(c, w_ada, b_ada.reshape(1, n))


PROJ_TM = 1024


def _rope128(xc, cos, sin_signed, first_half):
    r_prev = pltpu.roll(xc, 32, 1)
    r_next = pltpu.roll(xc, 96, 1)
    return xc * cos + jnp.where(first_half, r_next, r_prev) * sin_signed


def _rope_table_kernel(pos_ref, invf_ref, cos_ref, sin_ref):
    ang = pos_ref[...].astype(f32) * invf_ref[...]
    cos_ref[...] = jnp.cos(ang)
    sin_ref[...] = jnp.sin(ang)


def _rope_tables(pos4, invf):
    rows = pos4.shape[0]
    tr = 512
    spec = pl.BlockSpec((tr, LANES), lambda i: (i, 0))
    return pl.pallas_call(
        _rope_table_kernel,
        grid=(rows // tr,),
        in_specs=[spec, pl.BlockSpec((1, LANES), lambda i: (0, 0))],
        out_specs=[spec, spec],
        out_shape=[jax.ShapeDtypeStruct((rows, LANES), f32)] * 2,
        compiler_params=_cparams(("arbitrary",)),
        name="rope_tables",
    )(pos4, invf)


def _proj_kernel(x_ref, sc_ref, sh_ref, g_ref, cos_ref, sin_ref, w_ref, o_ref, h_scr, sin_scr):
    j = pl.program_id(1)
    lane = lax.broadcasted_iota(i32, (1, LANES), 1)
    first_half = (lane % HEAD_DIM) < (HEAD_DIM // 2)

    @pl.when(j == 0)
    def _():
        rc = 256
        for r in range(PROJ_TM // rc):
            x = x_ref[r * rc:(r + 1) * rc, :]
            ms = jnp.mean(x * x, axis=-1, keepdims=True)
            y = x * lax.rsqrt(ms + RMS_EPS) * g_ref[...]
            h = y * (1.0 + sc_ref[...]) + sh_ref[...]
            h_scr[r * rc:(r + 1) * rc, :] = h.astype(bf16)
        sin_scr[...] = sin_ref[...] * jnp.where(first_half, -1.0, 1.0)

    acc = _dot(h_scr[...], w_ref[...])
    nch = PROJ_TN // LANES

    def rope_chunks(chunks, scale):
        for c in chunks:
            xc = acc[:, c * LANES:(c + 1) * LANES]
            o_ref[:, c * LANES:(c + 1) * LANES] = (
                _rope128(xc, cos_ref[...], sin_scr[...], first_half) * scale).astype(bf16)

    @pl.when(j < 4)
    def _():
        rope_chunks(range(nch), 1.0 / math.sqrt(HEAD_DIM))

    @pl.when((j >= 4) & (j < 6))
    def _():
        rope_chunks(range(nch), 1.0)

    @pl.when((j >= 6) & (j < 16))
    def _():
        o_ref[...] = acc.astype(bf16)

    @pl.when(j == 16)
    def _():
        rope_chunks(range(nch // 2), 1.0)
        o_ref[:, PROJ_TN // 2:] = acc[:, PROJ_TN // 2:].astype(bf16)


def _proj(x2, mod3, g_mix, cos_t, sin_t, w_in_p):
    tm = PROJ_TM
    per_b = SEQ // tm
    return pl.pallas_call(
        _proj_kernel,
        grid=(N_TOK // tm, PROJ_COLS // PROJ_TN),
        in_specs=[pl.BlockSpec((tm, D_MODEL), lambda i, j: (i, 0)),
                  pl.BlockSpec((None, 1, D_MODEL), lambda i, j: (i // per_b, 0, 1)),
                  pl.BlockSpec((None, 1, D_MODEL), lambda i, j: (i // per_b, 0, 0)),
                  pl.BlockSpec((1, D_MODEL), lambda i, j: (0, 0)),
                  pl.BlockSpec((tm, LANES), lambda i, j: (i, 0)),
                  pl.BlockSpec((tm, LANES), lambda i, j: (i, 0)),
                  pl.BlockSpec((D_MODEL, PROJ_TN), lambda i, j: (0, j))],
        out_specs=pl.BlockSpec((tm, PROJ_TN), lambda i, j: (i, j)),
        out_shape=jax.ShapeDtypeStruct((N_TOK, PROJ_COLS), bf16),
        scratch_shapes=[pltpu.VMEM((tm, D_MODEL), bf16),
                        pltpu.VMEM((tm, LANES), f32)],
        compiler_params=_cparams(("arbitrary", "arbitrary")),
        name="proj",
    )(x2, mod3, mod3, g_mix, cos_t, sin_t, w_in_p)


def _swa_kernel(sink_ref, q_ref, kv_ref, kvp_ref, o_ref):
    n = pl.program_id(1)
    w = WINDOW
    qi = lax.broadcasted_iota(i32, (w, 2 * w), 0)
    kj = lax.broadcasted_iota(i32, (w, 2 * w), 1)
    rel = qi + w - kj
    first_key = jnp.where(n > 0, 0, w)
    valid = (rel >= 0) & (rel < w) & (kj >= first_key)
    lane = lax.broadcasted_iota(i32, (w, LANES), 1)
    low = lane < HEAD_DIM
    pairs = SWA_GROUP // 2
    for kvh in range(SWA_KV_HEADS):
        ks = slice(kvh * LANES, (kvh + 1) * LANES)
        vs = slice((SWA_KV_HEADS + kvh) * LANES, (SWA_KV_HEADS + kvh + 1) * LANES)
        kk = jnp.concatenate([kvp_ref[:, ks], kv_ref[:, ks]], axis=0)
        vv = jnp.concatenate([kvp_ref[:, vs], kv_ref[:, vs]], axis=0)
        parts = []
        for pair in range(pairs):
            gp = kvh * pairs + pair
            q2 = q_ref[:, gp * LANES:(gp + 1) * LANES]
            zero = jnp.zeros_like(q2)
            parts += [jnp.where(low, q2, zero), jnp.where(low, zero, q2)]
        s_all = _dot_nt(jnp.concatenate(parts, axis=0), kk)
        ps, dens = [], []
        for g in range(SWA_GROUP):
            s = jnp.where(valid, s_all[g * w:(g + 1) * w, :], NEG_INF)
            sink = sink_ref[kvh * SWA_GROUP + g]
            m = jnp.maximum(jnp.max(s, axis=-1, keepdims=True), sink)
            p = jnp.exp(s - m)
            dens.append(jnp.sum(p, axis=-1, keepdims=True) + jnp.exp(sink - m))
            ps.append(p.astype(bf16))
        o_all = _dot(jnp.concatenate(ps, axis=0), vv)
        for pair in range(pairs):
            gp = kvh * pairs + pair
            o_even = o_all[(2 * pair) * w:(2 * pair + 1) * w, :] / dens[2 * pair]
            o_odd = o_all[(2 * pair + 1) * w:(2 * pair + 2) * w, :] / dens[2 * pair + 1]
            o_ref[:, gp * LANES:(gp + 1) * LANES] = jnp.where(low, o_even, o_odd).astype(bf16)


def _swa(sinks, proj):
    nb = SEQ // WINDOW
    kv_col = COL_KVA // (4 * LANES)
    return pl.pallas_call(
        _swa_kernel,
        grid=(BATCH, nb),
        in_specs=[pl.BlockSpec(memory_space=pltpu.SMEM),
                  pl.BlockSpec((WINDOW, SWA_Q_HEADS * HEAD_DIM), lambda b, n: (b * nb + n, 0)),
                  pl.BlockSpec((WINDOW, 4 * LANES), lambda b, n: (b * nb + n, kv_col)),
                  pl.BlockSpec((WINDOW, 4 * LANES), lambda b, n: (jnp.maximum(b * nb + n - 1, 0), kv_col))],
        out_specs=pl.BlockSpec((WINDOW, SWA_Q_HEADS * HEAD_DIM), lambda b, n: (b * nb + n, 0)),
        out_shape=jax.ShapeDtypeStruct((N_TOK, SWA_Q_HEADS * HEAD_DIM), bf16),
        compiler_params=_cparams(("arbitrary", "arbitrary")),
        name="swa",
    )(sinks, proj, proj, proj)


DIFF_T = 256


def _diff_kernel(lq1_ref, lk1_ref, lq2_ref, lk2_ref, gcol_ref, q_ref, k_ref, v_ref, o_ref, vt_scr):
    t = DIFF_T
    nt = SEQ // t
    for j in range(nt):
        vt_scr[:, j * t:(j + 1) * t] = v_ref[j * t:(j + 1) * t, :].T
    lam = (jnp.exp(jnp.sum(lq1_ref[...] * lk1_ref[...], axis=-1, keepdims=True))
           - jnp.exp(jnp.sum(lq2_ref[...] * lk2_ref[...], axis=-1, keepdims=True)) + LAMBDA_INIT)
    lane = lax.broadcasted_iota(i32, (t, LANES), 1)
    causal = lax.broadcasted_iota(i32, (t, t), 0) <= lax.broadcasted_iota(i32, (t, t), 1)

    def one_map(qz, qi):
        lo = qi * t
        s_d = jnp.where(causal, _dot_nt(k_ref[lo:lo + t, :], qz), NEG_INF)
        m = jnp.max(s_d, axis=0, keepdims=True)
        if qi > 0:
            s_u = _dot_nt(k_ref[0:lo, :], qz)
            m = jnp.maximum(m, jnp.max(s_u, axis=0, keepdims=True))
        p_d = jnp.exp(s_d - m)
        l = jnp.sum(p_d, axis=0, keepdims=True)
        acc = _dot(vt_scr[:, lo:lo + t], p_d.astype(bf16))
        if qi > 0:
            p_u = jnp.exp(s_u - m)
            l = l + jnp.sum(p_u, axis=0, keepdims=True)
            acc = acc + _dot(vt_scr[:, 0:lo], p_u.astype(bf16))
        return acc / l

    for qi in range(nt):
        q = q_ref[qi * t:(qi + 1) * t, :]
        zero = jnp.zeros_like(q)
        o1 = one_map(jnp.where(lane < HEAD_DIM, q, zero), qi)
        o2 = one_map(jnp.where(lane >= HEAD_DIM, q, zero), qi)
        o = o1 - lam * o2
        y = o * lax.rsqrt(jnp.mean(o * o, axis=0, keepdims=True) + RMS_EPS) * gcol_ref[...]
        o_ref[qi * t:(qi + 1) * t, :] = (y * (1.0 - LAMBDA_INIT)).T.astype(bf16)


def _diff(lq1, lk1, lq2, lk2, subln_gcol, proj):
    vec = lambda n: pl.BlockSpec((1, n), lambda b, h: (0, 0))
    return pl.pallas_call(
        _diff_kernel,
        grid=(BATCH, DIFF_HEADS),
        in_specs=[vec(HEAD_DIM), vec(HEAD_DIM), vec(HEAD_DIM), vec(HEAD_DIM),
                  pl.BlockSpec((LANES, 1), lambda b, h: (0, 0)),
                  pl.BlockSpec((SEQ, LANES), lambda b, h: (b, COL_QD // LANES + h)),
                  pl.BlockSpec((SEQ, LANES), lambda b, h: (b, COL_KD // LANES + h)),
                  pl.BlockSpec((SEQ, LANES), lambda b, h: (b, COL_VD // LANES + h))],
        out_specs=pl.BlockSpec((SEQ, LANES), lambda b, h: (b, h)),
        out_shape=jax.ShapeDtypeStruct((N_TOK, DIFF_HEADS * LANES), bf16),
        scratch_shapes=[pltpu.VMEM((LANES, SEQ), bf16)],
        compiler_params=_cparams(("arbitrary", "arbitrary")),
        name="diff",
    )(lq1, lk1, lq2, lk2, subln_gcol, proj, proj, proj)


MERGE_TM = 256


def _split_bf16(v):
    hi = v.astype(bf16)
    lo = (v - hi.astype(f32)).astype(bf16)
    return hi, lo


def _merge_kernel(oa_ref, ob_ref, ga_ref, gb_ref, x_ref, gt_ref, sc_ref, sh_ref, g_ref,
                  wa_ref, wb_ref, wo_ref, wr_ref, br_ref, x1_ref, hp_ref, lt_ref, y_even, y_odd):
    i = pl.program_id(0)

    @pl.when(i == 0)
    def _():
        y_odd[...] = jnp.zeros_like(y_odd)

    def step(y_cur, y_prev):
        x1 = x_ref[...] + gt_ref[...] * y_prev[...]
        x1_ref[...] = x1
        ms = jnp.mean(x1 * x1, axis=-1, keepdims=True)
        h2 = (x1 * lax.rsqrt(ms + RMS_EPS) * g_ref[...]) * (1.0 + sc_ref[...]) + sh_ref[...]
        hb = lax.bitcast_convert_type(h2.astype(bf16).astype(f32), jnp.uint32)
        half = D_MODEL // 2
        hp_ref[...] = (hb[:, :half] >> 16) | (hb[:, half:] & jnp.uint32(0xFFFF0000))
        h_hi, h_lo = _split_bf16(h2)
        w_hi, w_lo = _split_bf16(wr_ref[...])
        lt_ref[...] = _dot_nt(w_hi, h_hi) + _dot_nt(w_hi, h_lo) + _dot_nt(w_lo, h_hi) + br_ref[...]

        a = _dot(oa_ref[...], wa_ref[...])
        b = _dot(ob_ref[...], wb_ref[...])
        merged = (jax.nn.sigmoid(ga_ref[...].astype(f32)) * a + jax.nn.sigmoid(gb_ref[...].astype(f32)) * b)
        y_cur[...] = _dot(merged.astype(bf16), wo_ref[...])

    @pl.when(i % 2 == 0)
    def _():
        step(y_even, y_odd)

    @pl.when(i % 2 == 1)
    def _():
        step(y_odd, y_even)


def _merge(o_a, o_b, proj, x2, mod3, g_ffn, wa, wb, wo, wr_t, br_col):
    tm = MERGE_TM
    per_b = SEQ // tm
    nt = N_TOK // tm
    cur = lambda i: jnp.minimum(i, nt - 1)
    prev = lambda i: jnp.maximum(i - 1, 0)
    const = lambda shape: pl.BlockSpec(shape, lambda i: (0,) * len(shape), pipeline_mode=pl.Buffered(1))
    modspec = lambda chunk: pl.BlockSpec((None, 1, D_MODEL), lambda i: (prev(i) // per_b, 0, chunk))
    return pl.pallas_call(
        _merge_kernel,
        grid=(nt + 1,),
        in_specs=[pl.BlockSpec((tm, 1024), lambda i: (cur(i), 0)),
                  pl.BlockSpec((tm, 1024), lambda i: (cur(i), 0)),
                  pl.BlockSpec((tm, D_MODEL), lambda i: (cur(i), COL_GA // D_MODEL)),
                  pl.BlockSpec((tm, D_MODEL), lambda i: (cur(i), COL_GB // D_MODEL)),
                  pl.BlockSpec((tm, D_MODEL), lambda i: (prev(i), 0)),
                  modspec(2), modspec(4), modspec(3),
                  const((1, D_MODEL)),
                  const((1024, D_MODEL)), const((1024, D_MODEL)), const((D_MODEL, D_MODEL)),
                  const((ROUTER_ROWS, D_MODEL)), const((ROUTER_ROWS, 1))],
        out_specs=[pl.BlockSpec((tm, D_MODEL), lambda i: (prev(i), 0)),
                   pl.BlockSpec((tm, D_MODEL // 2), lambda i: (prev(i), 0)),
                   pl.BlockSpec((ROUTER_ROWS, tm), lambda i: (0, prev(i)))],
        out_shape=[jax.ShapeDtypeStruct((N_TOK, D_MODEL), f32),
                   jax.ShapeDtypeStruct((N_TOK, D_MODEL // 2), jnp.uint32),
                   jax.ShapeDtypeStruct((ROUTER_ROWS, N_TOK), f32)],
        scratch_shapes=[pltpu.VMEM((tm, D_MODEL), f32), pltpu.VMEM((tm, D_MODEL), f32)],
        compiler_params=_cparams(("arbitrary",)),
        name="merge",
    )(o_a, o_b, proj, proj, x2, mod3, mod3, mod3, g_ffn, wa, wb, wo, wr_t, br_col)


def _first_argmax(v, iota_f):
    vmax = jnp.max(v, axis=0, keepdims=True)
    idx = jnp.min(jnp.where(v == vmax, iota_f, float(v.shape[0])), axis=0, keepdims=True)
    return vmax, idx


def _route_kernel(lt_ref, ints_ref, w_ref, cnt_ref, carry):
    @pl.when(pl.program_id(0) == 0)
    def _():
        carry[...] = jnp.zeros_like(carry)

    cw = 256
    io8 = lax.broadcasted_iota(i32, (N_GROUPS, cw), 0).astype(f32)
    io64 = lax.broadcasted_iota(i32, (N_EXPERTS, cw), 0).astype(f32)
    tri = jnp.where(lax.broadcasted_iota(i32, (cw, cw), 0) < lax.broadcasted_iota(i32, (cw, cw), 1),
                    1.0, 0.0).astype(bf16)
    ints_ref[4:8, :] = jnp.zeros((4, ROUTE_T), i32)
    w_ref[2:8, :] = jnp.zeros((6, ROUTE_T), f32)
    for c in range(ROUTE_T // cw):
        sl = slice(c * cw, (c + 1) * cw)
        gl = lt_ref[0:N_GROUPS, sl]
        gmax, gsel = _first_argmax(gl, io8)
        gw = 1.0 / jnp.sum(jnp.exp(gl - gmax), axis=0, keepdims=True)
        es = jnp.zeros((EXPERTS_PER_GROUP, cw), f32)
        for g in range(N_GROUPS):
            lo = N_GROUPS + g * EXPERTS_PER_GROUP
            es = jnp.where(gsel == float(g), lt_ref[lo:lo + EXPERTS_PER_GROUP, sl], es)
        v1, i1 = _first_argmax(es, io8)
        es2 = jnp.where(io8 == i1, -jnp.inf, es)
        v2, i2 = _first_argmax(es2, io8)
        e = jnp.exp(v2 - v1)
        w_ref[0:1, sl] = gw / (1.0 + e)
        w_ref[1:2, sl] = gw * e / (1.0 + e)
        e1 = gsel * float(EXPERTS_PER_GROUP) + i1
        e2 = gsel * float(EXPERTS_PER_GROUP) + i2
        oh1 = io64 == e1
        oh2 = io64 == e2
        oc = jnp.where(oh1, 1.0, jnp.where(oh2, 1.0, 0.0))
        pre = _dot(oc.astype(bf16), tri) + carry[...]
        ints_ref[0:1, sl] = e1.astype(i32)
        ints_ref[1:2, sl] = e2.astype(i32)
        ints_ref[2:3, sl] = jnp.sum(jnp.where(oh1, pre, 0.0), axis=0, keepdims=True).astype(i32)
        ints_ref[3:4, sl] = jnp.sum(jnp.where(oh2, pre, 0.0), axis=0, keepdims=True).astype(i32)
        carry[...] = carry[...] + jnp.sum(oc, axis=1, keepdims=True)
    cnt_ref[...] = jnp.broadcast_to(carry[...], (N_EXPERTS, LANES)).astype(i32)


def _route(logits_t):
    t = ROUTE_T
    return pl.pallas_call(
        _route_kernel,
        grid=(N_TOK // t,),
        in_specs=[pl.BlockSpec((ROUTER_ROWS, t), lambda c: (0, c))],
        out_specs=[pl.BlockSpec((8, t), lambda c: (0, c)),
                   pl.BlockSpec((8, t), lambda c: (0, c)),
                   pl.BlockSpec((N_EXPERTS, LANES), lambda c: (0, 0))],
        out_shape=[jax.ShapeDtypeStruct((8, N_TOK), i32),
                   jax.ShapeDtypeStruct((8, N_TOK), f32),
                   jax.ShapeDtypeStruct((N_EXPERTS, LANES), i32)],
        scratch_shapes=[pltpu.VMEM((N_EXPERTS, 1), f32)],
        compiler_params=_cparams(("arbitrary",)),
        name="route",
    )(logits_t)


def _plan_kernel(cnt_ref, ints_ref, pos_ref, blk_u_ref, used_e_ref, meta_ref, pstart_s):
    @pl.when(pl.program_id(0) == 0)
    def _():
        def per_expert(e, carry):
            b, u = carry
            nb = (cnt_ref[e] + (MOE_TM - 1)) // MOE_TM
            pstart_s[e] = b * MOE_TM
            used_e_ref[u] = e

            def mark(k, c):
                blk_u_ref[b + k] = u
                return c

            lax.fori_loop(0, nb, mark, 0)
            return b + nb, u + jnp.minimum(nb, 1)

        nb_total, nu = lax.fori_loop(0, N_EXPERTS, per_expert, (0, 0))
        meta_ref[0] = nb_total
        meta_ref[1] = nu
        last_e = used_e_ref[nu - 1]

        def fill_e(k, c):
            used_e_ref[k] = last_e
            return c

        lax.fori_loop(nu, N_EXPERTS, fill_e, 0)

        def fill_b(k, c):
            blk_u_ref[k] = nu - 1
            return c

        lax.fori_loop(nb_total, MOE_NBLK, fill_b, 0)

    t = ROUTE_T
    iocol = lax.broadcasted_iota(i32, (N_EXPERTS, 1), 0)

    def build(e, col):
        return jnp.where(iocol == e, pstart_s[e], col)

    pcol = lax.fori_loop(0, N_EXPERTS, build, jnp.zeros((N_EXPERTS, 1), i32)).astype(f32)
    io64 = lax.broadcasted_iota(i32, (N_EXPERTS, t), 0)
    e1 = ints_ref[0:1, :]
    e2 = ints_ref[1:2, :]
    pos_ref[0:1, :] = (jnp.sum(jnp.where(io64 == e1, pcol, 0.0), axis=0, keepdims=True).astype(i32)
                       + ints_ref[2:3, :])
    pos_ref[1:2, :] = (jnp.sum(jnp.where(io64 == e2, pcol, 0.0), axis=0, keepdims=True).astype(i32)
                       + ints_ref[3:4, :])
    pos_ref[2:8, :] = jnp.zeros((6, t), i32)


def _plan(counts, ints):
    t = ROUTE_T
    return pl.pallas_call(
        _plan_kernel,
        grid_spec=pltpu.PrefetchScalarGridSpec(
            num_scalar_prefetch=1,
            grid=(N_TOK // t,),
            in_specs=[pl.BlockSpec((8, t), lambda c, cnt: (0, c))],
            out_specs=[pl.BlockSpec((8, t), lambda c, cnt: (0, c)),
                       pl.BlockSpec(memory_space=pltpu.SMEM),
                       pl.BlockSpec(memory_space=pltpu.SMEM),
                       pl.BlockSpec(memory_space=pltpu.SMEM)],
            scratch_shapes=[pltpu.SMEM((N_EXPERTS,), i32)]),
        out_shape=[jax.ShapeDtypeStruct((8, N_TOK), i32),
                   jax.ShapeDtypeStruct((MOE_NBLK,), i32),
                   jax.ShapeDtypeStruct((N_EXPERTS,), i32),
                   jax.ShapeDtypeStruct((2,), i32)],
        compiler_params=_cparams(("arbitrary",)),
        name="plan",
    )(counts, ints)


DISP_CHUNK = 512
DMA_UNROLL = 8


def _dispatch_kernel(pos_ref, hp_ref, xs_in_ref, xs_ref, sem):
    del xs_in_ref
    base = pl.program_id(0) * DISP_CHUNK

    def row_copy(r, k):
        dst = pos_ref[k * N_TOK + base + r]
        return pltpu.make_async_copy(hp_ref.at[pl.ds(r, 1), :], xs_ref.at[pl.ds(dst, 1), :], sem)

    def issue(g, carry):
        for u in range(DMA_UNROLL):
            r = g * DMA_UNROLL + u
            row_copy(r, 0).start()
            row_copy(r, 1).start()
        return carry

    lax.fori_loop(0, DISP_CHUNK // DMA_UNROLL, issue, 0)

    def drain(g, carry):
        for _ in range(DMA_UNROLL):
            row_copy(0, 0).wait()
            row_copy(0, 0).wait()
        return carry

    lax.fori_loop(0, DISP_CHUNK // DMA_UNROLL, drain, 0)


def _dispatch(pos_flat, hp, xs_zero):
    return pl.pallas_call(
        _dispatch_kernel,
        grid_spec=pltpu.PrefetchScalarGridSpec(
            num_scalar_prefetch=1,
            grid=(N_TOK // DISP_CHUNK,),
            in_specs=[pl.BlockSpec((DISP_CHUNK, D_MODEL // 2), lambda i, pos: (i, 0)),
                      pl.BlockSpec(memory_space=pl.ANY)],
            out_specs=pl.BlockSpec(memory_space=pl.ANY),
            scratch_shapes=[pltpu.SemaphoreType.DMA(())]),
        out_shape=jax.ShapeDtypeStruct((MOE_ROWS, D_MODEL // 2), jnp.uint32),
        input_output_aliases={2: 0},
        compiler_params=_cparams(("arbitrary",)),
        name="dispatch",
    )(pos_flat, hp, xs_zero)


MOE_KC = 512
N_GU_CH = D_MODEL // MOE_KC
N_DN_CH = D_EXPERT // MOE_KC
N_CH = N_GU_CH + N_DN_CH


def _moe_kernel(blk_u, used_e, meta, xs_ref, wgu_hbm, wdn_hbm, y_ref, ring, wgu_b, wdn_b, sems):
    i = pl.program_id(0)
    u = blk_u[i]
    first = (i == 0) | (u != blk_u[jnp.maximum(i - 1, 0)])
    active = i < meta[0]

    def chunk_copy(e, c):
        if c < N_GU_CH:
            src = wgu_hbm.at[e, pl.ds(c * MOE_KC, MOE_KC), :]
        else:
            src = wdn_hbm.at[e, pl.ds((c - N_GU_CH) * MOE_KC, MOE_KC), :]
        return pltpu.make_async_copy(src, ring.at[c], sems.at[c])

    @pl.when(i == 0)
    def _():
        for c in range(N_CH):
            chunk_copy(used_e[0], c).start()

    def unpack_x():
        packed = xs_ref[...]
        x_lo = lax.bitcast_convert_type(packed << 16, f32).astype(bf16)
        x_hi = lax.bitcast_convert_type(packed & jnp.uint32(0xFFFF0000), f32).astype(bf16)
        return x_lo, x_hi

    def swiglu(gu):
        g = gu[:, :D_EXPERT]
        return (g * jax.nn.sigmoid(g) * gu[:, D_EXPERT:]).astype(bf16)

    def rows(c):
        return slice(c * MOE_KC, (c + 1) * MOE_KC)

    @pl.when(active & first)
    def _():
        e_next = used_e[jnp.minimum(u + 1, meta[1] - 1)]

        def take(c):
            chunk_copy(e_next, c).wait()
            wb = ring[c].astype(bf16)
            if c < N_GU_CH:
                wgu_b[rows(c), :] = wb
            else:
                wdn_b[rows(c - N_GU_CH), :] = wb
            chunk_copy(e_next, c).start()

        x_lo, x_hi = unpack_x()
        per_half = N_GU_CH // 2
        x_chunks = [xh[:, rows(c)] for xh in (x_lo, x_hi) for c in range(per_half)]
        take(0)
        gu = None
        for c in range(N_GU_CH):
            take(c + 1)
            d = _dot(x_chunks[c], wgu_b[rows(c), :])
            gu = d if gu is None else gu + d
        a = swiglu(gu)
        y = None
        for c in range(N_DN_CH):
            if c + 1 < N_DN_CH:
                take(N_GU_CH + c + 1)
            d = _dot(a[:, rows(c)], wdn_b[rows(c), :])
            y = d if y is None else y + d
        y_ref[...] = y

    @pl.when(active & jnp.logical_not(first))
    def _():
        x_lo, x_hi = unpack_x()
        half = D_MODEL // 2
        gu = _dot(x_lo, wgu_b[0:half, :]) + _dot(x_hi, wgu_b[half:, :])
        y_ref[...] = _dot(swiglu(gu), wdn_b[...])

    @pl.when(i == pl.num_programs(0) - 1)
    def _():
        for c in range(N_CH):
            chunk_copy(used_e[0], c).wait()


def _moe(blk_u, used_e, meta, xs, w_gu, w_dn):
    tm = MOE_TM
    blk = lambda i, bu, ue, mt: (jnp.minimum(i, mt[0] - 1), 0)
    return pl.pallas_call(
        _moe_kernel,
        grid_spec=pltpu.PrefetchScalarGridSpec(
            num_scalar_prefetch=3,
            grid=(MOE_NBLK,),
            in_specs=[pl.BlockSpec((tm, D_MODEL // 2), blk),
                      pl.BlockSpec(memory_space=pl.ANY),
                      pl.BlockSpec(memory_space=pl.ANY)],
            out_specs=pl.BlockSpec((tm, D_MODEL), blk),
            scratch_shapes=[pltpu.VMEM((N_CH, MOE_KC, D_MODEL), f32),
                            pltpu.VMEM((D_MODEL, 2 * D_EXPERT), bf16),
                            pltpu.VMEM((D_EXPERT, D_MODEL), bf16),
                            pltpu.SemaphoreType.DMA((N_CH,))]),
        out_shape=jax.ShapeDtypeStruct((MOE_ROWS, D_MODEL), f32),
        compiler_params=_cparams(("arbitrary",)),
        name="moe",
    )(blk_u, used_e, meta, xs, w_gu, w_dn)


COMB_TM = 256


def _combine_kernel(pos_ref, x1_ref, gt_ref, wc_ref, g_ref, y_ref, o_ref, ybuf, sems):
    i = pl.program_id(0)
    slot = i % 2

    def row_copy(tile, r, k, s):
        src = pos_ref[k * N_TOK + tile * COMB_TM + r]
        return pltpu.make_async_copy(y_ref.at[pl.ds(src, 1), :], ybuf.at[s, k, pl.ds(r, 1), :], sems.at[s])

    def issue_tile(tile, s):
        def body(g, carry):
            for u in range(DMA_UNROLL):
                r = g * DMA_UNROLL + u
                row_copy(tile, r, 0, s).start()
                row_copy(tile, r, 1, s).start()
            return carry
        lax.fori_loop(0, COMB_TM // DMA_UNROLL, body, 0)

    @pl.when(i == 0)
    def _():
        issue_tile(0, 0)

    @pl.when(i + 1 < pl.num_programs(0))
    def _():
        issue_tile(i + 1, 1 - slot)

    def drain(g, carry):
        for _ in range(DMA_UNROLL):
            row_copy(0, 0, 0, slot).wait()
            row_copy(0, 0, 0, slot).wait()
        return carry

    lax.fori_loop(0, COMB_TM // DMA_UNROLL, drain, 0)

    ffn = wc_ref[:, 0:1] * ybuf[slot, 0] + wc_ref[:, 1:2] * ybuf[slot, 1]
    xo = x1_ref[...] + gt_ref[...] * ffn
    ms = jnp.mean(xo * xo, axis=-1, keepdims=True)
    o_ref[...] = xo * lax.rsqrt(ms + RMS_EPS) * g_ref[...]


def _combine(pos_flat, x1, mod3, w_col, g_final, y):
    tm = COMB_TM
    per_b = SEQ // tm
    return pl.pallas_call(
        _combine_kernel,
        grid_spec=pltpu.PrefetchScalarGridSpec(
            num_scalar_prefetch=1,
            grid=(N_TOK // tm,),
            in_specs=[pl.BlockSpec((tm, D_MODEL), lambda i, pos: (i, 0)),
                      pl.BlockSpec((None, 1, D_MODEL), lambda i, pos: (i // per_b, 0, 5)),
                      pl.BlockSpec((tm, 8), lambda i, pos: (i, 0)),
                      pl.BlockSpec((1, D_MODEL), lambda i, pos: (0, 0)),
                      pl.BlockSpec(memory_space=pl.ANY)],
            out_specs=pl.BlockSpec((tm, D_MODEL), lambda i, pos: (i, 0)),
            scratch_shapes=[pltpu.VMEM((2, TOP_K, tm, D_MODEL), f32),
                            pltpu.SemaphoreType.DMA((2,))]),
        out_shape=jax.ShapeDtypeStruct((N_TOK, D_MODEL), f32),
        compiler_params=_cparams(("arbitrary",)),
        name="combine",
    )(pos_flat, x1, mod3, w_col, g_final, y)


def kernel(x, c, positions, w_ada, b_ada, g_mix, w_in, swa_sinks, diff_lambda_q1, diff_lambda_k1, diff_lambda_q2, diff_lambda_k2, diff_subln_g, w_branch_a, w_branch_b, w_out, g_ffn, w_router_group, b_router_group, w_router_expert, b_router_expert, w_exp_gate_up, w_exp_down, g_final):
    layer = 0
    x2 = x.reshape(N_TOK, D_MODEL)

    mod = _ada(c, w_ada[layer], b_ada[layer])
    mod3 = mod.reshape(BATCH, 1, 6 * D_MODEL)

    w = w_in[layer]
    s_qa, s_ka, s_va, s_qd, s_kd, s_vd, s_ga = np.cumsum([1024, 128, 128, 1024, 1024, 1024, 2048])
    kv_heads = [w[:, s_qa + h * HEAD_DIM:s_qa + (h + 1) * HEAD_DIM] for h in range(2 * SWA_KV_HEADS)]
    w_in_p = jnp.concatenate(
        [w[:, :s_qa], w[:, s_va:s_qd], w[:, s_qd:s_kd], w[:, s_kd:s_vd], w[:, s_vd:s_ga], w[:, s_ga:]]
        + [h for h in kv_heads for _ in range(2)],
        axis=1).astype(bf16)
    half = HEAD_DIM // 2
    inv_freq = ROPE_THETA ** (-jnp.arange(0, HEAD_DIM, 2, dtype=f32) / HEAD_DIM)
    invf = jnp.tile(inv_freq, LANES // half).reshape(1, LANES)
    pos4 = jnp.repeat(positions.reshape(N_TOK // 4, 4).astype(i32), half, axis=1)
    cos4, sin4 = _rope_tables(pos4, invf)
    cos_t = jnp.tile(cos4.reshape(N_TOK, half), (1, LANES // half))
    sin_t = jnp.tile(sin4.reshape(N_TOK, half), (1, LANES // half))

    proj = _proj(x2, mod3, g_mix[layer].reshape(1, D_MODEL), cos_t, sin_t, w_in_p)
    o_a = _swa(swa_sinks[layer], proj)
    o_b = _diff(diff_lambda_q1[layer].reshape(1, HEAD_DIM), diff_lambda_k1[layer].reshape(1, HEAD_DIM),
                diff_lambda_q2[layer].reshape(1, HEAD_DIM), diff_lambda_k2[layer].reshape(1, HEAD_DIM),
                diff_subln_g[layer].reshape(LANES, 1), proj)

    wr = jnp.concatenate([w_router_group[layer], w_router_expert[layer]], axis=1)
    wr_t = jnp.pad(wr.T, ((0, ROUTER_ROWS - wr.shape[1]), (0, 0)))
    br = jnp.concatenate([b_router_group[layer], b_router_expert[layer]])
    br_col = jnp.pad(br, (0, ROUTER_ROWS - br.shape[0])).reshape(ROUTER_ROWS, 1)
    x1, hp, logits_t = _merge(o_a, o_b, proj, x2, mod3, g_ffn[layer].reshape(1, D_MODEL),
                              w_branch_a[layer].astype(bf16), w_branch_b[layer].astype(bf16),
                              w_out[layer].astype(bf16), wr_t, br_col)

    ints, wts, counts = _route(logits_t)
    pos, blk_u, used_e, meta = _plan(counts[:, 0], ints)
    pos_flat = pos[0:TOP_K].reshape(TOP_K * N_TOK)
    xs = _dispatch(pos_flat, hp, jnp.zeros((MOE_ROWS, D_MODEL // 2), jnp.uint32))
    y = _moe(blk_u, used_e, meta, xs, w_exp_gate_up[layer], w_exp_down[layer])
    out = _combine(pos_flat, x1, mod3, wts.T, g_final.reshape(1, D_MODEL), y)
    return out.reshape(BATCH, SEQ, D_MODEL)
```

```python
import functools
import math

import jax
import jax.numpy as jnp
import numpy as np
from jax import lax
from jax.experimental import pallas as pl
from jax.experimental.pallas import tpu as pltpu

D_MODEL = 2048
BATCH = 8
SEQ = 2048
N_TOK = BATCH * SEQ
HEAD_DIM = 64
ROPE_THETA = 10000.0
RMS_EPS = 1e-6
NEG_INF = -1e30
SWA_Q_HEADS = 16
SWA_KV_HEADS = 2
SWA_GROUP = SWA_Q_HEADS // SWA_KV_HEADS
WINDOW = 128
DIFF_HEADS = 8
N_GROUPS = 8
EXPERTS_PER_GROUP = 8
N_EXPERTS = N_GROUPS * EXPERTS_PER_GROUP
TOP_K = 2
D_EXPERT = D_MODEL // 2
LAMBDA_INIT = 0.8 - 0.6 * math.exp(-0.3 * 0)

LANES = 128
VMEM_LIMIT = 56 * 1024 * 1024

PROJ_TN = 512
PROJ_MAIN_TILES = 16
PROJ_COLS = (PROJ_MAIN_TILES + 1) * PROJ_TN
COL_QA, COL_QD, COL_KD, COL_VD, COL_GA, COL_GB, COL_KVA = 0, 1024, 2048, 3072, 4096, 6144, 8192

MOE_TM = 256
MOE_ROWS = N_TOK * TOP_K + N_EXPERTS * MOE_TM
MOE_NBLK = MOE_ROWS // MOE_TM
ROUTE_T = 2048
ROUTER_ROWS = 128

f32 = jnp.float32
bf16 = jnp.bfloat16
i32 = jnp.int32


def _cparams(sem, vmem=VMEM_LIMIT):
    return pltpu.CompilerParams(dimension_semantics=sem, vmem_limit_bytes=vmem)


def _dot(a, b):
    return jnp.dot(a, b, preferred_element_type=f32)


def _dot_nt(a, b):
    return lax.dot_general(a, b, (((1,), (1,)), ((), ())), preferred_element_type=f32)


def _ada_kernel(c_ref, w_ref, b_ref, o_ref):
    c = c_ref[...]
    ca = (c * jax.nn.sigmoid(c)).astype(bf16)
    o_ref[...] = _dot(ca, w_ref[...].astype(bf16)) + b_ref[...]


def _ada(c, w_ada, b_ada):
    tn = 512
    n = w_ada.shape[1]
    return pl.pallas_call(
        _ada_kernel,
        grid=(n // tn,),
        in_specs=[pl.BlockSpec((BATCH, D_MODEL), lambda j: (0, 0)),
                  pl.BlockSpec((D_MODEL, tn), lambda j: (0, j)),
                  pl.BlockSpec((1, tn), lambda j: (0, j))],
        out_specs=pl.BlockSpec((BATCH, tn), lambda j: (0, j)),
        out_shape=jax.ShapeDtypeStruct((BATCH, n), f32),
        compiler_params=_cparams(("arbitrary",)),
        name="ada",
    )(c, w_ada, b_ada.reshape(1, n))


PROJ_TM = 1024


def _rope128(xc, cos, sin_signed, first_half):
    r_prev = pltpu.roll(xc, 32, 1)
    r_next = pltpu.roll(xc, 96, 1)
    return xc * cos + jnp.where(first_half, r_next, r_prev) * sin_signed


def _rope_table_kernel(pos_ref, invf_ref, cos_ref, sin_ref):
    ang = pos_ref[...].astype(f32) * invf_ref[...]
    cos_ref[...] = jnp.cos(ang)
    sin_ref[...] = jnp.sin(ang)


def _rope_tables(pos4, invf):
    rows = pos4.shape[0]
    tr = 512
    spec = pl.BlockSpec((tr, LANES), lambda i: (i, 0))
    return pl.pallas_call(
        _rope_table_kernel,
        grid=(rows // tr,),
        in_specs=[spec, pl.BlockSpec((1, LANES), lambda i: (0, 0))],
        out_specs=[spec, spec],
        out_shape=[jax.ShapeDtypeStruct((rows, LANES), f32)] * 2,
        compiler_params=_cparams(("arbitrary",)),
        name="rope_tables",
    )(pos4, invf)


def _proj_kernel(x_ref, sc_ref, sh_ref, g_ref, cos_ref, sin_ref, w_ref, wkv_ref, o_ref, h_scr, sin_scr):
    j = pl.program_id(1)
    lane = lax.broadcasted_iota(i32, (1, LANES), 1)
    first_half = (lane % HEAD_DIM) < (HEAD_DIM // 2)

    @pl.when(j == 0)
    def _():
        rc = 256
        for r in range(PROJ_TM // rc):
            x = x_ref[r * rc:(r + 1) * rc, :]
            ms = jnp.mean(x * x, axis=-1, keepdims=True)
            y = x * lax.rsqrt(ms + RMS_EPS) * g_ref[...]
            h = y * (1.0 + sc_ref[...]) + sh_ref[...]
            h_scr[r * rc:(r + 1) * rc, :] = h.astype(bf16)
        sin_scr[...] = sin_ref[...] * jnp.where(first_half, -1.0, 1.0)

    hw = PROJ_TN // 2

    def rope_half(acc, off, scale):
        for c in range(hw // LANES):
            xc = acc[:, c * LANES:(c + 1) * LANES]
            o_ref[:, off + c * LANES:off + (c + 1) * LANES] = (
                _rope128(xc, cos_ref[...], sin_scr[...], first_half) * scale).astype(bf16)

    def cast_half(acc, off, scale):
        o_ref[:, off:off + hw] = acc.astype(bf16)

    def tile(w, epilogues, scale=1.0):
        for s, epilogue in enumerate(epilogues):
            epilogue(_dot(h_scr[...], w[:, s * hw:(s + 1) * hw]), s * hw, scale)

    @pl.when(j < 4)
    def _():
        tile(w_ref, (rope_half, rope_half), 1.0 / math.sqrt(HEAD_DIM))

    @pl.when((j >= 4) & (j < 6))
    def _():
        tile(w_ref, (rope_half, rope_half))

    @pl.when((j >= 6) & (j < PROJ_MAIN_TILES))
    def _():
        tile(w_ref, (cast_half, cast_half))

    @pl.when(j == PROJ_MAIN_TILES)
    def _():
        tile(wkv_ref, (rope_half, cast_half))


def _proj(x2, mod3, g_mix, cos_t, sin_t, w_main, w_kv):
    tm = PROJ_TM
    per_b = SEQ // tm
    return pl.pallas_call(
        _proj_kernel,
        grid=(N_TOK // tm, PROJ_COLS // PROJ_TN),
        in_specs=[pl.BlockSpec((tm, D_MODEL), lambda i, j: (i, 0)),
                  pl.BlockSpec((None, 1, D_MODEL), lambda i, j: (i // per_b, 0, 1)),
                  pl.BlockSpec((None, 1, D_MODEL), lambda i, j: (i // per_b, 0, 0)),
                  pl.BlockSpec((1, D_MODEL), lambda i, j: (0, 0)),
                  pl.BlockSpec((tm, LANES), lambda i, j: (i, 0)),
                  pl.BlockSpec((tm, LANES), lambda i, j: (i, 0)),
                  pl.BlockSpec((D_MODEL, PROJ_TN), lambda i, j: (0, jnp.minimum(j, PROJ_MAIN_TILES - 1))),
                  pl.BlockSpec((D_MODEL, PROJ_TN), lambda i, j: (0, 0))],
        out_specs=pl.BlockSpec((tm, PROJ_TN), lambda i, j: (i, j)),
        out_shape=jax.ShapeDtypeStruct((N_TOK, PROJ_COLS), bf16),
        scratch_shapes=[pltpu.VMEM((tm, D_MODEL), bf16),
                        pltpu.VMEM((tm, LANES), f32)],
        compiler_params=_cparams(("arbitrary", "arbitrary")),
        name="proj",
    )(x2, mod3, mod3, g_mix, cos_t, sin_t, w_main, w_kv)


def _swa_kernel(sink_ref, q_ref, kv_ref, kvp_ref, o_ref):
    n = pl.program_id(1)
    w = WINDOW
    qi = lax.broadcasted_iota(i32, (w, 2 * w), 0)
    kj = lax.broadcasted_iota(i32, (w, 2 * w), 1)
    rel = qi + w - kj
    first_key = jnp.where(n > 0, 0, w)
    valid = (rel >= 0) & (rel < w) & (kj >= first_key)
    lane = lax.broadcasted_iota(i32, (w, LANES), 1)
    low = lane < HEAD_DIM
    pairs = SWA_GROUP // 2
    for kvh in range(SWA_KV_HEADS):
        ks = slice(kvh * LANES, (kvh + 1) * LANES)
        vs = slice((SWA_KV_HEADS + kvh) * LANES, (SWA_KV_HEADS + kvh + 1) * LANES)
        kk = jnp.concatenate([kvp_ref[:, ks], kv_ref[:, ks]], axis=0)
        vv = jnp.concatenate([kvp_ref[:, vs], kv_ref[:, vs]], axis=0)
        parts = []
        for pair in range(pairs):
            gp = kvh * pairs + pair
            q2 = q_ref[:, gp * LANES:(gp + 1) * LANES]
            zero = jnp.zeros_like(q2)
            parts += [jnp.where(low, q2, zero), jnp.where(low, zero, q2)]
        s_all = _dot_nt(jnp.concatenate(parts, axis=0), kk)
        ps, dens = [], []
        for g in range(SWA_GROUP):
            s = jnp.where(valid, s_all[g * w:(g + 1) * w, :], NEG_INF)
            sink = sink_ref[kvh * SWA_GROUP + g]
            m = jnp.maximum(jnp.max(s, axis=-1, keepdims=True), sink)
            p = jnp.exp(s - m)
            dens.append(jnp.sum(p, axis=-1, keepdims=True) + jnp.exp(sink - m))
            ps.append(p.astype(bf16))
        o_all = _dot(jnp.concatenate(ps, axis=0), vv)
        for pair in range(pairs):
            gp = kvh * pairs + pair
            o_even = o_all[(2 * pair) * w:(2 * pair + 1) * w, :] / dens[2 * pair]
            o_odd = o_all[(2 * pair + 1) * w:(2 * pair + 2) * w, :] / dens[2 * pair + 1]
            o_ref[:, gp * LANES:(gp + 1) * LANES] = jnp.where(low, o_even, o_odd).astype(bf16)


def _swa(sinks, proj):
    nb = SEQ // WINDOW
    kv_col = COL_KVA // (4 * LANES)
    return pl.pallas_call(
        _swa_kernel,
        grid=(BATCH, nb),
        in_specs=[pl.BlockSpec(memory_space=pltpu.SMEM),
                  pl.BlockSpec((WINDOW, SWA_Q_HEADS * HEAD_DIM), lambda b, n: (b * nb + n, 0)),
                  pl.BlockSpec((WINDOW, 4 * LANES), lambda b, n: (b * nb + n, kv_col)),
                  pl.BlockSpec((WINDOW, 4 * LANES), lambda b, n: (jnp.maximum(b * nb + n - 1, 0), kv_col))],
        out_specs=pl.BlockSpec((WINDOW, SWA_Q_HEADS * HEAD_DIM), lambda b, n: (b * nb + n, 0)),
        out_shape=jax.ShapeDtypeStruct((N_TOK, SWA_Q_HEADS * HEAD_DIM), bf16),
        compiler_params=_cparams(("arbitrary", "arbitrary")),
        name="swa",
    )(sinks, proj, proj, proj)


DIFF_T = 256


def _diff_kernel(lq1_ref, lk1_ref, lq2_ref, lk2_ref, gcol_ref, q_ref, k_ref, v_ref, o_ref, vt_scr):
    t = DIFF_T
    nt = SEQ // t
    for j in range(nt):
        vt_scr[:, j * t:(j + 1) * t] = v_ref[j * t:(j + 1) * t, :].T
    lam = (jnp.exp(jnp.sum(lq1_ref[...] * lk1_ref[...], axis=-1, keepdims=True))
           - jnp.exp(jnp.sum(lq2_ref[...] * lk2_ref[...], axis=-1, keepdims=True)) + LAMBDA_INIT)
    lane = lax.broadcasted_iota(i32, (t, LANES), 1)
    causal = lax.broadcasted_iota(i32, (t, t), 0) <= lax.broadcasted_iota(i32, (t, t), 1)

    def one_map(qz, qi):
        lo = qi * t
        s_d = jnp.where(causal, _dot_nt(k_ref[lo:lo + t, :], qz), NEG_INF)
        m = jnp.max(s_d, axis=0, keepdims=True)
        if qi > 0:
            s_u = _dot_nt(k_ref[0:lo, :], qz)
            m = jnp.maximum(m, jnp.max(s_u, axis=0, keepdims=True))
        p_d = jnp.exp(s_d - m)
        l = jnp.sum(p_d, axis=0, keepdims=True)
        acc = _dot(vt_scr[:, lo:lo + t], p_d.astype(bf16))
        if qi > 0:
            p_u = jnp.exp(s_u - m)
            l = l + jnp.sum(p_u, axis=0, keepdims=True)
            acc = acc + _dot(vt_scr[:, 0:lo], p_u.astype(bf16))
        return acc / l

    for qi in range(nt):
        q = q_ref[qi * t:(qi + 1) * t, :]
        zero = jnp.zeros_like(q)
        o1 = one_map(jnp.where(lane < HEAD_DIM, q, zero), qi)
        o2 = one_map(jnp.where(lane >= HEAD_DIM, q, zero), qi)
        o = o1 - lam * o2
        y = o * lax.rsqrt(jnp.mean(o * o, axis=0, keepdims=True) + RMS_EPS) * gcol_ref[...]
        o_ref[qi * t:(qi + 1) * t, :] = (y * (1.0 - LAMBDA_INIT)).T.astype(bf16)


def _diff(lq1, lk1, lq2, lk2, subln_gcol, proj):
    vec = lambda n: pl.BlockSpec((1, n), lambda b, h: (0, 0))
    return pl.pallas_call(
        _diff_kernel,
        grid=(BATCH, DIFF_HEADS),
        in_specs=[vec(HEAD_DIM), vec(HEAD_DIM), vec(HEAD_DIM), vec(HEAD_DIM),
                  pl.BlockSpec((LANES, 1), lambda b, h: (0, 0)),
                  pl.BlockSpec((SEQ, LANES), lambda b, h: (b, COL_QD // LANES + h)),
                  pl.BlockSpec((SEQ, LANES), lambda b, h: (b, COL_KD // LANES + h)),
                  pl.BlockSpec((SEQ, LANES), lambda b, h: (b, COL_VD // LANES + h))],
        out_specs=pl.BlockSpec((SEQ, LANES), lambda b, h: (b, h)),
        out_shape=jax.ShapeDtypeStruct((N_TOK, DIFF_HEADS * LANES), bf16),
        scratch_shapes=[pltpu.VMEM((LANES, SEQ), bf16)],
        compiler_params=_cparams(("arbitrary", "arbitrary")),
        name="diff",
    )(lq1, lk1, lq2, lk2, subln_gcol, proj, proj, proj)


MERGE_TM = 256


def _split_bf16(v):
    hi = v.astype(bf16)
    lo = (v - hi.astype(f32)).astype(bf16)
    return hi, lo


def _merge_kernel(oa_ref, ob_ref, ga_ref, gb_ref, x_ref, gt_ref, sc_ref, sh_ref, g_ref,
                  wa_ref, wb_ref, wo_ref, wr_ref, br_ref, x1_ref, hp_ref, lt_ref, y_even, y_odd):
    i = pl.program_id(0)

    @pl.when(i == 0)
    def _():
        y_odd[...] = jnp.zeros_like(y_odd)

    def step(y_cur, y_prev):
        x1 = x_ref[...] + gt_ref[...] * y_prev[...]
        x1_ref[...] = x1
        ms = jnp.mean(x1 * x1, axis=-1, keepdims=True)
        h2 = (x1 * lax.rsqrt(ms + RMS_EPS) * g_ref[...]) * (1.0 + sc_ref[...]) + sh_ref[...]
        hb = lax.bitcast_convert_type(h2.astype(bf16).astype(f32), jnp.uint32)
        half = D_MODEL // 2
        hp_ref[...] = (hb[:, :half] >> 16) | (hb[:, half:] & jnp.uint32(0xFFFF0000))
        h_hi, h_lo = _split_bf16(h2)
        w_hi, w_lo = _split_bf16(wr_ref[...])
        lt_ref[...] = _dot_nt(w_hi, h_hi) + _dot_nt(w_hi, h_lo) + _dot_nt(w_lo, h_hi) + br_ref[...]

        a = _dot(oa_ref[...], wa_ref[...])
        b = _dot(ob_ref[...], wb_ref[...])
        merged = (jax.nn.sigmoid(ga_ref[...].astype(f32)) * a + jax.nn.sigmoid(gb_ref[...].astype(f32)) * b)
        y_cur[...] = _dot(merged.astype(bf16), wo_ref[...])

    @pl.when(i % 2 == 0)
    def _():
        step(y_even, y_odd)

    @pl.when(i % 2 == 1)
    def _():
        step(y_odd, y_even)


def _merge(o_a, o_b, proj, x2, mod3, g_ffn, wa, wb, wo, wr_t, br_col):
    tm = MERGE_TM
    per_b = SEQ // tm
    nt = N_TOK // tm
    cur = lambda i: jnp.minimum(i, nt - 1)
    prev = lambda i: jnp.maximum(i - 1, 0)
    const = lambda shape: pl.BlockSpec(shape, lambda i: (0,) * len(shape), pipeline_mode=pl.Buffered(1))
    modspec = lambda chunk: pl.BlockSpec((None, 1, D_MODEL), lambda i: (prev(i) // per_b, 0, chunk))
    return pl.pallas_call(
        _merge_kernel,
        grid=(nt + 1,),
        in_specs=[pl.BlockSpec((tm, 1024), lambda i: (cur(i), 0)),
                  pl.BlockSpec((tm, 1024), lambda i: (cur(i), 0)),
                  pl.BlockSpec((tm, D_MODEL), lambda i: (cur(i), COL_GA // D_MODEL)),
                  pl.BlockSpec((tm, D_MODEL), lambda i: (cur(i), COL_GB // D_MODEL)),
                  pl.BlockSpec((tm, D_MODEL), lambda i: (prev(i), 0)),
                  modspec(2), modspec(4), modspec(3),
                  const((1, D_MODEL)),
                  const((1024, D_MODEL)), const((1024, D_MODEL)), const((D_MODEL, D_MODEL)),
                  const((ROUTER_ROWS, D_MODEL)), const((ROUTER_ROWS, 1))],
        out_specs=[pl.BlockSpec((tm, D_MODEL), lambda i: (prev(i), 0)),
                   pl.BlockSpec((tm, D_MODEL // 2), lambda i: (prev(i), 0)),
                   pl.BlockSpec((ROUTER_ROWS, tm), lambda i: (0, prev(i)))],
        out_shape=[jax.ShapeDtypeStruct((N_TOK, D_MODEL), f32),
                   jax.ShapeDtypeStruct((N_TOK, D_MODEL // 2), jnp.uint32),
                   jax.ShapeDtypeStruct((ROUTER_ROWS, N_TOK), f32)],
        scratch_shapes=[pltpu.VMEM((tm, D_MODEL), f32), pltpu.VMEM((tm, D_MODEL), f32)],
        compiler_params=_cparams(("arbitrary",)),
        name="merge",
    )(o_a, o_b, proj, proj, x2, mod3, mod3, mod3, g_ffn, wa, wb, wo, wr_t, br_col)


def _first_argmax(v, iota_f):
    vmax = jnp.max(v, axis=0, keepdims=True)
    idx = jnp.min(jnp.where(v == vmax, iota_f, float(v.shape[0])), axis=0, keepdims=True)
    return vmax, idx


def _route_kernel(lt_ref, ints_ref, w_ref, cnt_ref, carry):
    @pl.when(pl.program_id(0) == 0)
    def _():
        carry[...] = jnp.zeros_like(carry)

    cw = 256
    io8 = lax.broadcasted_iota(i32, (N_GROUPS, cw), 0).astype(f32)
    io64 = lax.broadcasted_iota(i32, (N_EXPERTS, cw), 0).astype(f32)
    tri = jnp.where(lax.broadcasted_iota(i32, (cw, cw), 0) < lax.broadcasted_iota(i32, (cw, cw), 1),
                    1.0, 0.0).astype(bf16)
    ints_ref[4:8, :] = jnp.zeros((4, ROUTE_T), i32)
    w_ref[2:8, :] = jnp.zeros((6, ROUTE_T), f32)
    for c in range(ROUTE_T // cw):
        sl = slice(c * cw, (c + 1) * cw)
        gl = lt_ref[0:N_GROUPS, sl]
        gmax, gsel = _first_argmax(gl, io8)
        gw = 1.0 / jnp.sum(jnp.exp(gl - gmax), axis=0, keepdims=True)
        es = jnp.zeros((EXPERTS_PER_GROUP, cw), f32)
        for g in range(N_GROUPS):
            lo = N_GROUPS + g * EXPERTS_PER_GROUP
            es = jnp.where(gsel == float(g), lt_ref[lo:lo + EXPERTS_PER_GROUP, sl], es)
        v1, i1 = _first_argmax(es, io8)
        es2 = jnp.where(io8 == i1, -jnp.inf, es)
        v2, i2 = _first_argmax(es2, io8)
        e = jnp.exp(v2 - v1)
        w_ref[0:1, sl] = gw / (1.0 + e)
        w_ref[1:2, sl] = gw * e / (1.0 + e)
        e1 = gsel * float(EXPERTS_PER_GROUP) + i1
        e2 = gsel * float(EXPERTS_PER_GROUP) + i2
        oh1 = io64 == e1
        oh2 = io64 == e2
        oc = jnp.where(oh1, 1.0, jnp.where(oh2, 1.0, 0.0))
        pre = _dot(oc.astype(bf16), tri) + carry[...]
        ints_ref[0:1, sl] = e1.astype(i32)
        ints_ref[1:2, sl] = e2.astype(i32)
        ints_ref[2:3, sl] = jnp.sum(jnp.where(oh1, pre, 0.0), axis=0, keepdims=True).astype(i32)
        ints_ref[3:4, sl] = jnp.sum(jnp.where(oh2, pre, 0.0), axis=0, keepdims=True).astype(i32)
        carry[...] = carry[...] + jnp.sum(oc, axis=1, keepdims=True)
    cnt_ref[...] = jnp.broadcast_to(carry[...], (N_EXPERTS, LANES)).astype(i32)


def _route(logits_t):
    t = ROUTE_T
    return pl.pallas_call(
        _route_kernel,
        grid=(N_TOK // t,),
        in_specs=[pl.BlockSpec((ROUTER_ROWS, t), lambda c: (0, c))],
        out_specs=[pl.BlockSpec((8, t), lambda c: (0, c)),
                   pl.BlockSpec((8, t), lambda c: (0, c)),
                   pl.BlockSpec((N_EXPERTS, LANES), lambda c: (0, 0))],
        out_shape=[jax.ShapeDtypeStruct((8, N_TOK), i32),
                   jax.ShapeDtypeStruct((8, N_TOK), f32),
                   jax.ShapeDtypeStruct((N_EXPERTS, LANES), i32)],
        scratch_shapes=[pltpu.VMEM((N_EXPERTS, 1), f32)],
        compiler_params=_cparams(("arbitrary",)),
        name="route",
    )(logits_t)


def _plan_kernel(cnt_ref, ints_ref, pos_ref, blk_u_ref, used_e_ref, meta_ref, pstart_s):
    @pl.when(pl.program_id(0) == 0)
    def _():
        def per_expert(e, carry):
            b, u = carry
            nb = (cnt_ref[e] + (MOE_TM - 1)) // MOE_TM
            pstart_s[e] = b * MOE_TM
            used_e_ref[u] = e

            def mark(k, c):
                blk_u_ref[b + k] = u
                return c

            lax.fori_loop(0, nb, mark, 0)
            return b + nb, u + jnp.minimum(nb, 1)

        nb_total, nu = lax.fori_loop(0, N_EXPERTS, per_expert, (0, 0))
        meta_ref[0] = nb_total
        meta_ref[1] = nu
        last_e = used_e_ref[nu - 1]

        def fill_e(k, c):
            used_e_ref[k] = last_e
            return c

        lax.fori_loop(nu, N_EXPERTS, fill_e, 0)

        def fill_b(k, c):
            blk_u_ref[k] = nu - 1
            return c

        lax.fori_loop(nb_total, MOE_NBLK, fill_b, 0)

    t = ROUTE_T
    iocol = lax.broadcasted_iota(i32, (N_EXPERTS, 1), 0)

    def build(e, col):
        return jnp.where(iocol == e, pstart_s[e], col)

    pcol = lax.fori_loop(0, N_EXPERTS, build, jnp.zeros((N_EXPERTS, 1), i32)).astype(f32)
    io64 = lax.broadcasted_iota(i32, (N_EXPERTS, t), 0)
    e1 = ints_ref[0:1, :]
    e2 = ints_ref[1:2, :]
    pos_ref[0:1, :] = (jnp.sum(jnp.where(io64 == e1, pcol, 0.0), axis=0, keepdims=True).astype(i32)
                       + ints_ref[2:3, :])
    pos_ref[1:2, :] = (jnp.sum(jnp.where(io64 == e2, pcol, 0.0), axis=0, keepdims=True).astype(i32)
                       + ints_ref[3:4, :])
    pos_ref[2:8, :] = jnp.zeros((6, t), i32)


def _plan(counts, ints):
    t = ROUTE_T
    return pl.pallas_call(
        _plan_kernel,
        grid_spec=pltpu.PrefetchScalarGridSpec(
            num_scalar_prefetch=1,
            grid=(N_TOK // t,),
            in_specs=[pl.BlockSpec((8, t), lambda c, cnt: (0, c))],
            out_specs=[pl.BlockSpec((8, t), lambda c, cnt: (0, c)),
                       pl.BlockSpec(memory_space=pltpu.SMEM),
                       pl.BlockSpec(memory_space=pltpu.SMEM),
                       pl.BlockSpec(memory_space=pltpu.SMEM)],
            scratch_shapes=[pltpu.SMEM((N_EXPERTS,), i32)]),
        out_shape=[jax.ShapeDtypeStruct((8, N_TOK), i32),
                   jax.ShapeDtypeStruct((MOE_NBLK,), i32),
                   jax.ShapeDtypeStruct((N_EXPERTS,), i32),
                   jax.ShapeDtypeStruct((2,), i32)],
        compiler_params=_cparams(("arbitrary",)),
        name="plan",
    )(counts, ints)


DISP_CHUNK = 512
DMA_UNROLL = 8


def _dispatch_kernel(pos_ref, hp_ref, xs_in_ref, xs_ref, sem):
    del xs_in_ref
    base = pl.program_id(0) * DISP_CHUNK

    def row_copy(r, k):
        dst = pos_ref[k * N_TOK + base + r]
        return pltpu.make_async_copy(hp_ref.at[pl.ds(r, 1), :], xs_ref.at[pl.ds(dst, 1), :], sem)

    def issue(g, carry):
        for u in range(DMA_UNROLL):
            r = g * DMA_UNROLL + u
            row_copy(r, 0).start()
            row_copy(r, 1).start()
        return carry

    lax.fori_loop(0, DISP_CHUNK // DMA_UNROLL, issue, 0)

    def drain(g, carry):
        for _ in range(DMA_UNROLL):
            row_copy(0, 0).wait()
            row_copy(0, 0).wait()
        return carry

    lax.fori_loop(0, DISP_CHUNK // DMA_UNROLL, drain, 0)


def _dispatch(pos_flat, hp, xs_zero):
    return pl.pallas_call(
        _dispatch_kernel,
        grid_spec=pltpu.PrefetchScalarGridSpec(
            num_scalar_prefetch=1,
            grid=(N_TOK // DISP_CHUNK,),
            in_specs=[pl.BlockSpec((DISP_CHUNK, D_MODEL // 2), lambda i, pos: (i, 0)),
                      pl.BlockSpec(memory_space=pl.ANY)],
            out_specs=pl.BlockSpec(memory_space=pl.ANY),
            scratch_shapes=[pltpu.SemaphoreType.DMA(())]),
        out_shape=jax.ShapeDtypeStruct((MOE_ROWS, D_MODEL // 2), jnp.uint32),
        input_output_aliases={2: 0},
        compiler_params=_cparams(("arbitrary",)),
        name="dispatch",
    )(pos_flat, hp, xs_zero)


MOE_KC = 512
N_GU_CH = D_MODEL // MOE_KC
N_DN_CH = D_EXPERT // MOE_KC
N_CH = N_GU_CH + N_DN_CH


def _moe_kernel(blk_u, used_e, meta, xs_ref, wgu_hbm, wdn_hbm, y_ref, ring, wgu_b, wdn_b, sems):
    i = pl.program_id(0)
    u = blk_u[i]
    first = (i == 0) | (u != blk_u[jnp.maximum(i - 1, 0)])
    active = i < meta[0]

    def chunk_copy(e, c):
        if c < N_GU_CH:
            src = wgu_hbm.at[e, pl.ds(c * MOE_KC, MOE_KC), :]
        else:
            src = wdn_hbm.at[e, pl.ds((c - N_GU_CH) * MOE_KC, MOE_KC), :]
        return pltpu.make_async_copy(src, ring.at[c], sems.at[c])

    @pl.when(i == 0)
    def _():
        for c in range(N_CH):
            chunk_copy(used_e[0], c).start()

    def unpack_x():
        packed = xs_ref[...]
        x_lo = lax.bitcast_convert_type(packed << 16, f32).astype(bf16)
        x_hi = lax.bitcast_convert_type(packed & jnp.uint32(0xFFFF0000), f32).astype(bf16)
        return x_lo, x_hi

    def swiglu(gu):
        g = gu[:, :D_EXPERT]
        return (g * jax.nn.sigmoid(g) * gu[:, D_EXPERT:]).astype(bf16)

    def rows(c):
        return slice(c * MOE_KC, (c + 1) * MOE_KC)

    @pl.when(active & first)
    def _():
        e_next = used_e[jnp.minimum(u + 1, meta[1] - 1)]

        def take(c):
            chunk_copy(e_next, c).wait()
            wb = ring[c].astype(bf16)
            if c < N_GU_CH:
                wgu_b[rows(c), :] = wb
            else:
                wdn_b[rows(c - N_GU_CH), :] = wb
            chunk_copy(e_next, c).start()

        x_lo, x_hi = unpack_x()
        per_half = N_GU_CH // 2
        x_chunks = [xh[:, rows(c)] for xh in (x_lo, x_hi) for c in range(per_half)]
        take(0)
        gu = None
        for c in range(N_GU_CH):
            take(c + 1)
            d = _dot(x_chunks[c], wgu_b[rows(c), :])
            gu = d if gu is None else gu + d
        a = swiglu(gu)
        y = None
        for c in range(N_DN_CH):
            if c + 1 < N_DN_CH:
                take(N_GU_CH + c + 1)
            d = _dot(a[:, rows(c)], wdn_b[rows(c), :])
            y = d if y is None else y + d
        y_ref[...] = y

    @pl.when(active & jnp.logical_not(first))
    def _():
        x_lo, x_hi = unpack_x()
        half = D_MODEL // 2
        gu = _dot(x_lo, wgu_b[0:half, :]) + _dot(x_hi, wgu_b[half:, :])
        y_ref[...] = _dot(swiglu(gu), wdn_b[...])

    @pl.when(i == pl.num_programs(0) - 1)
    def _():
        for c in range(N_CH):
            chunk_copy(used_e[0], c).wait()


def _moe(blk_u, used_e, meta, xs, w_gu, w_dn):
    tm = MOE_TM
    blk = lambda i, bu, ue, mt: (jnp.minimum(i, mt[0] - 1), 0)
    return pl.pallas_call(
        _moe_kernel,
        grid_spec=pltpu.PrefetchScalarGridSpec(
            num_scalar_prefetch=3,
            grid=(MOE_NBLK,),
            in_specs=[pl.BlockSpec((tm, D_MODEL // 2), blk),
                      pl.BlockSpec(memory_space=pl.ANY),
                      pl.BlockSpec(memory_space=pl.ANY)],
            out_specs=pl.BlockSpec((tm, D_MODEL), blk),
            scratch_shapes=[pltpu.VMEM((N_CH, MOE_KC, D_MODEL), f32),
                            pltpu.VMEM((D_MODEL, 2 * D_EXPERT), bf16),
                            pltpu.VMEM((D_EXPERT, D_MODEL), bf16),
                            pltpu.SemaphoreType.DMA((N_CH,))]),
        out_shape=jax.ShapeDtypeStruct((MOE_ROWS, D_MODEL), f32),
        compiler_params=_cparams(("arbitrary",)),
        name="moe",
    )(blk_u, used_e, meta, xs, w_gu, w_dn)


COMB_TM = 256


def _combine_kernel(pos_ref, x1_ref, gt_ref, wc_ref, g_ref, y_ref, o_ref, ybuf_even, ybuf_odd, sems):
    i = pl.program_id(0)
    last = pl.num_programs(0) - 1
    nxt = jnp.minimum(i + 1, last)
    bufs = (ybuf_even, ybuf_odd)

    def row_copy(tile, r, k, s):
        src = pos_ref[k * N_TOK + tile * COMB_TM + r]
        return pltpu.make_async_copy(y_ref.at[pl.ds(src, 1), :], bufs[s].at[k, pl.ds(r, 1), :], sems.at[s])

    def drain(s):
        def body(g, carry):
            for _ in range(DMA_UNROLL):
                row_copy(0, 0, 0, s).wait()
                row_copy(0, 0, 0, s).wait()
            return carry
        lax.fori_loop(0, COMB_TM // DMA_UNROLL, body, 0)

    @pl.when(i == 0)
    def _():
        def body(g, carry):
            for u in range(DMA_UNROLL):
                r = g * DMA_UNROLL + u
                row_copy(0, r, 0, 0).start()
                row_copy(0, r, 1, 0).start()
            return carry
        lax.fori_loop(0, COMB_TM // DMA_UNROLL, body, 0)

    def step(s):
        drain(s)
        for r in range(COMB_TM):
            row_copy(nxt, r, 0, 1 - s).start()
            row_copy(nxt, r, 1, 1 - s).start()
        ffn = wc_ref[:, 0:1] * bufs[s][0] + wc_ref[:, 1:2] * bufs[s][1]
        xo = x1_ref[...] + gt_ref[...] * ffn
        ms = jnp.mean(xo * xo, axis=-1, keepdims=True)
        o_ref[...] = xo * lax.rsqrt(ms + RMS_EPS) * g_ref[...]

        @pl.when(i == last)
        def _():
            drain(1 - s)

    @pl.when(i % 2 == 0)
    def _():
        step(0)

    @pl.when(i % 2 == 1)
    def _():
        step(1)


def _combine(pos_flat, x1, mod3, w_col, g_final, y):
    tm = COMB_TM
    per_b = SEQ // tm
    return pl.pallas_call(
        _combine_kernel,
        grid_spec=pltpu.PrefetchScalarGridSpec(
            num_scalar_prefetch=1,
            grid=(N_TOK // tm,),
            in_specs=[pl.BlockSpec((tm, D_MODEL), lambda i, pos: (i, 0)),
                      pl.BlockSpec((None, 1, D_MODEL), lambda i, pos: (i // per_b, 0, 5)),
                      pl.BlockSpec((tm, 8), lambda i, pos: (i, 0)),
                      pl.BlockSpec((1, D_MODEL), lambda i, pos: (0, 0)),
                      pl.BlockSpec(memory_space=pl.ANY)],
            out_specs=pl.BlockSpec((tm, D_MODEL), lambda i, pos: (i, 0)),
            scratch_shapes=[pltpu.VMEM((TOP_K, tm, D_MODEL), f32),
                            pltpu.VMEM((TOP_K, tm, D_MODEL), f32),
                            pltpu.SemaphoreType.DMA((2,))]),
        out_shape=jax.ShapeDtypeStruct((N_TOK, D_MODEL), f32),
        compiler_params=_cparams(("arbitrary",)),
        name="combine",
    )(pos_flat, x1, mod3, w_col, g_final, y)


def kernel(x, c, positions, w_ada, b_ada, g_mix, w_in, swa_sinks, diff_lambda_q1, diff_lambda_k1, diff_lambda_q2, diff_lambda_k2, diff_subln_g, w_branch_a, w_branch_b, w_out, g_ffn, w_router_group, b_router_group, w_router_expert, b_router_expert, w_exp_gate_up, w_exp_down, g_final):
    layer = 0
    x2 = x.reshape(N_TOK, D_MODEL)

    mod = _ada(c, w_ada[layer], b_ada[layer])
    mod3 = mod.reshape(BATCH, 1, 6 * D_MODEL)

    w = w_in[layer]
    s_qa, s_ka, s_va, s_qd, s_kd, s_vd, s_ga = np.cumsum([1024, 128, 128, 1024, 1024, 1024, 2048])
    w_main = jnp.concatenate([w[:, :s_qa], w[:, s_va:]], axis=1).astype(bf16)
    kv_heads = [w[:, s_qa + h * HEAD_DIM:s_qa + (h + 1) * HEAD_DIM] for h in range(2 * SWA_KV_HEADS)]
    w_kv = jnp.concatenate([h for h in kv_heads for _ in range(2)], axis=1).astype(bf16)
    half = HEAD_DIM // 2
    inv_freq = ROPE_THETA ** (-jnp.arange(0, HEAD_DIM, 2, dtype=f32) / HEAD_DIM)
    invf = jnp.tile(inv_freq, LANES // half).reshape(1, LANES)
    pos4 = jnp.repeat(positions.reshape(N_TOK // 4, 4).astype(i32), half, axis=1)
    cos4, sin4 = _rope_tables(pos4, invf)
    cos_t = jnp.tile(cos4.reshape(N_TOK, half), (1, LANES // half))
    sin_t = jnp.tile(sin4.reshape(N_TOK, half), (1, LANES // half))

    proj = _proj(x2, mod3, g_mix[layer].reshape(1, D_MODEL), cos_t, sin_t, w_main, w_kv)
    o_a = _swa(swa_sinks[layer], proj)
    o_b = _diff(diff_lambda_q1[layer].reshape(1, HEAD_DIM), diff_lambda_k1[layer].reshape(1, HEAD_DIM),
                diff_lambda_q2[layer].reshape(1, HEAD_DIM), diff_lambda_k2[layer].reshape(1, HEAD_DIM),
                diff_subln_g[layer].reshape(LANES, 1), proj)

    wr = jnp.concatenate([w_router_group[layer], w_router_expert[layer]], axis=1)
    wr_t = jnp.pad(wr.T, ((0, ROUTER_ROWS - wr.shape[1]), (0, 0)))
    br = jnp.concatenate([b_router_group[layer], b_router_expert[layer]])
    br_col = jnp.pad(br, (0, ROUTER_ROWS - br.shape[0])).reshape(ROUTER_ROWS, 1)
    x1, hp, logits_t = _merge(o_a, o_b, proj, x2, mod3, g_ffn[layer].reshape(1, D_MODEL),
                              w_branch_a[layer].astype(bf16), w_branch_b[layer].astype(bf16),
                              w_out[layer].astype(bf16), wr_t, br_col)

    ints, wts, counts = _route(logits_t)
    pos, blk_u, used_e, meta = _plan(counts[:, 0], ints)
    pos_flat = pos[0:TOP_K].reshape(TOP_K * N_TOK)
    xs = _dispatch(pos_flat, hp, jnp.zeros((MOE_ROWS, D_MODEL // 2), jnp.uint32))
    y = _moe(blk_u, used_e, meta, xs, w_exp_gate_up[layer], w_exp_down[layer])
    out = _combine(pos_flat, x1, mod3, wts.T, g_final.reshape(1, D_MODEL), y)
    return out.reshape(BATCH, SEQ, D_MODEL)
```

```python
import functools
import math

import jax
import jax.numpy as jnp
import numpy as np
from jax import lax
from jax.experimental import pallas as pl
from jax.experimental.pallas import tpu as pltpu

D_MODEL = 2048
BATCH = 8
SEQ = 2048
N_TOK = BATCH * SEQ
HEAD_DIM = 64
ROPE_THETA = 10000.0
RMS_EPS = 1e-6
NEG_INF = -1e30
SWA_Q_HEADS = 16
SWA_KV_HEADS = 2
SWA_GROUP = SWA_Q_HEADS // SWA_KV_HEADS
WINDOW = 128
DIFF_HEADS = 8
N_GROUPS = 8
EXPERTS_PER_GROUP = 8
N_EXPERTS = N_GROUPS * EXPERTS_PER_GROUP
TOP_K = 2
D_EXPERT = D_MODEL // 2
LAMBDA_INIT = 0.8 - 0.6 * math.exp(-0.3 * 0)

LANES = 128
VMEM_LIMIT = 56 * 1024 * 1024

PROJ_TN = 512
PROJ_MAIN_TILES = 16
PROJ_COLS = (PROJ_MAIN_TILES + 1) * PROJ_TN
COL_QA, COL_QD, COL_KD, COL_VD, COL_GA, COL_GB, COL_KVA = 0, 1024, 2048, 3072, 4096, 6144, 8192

MOE_TM = 256
MOE_ROWS = N_TOK * TOP_K + N_EXPERTS * MOE_TM
MOE_NBLK = MOE_ROWS // MOE_TM
ROUTE_T = 2048
ROUTER_ROWS = 128

f32 = jnp.float32
bf16 = jnp.bfloat16
i32 = jnp.int32


def _cparams(sem, vmem=VMEM_LIMIT):
    return pltpu.CompilerParams(dimension_semantics=sem, vmem_limit_bytes=vmem)


def _dot(a, b):
    return jnp.dot(a, b, preferred_element_type=f32)


def _dot_nt(a, b):
    return lax.dot_general(a, b, (((1,), (1,)), ((), ())), preferred_element_type=f32)


def _ada_kernel(c_ref, w_ref, b_ref, o_ref):
    c = c_ref[...]
    ca = (c * jax.nn.sigmoid(c)).astype(bf16)
    o_ref[...] = _dot(ca, w_ref[...].astype(bf16)) + b_ref[...]


def _ada(c, w_ada, b_ada):
    tn = 512
    n = w_ada.shape[1]
    return pl.pallas_call(
        _ada_kernel,
        grid=(n // tn,),
        in_specs=[pl.BlockSpec((BATCH, D_MODEL), lambda j: (0, 0)),
                  pl.BlockSpec((D_MODEL, tn), lambda j: (0, j)),
                  pl.BlockSpec((1, tn), lambda j: (0, j))],
        out_specs=pl.BlockSpec((BATCH, tn), lambda j: (0, j)),
        out_shape=jax.ShapeDtypeStruct((BATCH, n), f32),
        compiler_params=_cparams(("arbitrary",)),
        name="ada",
    )(c, w_ada, b_ada.reshape(1, n))


PROJ_TM = 1024


def _rope128(xc, cos, sin_signed, first_half):
    r_prev = pltpu.roll(xc, 32, 1)
    r_next = pltpu.roll(xc, 96, 1)
    return xc * cos + jnp.where(first_half, r_next, r_prev) * sin_signed


def _rope_table_kernel(pos_ref, invf_ref, cos_ref, sin_ref):
    ang = pos_ref[...].astype(f32) * invf_ref[...]
    cos_ref[...] = jnp.cos(ang)
    sin_ref[...] = jnp.sin(ang)


def _rope_tables(pos4, invf):
    rows = pos4.shape[0]
    tr = 512
    spec = pl.BlockSpec((tr, LANES), lambda i: (i, 0))
    return pl.pallas_call(
        _rope_table_kernel,
        grid=(rows // tr,),
        in_specs=[spec, pl.BlockSpec((1, LANES), lambda i: (0, 0))],
        out_specs=[spec, spec],
        out_shape=[jax.ShapeDtypeStruct((rows, LANES), f32)] * 2,
        compiler_params=_cparams(("arbitrary",)),
        name="rope_tables",
    )(pos4, invf)


def _proj_kernel(x_ref, sc_ref, sh_ref, g_ref, cos_ref, sin_ref, w_ref, wkv_ref, o_ref, h_scr, sin_scr):
    j = pl.program_id(1)
    lane = lax.broadcasted_iota(i32, (1, LANES), 1)
    first_half = (lane % HEAD_DIM) < (HEAD_DIM // 2)

    @pl.when(j == 0)
    def _():
        rc = 256
        for r in range(PROJ_TM // rc):
            x = x_ref[r * rc:(r + 1) * rc, :]
            ms = jnp.mean(x * x, axis=-1, keepdims=True)
            y = x * lax.rsqrt(ms + RMS_EPS) * g_ref[...]
            h = y * (1.0 + sc_ref[...]) + sh_ref[...]
            h_scr[r * rc:(r + 1) * rc, :] = h.astype(bf16)
        sin_scr[...] = sin_ref[...] * jnp.where(first_half, -1.0, 1.0)

    hw = PROJ_TN // 2

    def rope_half(acc, off, scale):
        for c in range(hw // LANES):
            xc = acc[:, c * LANES:(c + 1) * LANES]
            o_ref[:, off + c * LANES:off + (c + 1) * LANES] = (
                _rope128(xc, cos_ref[...], sin_scr[...], first_half) * scale).astype(bf16)

    def cast_half(acc, off, scale):
        o_ref[:, off:off + hw] = acc.astype(bf16)

    def tile(w, epilogues, scale=1.0):
        for s, epilogue in enumerate(epilogues):
            epilogue(_dot(h_scr[...], w[:, s * hw:(s + 1) * hw]), s * hw, scale)

    @pl.when(j < 4)
    def _():
        tile(w_ref, (rope_half, rope_half), 1.0 / math.sqrt(HEAD_DIM))

    @pl.when((j >= 4) & (j < 6))
    def _():
        tile(w_ref, (rope_half, rope_half))

    @pl.when((j >= 6) & (j < PROJ_MAIN_TILES))
    def _():
        tile(w_ref, (cast_half, cast_half))

    @pl.when(j == PROJ_MAIN_TILES)
    def _():
        tile(wkv_ref, (rope_half, cast_half))


def _proj(x2, mod3, g_mix, cos_t, sin_t, w_main, w_kv):
    tm = PROJ_TM
    per_b = SEQ // tm
    return pl.pallas_call(
        _proj_kernel,
        grid=(N_TOK // tm, PROJ_COLS // PROJ_TN),
        in_specs=[pl.BlockSpec((tm, D_MODEL), lambda i, j: (i, 0)),
                  pl.BlockSpec((None, 1, D_MODEL), lambda i, j: (i // per_b, 0, 1)),
                  pl.BlockSpec((None, 1, D_MODEL), lambda i, j: (i // per_b, 0, 0)),
                  pl.BlockSpec((1, D_MODEL), lambda i, j: (0, 0)),
                  pl.BlockSpec((tm, LANES), lambda i, j: (i, 0)),
                  pl.BlockSpec((tm, LANES), lambda i, j: (i, 0)),
                  pl.BlockSpec((D_MODEL, PROJ_TN), lambda i, j: (0, jnp.minimum(j, PROJ_MAIN_TILES - 1))),
                  pl.BlockSpec((D_MODEL, PROJ_TN), lambda i, j: (0, 0))],
        out_specs=pl.BlockSpec((tm, PROJ_TN), lambda i, j: (i, j)),
        out_shape=jax.ShapeDtypeStruct((N_TOK, PROJ_COLS), bf16),
        scratch_shapes=[pltpu.VMEM((tm, D_MODEL), bf16),
                        pltpu.VMEM((tm, LANES), f32)],
        compiler_params=_cparams(("arbitrary", "arbitrary")),
        name="proj",
    )(x2, mod3, mod3, g_mix, cos_t, sin_t, w_main, w_kv)


def _swa_kernel(sink_ref, q_ref, kv_ref, kvp_ref, o_ref):
    n = pl.program_id(1)
    w = WINDOW
    qi = lax.broadcasted_iota(i32, (w, 2 * w), 0)
    kj = lax.broadcasted_iota(i32, (w, 2 * w), 1)
    rel = qi + w - kj
    first_key = jnp.where(n > 0, 0, w)
    valid = (rel >= 0) & (rel < w) & (kj >= first_key)
    lane = lax.broadcasted_iota(i32, (w, LANES), 1)
    low = lane < HEAD_DIM
    pairs = SWA_GROUP // 2
    for kvh in range(SWA_KV_HEADS):
        ks = slice(kvh * LANES, (kvh + 1) * LANES)
        vs = slice((SWA_KV_HEADS + kvh) * LANES, (SWA_KV_HEADS + kvh + 1) * LANES)
        kk = jnp.concatenate([kvp_ref[:, ks], kv_ref[:, ks]], axis=0)
        vv = jnp.concatenate([kvp_ref[:, vs], kv_ref[:, vs]], axis=0)
        parts = []
        for pair in range(pairs):
            gp = kvh * pairs + pair
            q2 = q_ref[:, gp * LANES:(gp + 1) * LANES]
            zero = jnp.zeros_like(q2)
            parts += [jnp.where(low, q2, zero), jnp.where(low, zero, q2)]
        s_all = _dot_nt(jnp.concatenate(parts, axis=0), kk)
        ps, dens = [], []
        for g in range(SWA_GROUP):
            s = jnp.where(valid, s_all[g * w:(g + 1) * w, :], NEG_INF)
            sink = sink_ref[kvh * SWA_GROUP + g]
            m = jnp.maximum(jnp.max(s, axis=-1, keepdims=True), sink)
            p = jnp.exp(s - m)
            dens.append(jnp.sum(p, axis=-1, keepdims=True) + jnp.exp(sink - m))
            ps.append(p.astype(bf16))
        o_all = _dot(jnp.concatenate(ps, axis=0), vv)
        for pair in range(pairs):
            gp = kvh * pairs + pair
            o_even = o_all[(2 * pair) * w:(2 * pair + 1) * w, :] / dens[2 * pair]
            o_odd = o_all[(2 * pair + 1) * w:(2 * pair + 2) * w, :] / dens[2 * pair + 1]
            o_ref[:, gp * LANES:(gp + 1) * LANES] = jnp.where(low, o_even, o_odd).astype(bf16)


def _swa(sinks, proj):
    nb = SEQ // WINDOW
    kv_col = COL_KVA // (4 * LANES)
    return pl.pallas_call(
        _swa_kernel,
        grid=(BATCH, nb),
        in_specs=[pl.BlockSpec(memory_space=pltpu.SMEM),
                  pl.BlockSpec((WINDOW, SWA_Q_HEADS * HEAD_DIM), lambda b, n: (b * nb + n, 0)),
                  pl.BlockSpec((WINDOW, 4 * LANES), lambda b, n: (b * nb + n, kv_col)),
                  pl.BlockSpec((WINDOW, 4 * LANES), lambda b, n: (jnp.maximum(b * nb + n - 1, 0), kv_col))],
        out_specs=pl.BlockSpec((WINDOW, SWA_Q_HEADS * HEAD_DIM), lambda b, n: (b * nb + n, 0)),
        out_shape=jax.ShapeDtypeStruct((N_TOK, SWA_Q_HEADS * HEAD_DIM), bf16),
        compiler_params=_cparams(("arbitrary", "arbitrary")),
        name="swa",
    )(sinks, proj, proj, proj)


DIFF_T = 256


def _diff_kernel(lq1_ref, lk1_ref, lq2_ref, lk2_ref, gcol_ref, q_ref, k_ref, v_ref, o_ref, vt_scr):
    t = DIFF_T
    nt = SEQ // t
    for j in range(nt):
        vt_scr[:, j * t:(j + 1) * t] = v_ref[j * t:(j + 1) * t, :].T
    lam = (jnp.exp(jnp.sum(lq1_ref[...] * lk1_ref[...], axis=-1, keepdims=True))
           - jnp.exp(jnp.sum(lq2_ref[...] * lk2_ref[...], axis=-1, keepdims=True)) + LAMBDA_INIT)
    lane = lax.broadcasted_iota(i32, (t, LANES), 1)
    causal = lax.broadcasted_iota(i32, (t, t), 0) <= lax.broadcasted_iota(i32, (t, t), 1)

    def one_map(qz, qi):
        lo = qi * t
        s_d = jnp.where(causal, _dot_nt(k_ref[lo:lo + t, :], qz), NEG_INF)
        m = jnp.max(s_d, axis=0, keepdims=True)
        if qi > 0:
            s_u = _dot_nt(k_ref[0:lo, :], qz)
            m = jnp.maximum(m, jnp.max(s_u, axis=0, keepdims=True))
        p_d = jnp.exp(s_d - m)
        l = jnp.sum(p_d, axis=0, keepdims=True)
        p_u = None
        if qi > 0:
            p_u = jnp.exp(s_u - m)
            l = l + jnp.sum(p_u, axis=0, keepdims=True)
        return p_d, p_u, 1.0 / l

    for qi in range(nt):
        lo = qi * t
        q = q_ref[lo:lo + t, :]
        zero = jnp.zeros_like(q)
        pd1, pu1, r1 = one_map(jnp.where(lane < HEAD_DIM, q, zero), qi)
        pd2, pu2, r2 = one_map(jnp.where(lane >= HEAD_DIM, q, zero), qi)
        r2 = lam * r2
        o = _dot(vt_scr[:, lo:lo + t], (pd1 * r1 - pd2 * r2).astype(bf16))
        if qi > 0:
            o = o + _dot(vt_scr[:, 0:lo], (pu1 * r1 - pu2 * r2).astype(bf16))
        y = o * lax.rsqrt(jnp.mean(o * o, axis=0, keepdims=True) + RMS_EPS) * gcol_ref[...]
        o_ref[qi * t:(qi + 1) * t, :] = (y * (1.0 - LAMBDA_INIT)).T.astype(bf16)


def _diff(lq1, lk1, lq2, lk2, subln_gcol, proj):
    vec = lambda n: pl.BlockSpec((1, n), lambda b, h: (0, 0))
    return pl.pallas_call(
        _diff_kernel,
        grid=(BATCH, DIFF_HEADS),
        in_specs=[vec(HEAD_DIM), vec(HEAD_DIM), vec(HEAD_DIM), vec(HEAD_DIM),
                  pl.BlockSpec((LANES, 1), lambda b, h: (0, 0)),
                  pl.BlockSpec((SEQ, LANES), lambda b, h: (b, COL_QD // LANES + h)),
                  pl.BlockSpec((SEQ, LANES), lambda b, h: (b, COL_KD // LANES + h)),
                  pl.BlockSpec((SEQ, LANES), lambda b, h: (b, COL_VD // LANES + h))],
        out_specs=pl.BlockSpec((SEQ, LANES), lambda b, h: (b, h)),
        out_shape=jax.ShapeDtypeStruct((N_TOK, DIFF_HEADS * LANES), bf16),
        scratch_shapes=[pltpu.VMEM((LANES, SEQ), bf16)],
        compiler_params=_cparams(("arbitrary", "arbitrary")),
        name="diff",
    )(lq1, lk1, lq2, lk2, subln_gcol, proj, proj, proj)


MERGE_TM = 256


def _split_bf16(v):
    hi = v.astype(bf16)
    lo = (v - hi.astype(f32)).astype(bf16)
    return hi, lo


def _merge_kernel(oa_ref, ob_ref, ga_ref, gb_ref, x_ref, gt_ref, sc_ref, sh_ref, g_ref,
                  wa_ref, wb_ref, wo_ref, wr_ref, br_ref, x1_ref, hp_ref, lt_ref, y_even, y_odd):
    i = pl.program_id(0)

    @pl.when(i == 0)
    def _():
        y_odd[...] = jnp.zeros_like(y_odd)

    def step(y_cur, y_prev):
        x1 = x_ref[...] + gt_ref[...] * y_prev[...]
        x1_ref[...] = x1
        ms = jnp.mean(x1 * x1, axis=-1, keepdims=True)
        h2 = (x1 * lax.rsqrt(ms + RMS_EPS) * g_ref[...]) * (1.0 + sc_ref[...]) + sh_ref[...]
        hb = lax.bitcast_convert_type(h2.astype(bf16).astype(f32), jnp.uint32)
        half = D_MODEL // 2
        hp_ref[...] = (hb[:, :half] >> 16) | (hb[:, half:] & jnp.uint32(0xFFFF0000))
        h_hi, h_lo = _split_bf16(h2)
        w_hi, w_lo = _split_bf16(wr_ref[...])
        lt_ref[...] = _dot_nt(w_hi, h_hi) + _dot_nt(w_hi, h_lo) + _dot_nt(w_lo, h_hi) + br_ref[...]

        a = _dot(oa_ref[...], wa_ref[...])
        b = _dot(ob_ref[...], wb_ref[...])
        merged = (jax.nn.sigmoid(ga_ref[...].astype(f32)) * a + jax.nn.sigmoid(gb_ref[...].astype(f32)) * b)
        y_cur[...] = _dot(merged.astype(bf16), wo_ref[...])

    @pl.when(i % 2 == 0)
    def _():
        step(y_even, y_odd)

    @pl.when(i % 2 == 1)
    def _():
        step(y_odd, y_even)


def _merge(o_a, o_b, proj, x2, mod3, g_ffn, wa, wb, wo, wr_t, br_col):
    tm = MERGE_TM
    per_b = SEQ // tm
    nt = N_TOK // tm
    cur = lambda i: jnp.minimum(i, nt - 1)
    prev = lambda i: jnp.maximum(i - 1, 0)
    const = lambda shape: pl.BlockSpec(shape, lambda i: (0,) * len(shape), pipeline_mode=pl.Buffered(1))
    modspec = lambda chunk: pl.BlockSpec((None, 1, D_MODEL), lambda i: (prev(i) // per_b, 0, chunk))
    return pl.pallas_call(
        _merge_kernel,
        grid=(nt + 1,),
        in_specs=[pl.BlockSpec((tm, 1024), lambda i: (cur(i), 0)),
                  pl.BlockSpec((tm, 1024), lambda i: (cur(i), 0)),
                  pl.BlockSpec((tm, D_MODEL), lambda i: (cur(i), COL_GA // D_MODEL)),
                  pl.BlockSpec((tm, D_MODEL), lambda i: (cur(i), COL_GB // D_MODEL)),
                  pl.BlockSpec((tm, D_MODEL), lambda i: (prev(i), 0)),
                  modspec(2), modspec(4), modspec(3),
                  const((1, D_MODEL)),
                  const((1024, D_MODEL)), const((1024, D_MODEL)), const((D_MODEL, D_MODEL)),
                  const((ROUTER_ROWS, D_MODEL)), const((ROUTER_ROWS, 1))],
        out_specs=[pl.BlockSpec((tm, D_MODEL), lambda i: (prev(i), 0)),
                   pl.BlockSpec((tm, D_MODEL // 2), lambda i: (prev(i), 0)),
                   pl.BlockSpec((ROUTER_ROWS, tm), lambda i: (0, prev(i)))],
        out_shape=[jax.ShapeDtypeStruct((N_TOK, D_MODEL), f32),
                   jax.ShapeDtypeStruct((N_TOK, D_MODEL // 2), jnp.uint32),
                   jax.ShapeDtypeStruct((ROUTER_ROWS, N_TOK), f32)],
        scratch_shapes=[pltpu.VMEM((tm, D_MODEL), f32), pltpu.VMEM((tm, D_MODEL), f32)],
        compiler_params=_cparams(("arbitrary",)),
        name="merge",
    )(o_a, o_b, proj, proj, x2, mod3, mod3, mod3, g_ffn, wa, wb, wo, wr_t, br_col)


def _first_argmax(v, iota_f):
    vmax = jnp.max(v, axis=0, keepdims=True)
    idx = jnp.min(jnp.where(v == vmax, iota_f, float(v.shape[0])), axis=0, keepdims=True)
    return vmax, idx


def _route_kernel(lt_ref, ints_ref, w_ref, cnt_ref, carry):
    @pl.when(pl.program_id(0) == 0)
    def _():
        carry[...] = jnp.zeros_like(carry)

    cw = 256
    io8 = lax.broadcasted_iota(i32, (N_GROUPS, cw), 0).astype(f32)
    io64 = lax.broadcasted_iota(i32, (N_EXPERTS, cw), 0).astype(f32)
    tri = jnp.where(lax.broadcasted_iota(i32, (cw, cw), 0) < lax.broadcasted_iota(i32, (cw, cw), 1),
                    1.0, 0.0).astype(bf16)
    ints_ref[4:8, :] = jnp.zeros((4, ROUTE_T), i32)
    w_ref[2:8, :] = jnp.zeros((6, ROUTE_T), f32)
    for c in range(ROUTE_T // cw):
        sl = slice(c * cw, (c + 1) * cw)
        gl = lt_ref[0:N_GROUPS, sl]
        gmax, gsel = _first_argmax(gl, io8)
        gw = 1.0 / jnp.sum(jnp.exp(gl - gmax), axis=0, keepdims=True)
        es = jnp.zeros((EXPERTS_PER_GROUP, cw), f32)
        for g in range(N_GROUPS):
            lo = N_GROUPS + g * EXPERTS_PER_GROUP
            es = jnp.where(gsel == float(g), lt_ref[lo:lo + EXPERTS_PER_GROUP, sl], es)
        v1, i1 = _first_argmax(es, io8)
        es2 = jnp.where(io8 == i1, -jnp.inf, es)
        v2, i2 = _first_argmax(es2, io8)
        e = jnp.exp(v2 - v1)
        w_ref[0:1, sl] = gw / (1.0 + e)
        w_ref[1:2, sl] = gw * e / (1.0 + e)
        e1 = gsel * float(EXPERTS_PER_GROUP) + i1
        e2 = gsel * float(EXPERTS_PER_GROUP) + i2
        oh1 = io64 == e1
        oh2 = io64 == e2
        oc = jnp.where(oh1, 1.0, jnp.where(oh2, 1.0, 0.0))
        pre = _dot(oc.astype(bf16), tri) + carry[...]
        ints_ref[0:1, sl] = e1.astype(i32)
        ints_ref[1:2, sl] = e2.astype(i32)
        ints_ref[2:3, sl] = jnp.sum(jnp.where(oh1, pre, 0.0), axis=0, keepdims=True).astype(i32)
        ints_ref[3:4, sl] = jnp.sum(jnp.where(oh2, pre, 0.0), axis=0, keepdims=True).astype(i32)
        carry[...] = carry[...] + jnp.sum(oc, axis=1, keepdims=True)
    cnt_ref[...] = jnp.broadcast_to(carry[...], (N_EXPERTS, LANES)).astype(i32)


def _route(logits_t):
    t = ROUTE_T
    return pl.pallas_call(
        _route_kernel,
        grid=(N_TOK // t,),
        in_specs=[pl.BlockSpec((ROUTER_ROWS, t), lambda c: (0, c))],
        out_specs=[pl.BlockSpec((8, t), lambda c: (0, c)),
                   pl.BlockSpec((8, t), lambda c: (0, c)),
                   pl.BlockSpec((N_EXPERTS, LANES), lambda c: (0, 0))],
        out_shape=[jax.ShapeDtypeStruct((8, N_TOK), i32),
                   jax.ShapeDtypeStruct((8, N_TOK), f32),
                   jax.ShapeDtypeStruct((N_EXPERTS, LANES), i32)],
        scratch_shapes=[pltpu.VMEM((N_EXPERTS, 1), f32)],
        compiler_params=_cparams(("arbitrary",)),
        name="route",
    )(logits_t)


def _plan_kernel(cnt_ref, ints_ref, pos_ref, blk_u_ref, used_e_ref, meta_ref, pstart_s):
    @pl.when(pl.program_id(0) == 0)
    def _():
        def per_expert(e, carry):
            b, u = carry
            nb = (cnt_ref[e] + (MOE_TM - 1)) // MOE_TM
            pstart_s[e] = b * MOE_TM
            used_e_ref[u] = e

            def mark(k, c):
                blk_u_ref[b + k] = u
                return c

            lax.fori_loop(0, nb, mark, 0)
            return b + nb, u + jnp.minimum(nb, 1)

        nb_total, nu = lax.fori_loop(0, N_EXPERTS, per_expert, (0, 0))
        meta_ref[0] = nb_total
        meta_ref[1] = nu
        last_e = used_e_ref[nu - 1]

        def fill_e(k, c):
            used_e_ref[k] = last_e
            return c

        lax.fori_loop(nu, N_EXPERTS, fill_e, 0)

        def fill_b(k, c):
            blk_u_ref[k] = nu - 1
            return c

        lax.fori_loop(nb_total, MOE_NBLK, fill_b, 0)

    t = ROUTE_T
    iocol = lax.broadcasted_iota(i32, (N_EXPERTS, 1), 0)

    def build(e, col):
        return jnp.where(iocol == e, pstart_s[e], col)

    pcol = lax.fori_loop(0, N_EXPERTS, build, jnp.zeros((N_EXPERTS, 1), i32)).astype(f32)
    io64 = lax.broadcasted_iota(i32, (N_EXPERTS, t), 0)
    e1 = ints_ref[0:1, :]
    e2 = ints_ref[1:2, :]
    pos_ref[0:1, :] = (jnp.sum(jnp.where(io64 == e1, pcol, 0.0), axis=0, keepdims=True).astype(i32)
                       + ints_ref[2:3, :])
    pos_ref[1:2, :] = (jnp.sum(jnp.where(io64 == e2, pcol, 0.0), axis=0, keepdims=True).astype(i32)
                       + ints_ref[3:4, :])
    pos_ref[2:8, :] = jnp.zeros((6, t), i32)


def _plan(counts, ints):
    t = ROUTE_T
    return pl.pallas_call(
        _plan_kernel,
        grid_spec=pltpu.PrefetchScalarGridSpec(
            num_scalar_prefetch=1,
            grid=(N_TOK // t,),
            in_specs=[pl.BlockSpec((8, t), lambda c, cnt: (0, c))],
            out_specs=[pl.BlockSpec((8, t), lambda c, cnt: (0, c)),
                       pl.BlockSpec(memory_space=pltpu.SMEM),
                       pl.BlockSpec(memory_space=pltpu.SMEM),
                       pl.BlockSpec(memory_space=pltpu.SMEM),
                       pl.BlockSpec(memory_space=pltpu.SMEM)]),
        out_shape=[jax.ShapeDtypeStruct((8, N_TOK), i32),
                   jax.ShapeDtypeStruct((MOE_NBLK,), i32),
                   jax.ShapeDtypeStruct((N_EXPERTS,), i32),
                   jax.ShapeDtypeStruct((2,), i32),
                   jax.ShapeDtypeStruct((N_EXPERTS,), i32)],
        compiler_params=_cparams(("arbitrary",)),
        name="plan",
    )(counts, ints)


DMA_UNROLL = 8


def _invert_kernel(pos_ref, cnt_ref, pstart_ref, meta_ref, row_tok_ref):
    def pads(e, c):
        n = cnt_ref[e]
        start = pstart_ref[e] + n
        last = pstart_ref[e] + ((n + (MOE_TM - 1)) // MOE_TM) * MOE_TM - 1

        def group(g, c2):
            for k in range(DMA_UNROLL):
                row_tok_ref[jnp.maximum(jnp.minimum(start + g * DMA_UNROLL + k, last), 0)] = 0
            return c2

        lax.fori_loop(0, MOE_TM // DMA_UNROLL, group, 0)
        return c

    lax.fori_loop(0, N_EXPERTS, pads, 0)

    def tail(g, c):
        for k in range(DMA_UNROLL):
            row_tok_ref[g * DMA_UNROLL + k] = 0
        return c

    lax.fori_loop(meta_ref[0] * (MOE_TM // DMA_UNROLL), MOE_ROWS // DMA_UNROLL, tail, 0)

    def tokens(g, c):
        for u in range(DMA_UNROLL):
            t = g * DMA_UNROLL + u
            row_tok_ref[pos_ref[t]] = t
            row_tok_ref[pos_ref[N_TOK + t]] = t
        return c

    lax.fori_loop(0, N_TOK // DMA_UNROLL, tokens, 0)


def _invert(pos_flat, counts, pstart, meta):
    return pl.pallas_call(
        _invert_kernel,
        grid_spec=pltpu.PrefetchScalarGridSpec(
            num_scalar_prefetch=4,
            grid=(1,),
            in_specs=[],
            out_specs=pl.BlockSpec(memory_space=pltpu.SMEM)),
        out_shape=jax.ShapeDtypeStruct((MOE_ROWS,), i32),
        compiler_params=_cparams(("arbitrary",)),
        name="invert",
    )(pos_flat, counts, pstart, meta)


MOE_KC = 512
N_GU_CH = D_MODEL // MOE_KC
N_DN_CH = D_EXPERT // MOE_KC
N_CH = N_GU_CH + N_DN_CH


def _moe_kernel(blk_u, used_e, meta, row_tok, hp_hbm, wgu_hbm, wdn_hbm, y_ref, ring, wgu_b, wdn_b, xbuf, sems, gsems):
    i = pl.program_id(0)
    u = blk_u[i]
    first = (i == 0) | (u != blk_u[jnp.maximum(i - 1, 0)])
    active = i < meta[0]
    slot = i % 2
    nxt = jnp.minimum(i + 1, meta[0] - 1)

    def chunk_copy(e, c):
        if c < N_GU_CH:
            src = wgu_hbm.at[e, pl.ds(c * MOE_KC, MOE_KC), :]
        else:
            src = wdn_hbm.at[e, pl.ds((c - N_GU_CH) * MOE_KC, MOE_KC), :]
        return pltpu.make_async_copy(src, ring.at[c], sems.at[c])

    def row_copy(b, r, s):
        tok = row_tok[b * MOE_TM + r]
        return pltpu.make_async_copy(hp_hbm.at[pl.ds(tok, 1), :], xbuf.at[s, pl.ds(r, 1), :], gsems.at[s])

    def gather_next(r0, r1):
        for r in range(r0, r1):
            row_copy(nxt, r, 1 - slot).start()

    def drain_rows(s):
        def body(g, carry):
            for _ in range(DMA_UNROLL):
                row_copy(0, 0, s).wait()
            return carry
        lax.fori_loop(0, MOE_TM // DMA_UNROLL, body, 0)

    @pl.when(i == 0)
    def _():
        for c in range(N_CH):
            chunk_copy(used_e[0], c).start()

        def body(g, carry):
            for k in range(DMA_UNROLL):
                row_copy(0, g * DMA_UNROLL + k, 0).start()
            return carry
        lax.fori_loop(0, MOE_TM // DMA_UNROLL, body, 0)

    @pl.when(active)
    def _():
        drain_rows(slot)

    def unpack_x():
        packed = xbuf[slot]
        x_lo = lax.bitcast_convert_type(packed << 16, f32).astype(bf16)
        x_hi = lax.bitcast_convert_type(packed & jnp.uint32(0xFFFF0000), f32).astype(bf16)
        return x_lo, x_hi

    def swiglu(gu):
        g = gu[:, :D_EXPERT]
        return (g * jax.nn.sigmoid(g) * gu[:, D_EXPERT:]).astype(bf16)

    def rows(c):
        return slice(c * MOE_KC, (c + 1) * MOE_KC)

    @pl.when(active & first)
    def _():
        e_next = used_e[jnp.minimum(u + 1, meta[1] - 1)]

        def take(c):
            chunk_copy(e_next, c).wait()
            wb = ring[c].astype(bf16)
            if c < N_GU_CH:
                wgu_b[rows(c), :] = wb
            else:
                wdn_b[rows(c - N_GU_CH), :] = wb
            chunk_copy(e_next, c).start()
            gather_next(c * MOE_TM // N_CH, (c + 1) * MOE_TM // N_CH)

        x_lo, x_hi = unpack_x()
        per_half = N_GU_CH // 2
        x_chunks = [xh[:, rows(c)] for xh in (x_lo, x_hi) for c in range(per_half)]
        take(0)
        gu = None
        for c in range(N_GU_CH):
            take(c + 1)
            d = _dot(x_chunks[c], wgu_b[rows(c), :])
            gu = d if gu is None else gu + d
        a = swiglu(gu)
        y = None
        for c in range(N_DN_CH):
            if c + 1 < N_DN_CH:
                take(N_GU_CH + c + 1)
            d = _dot(a[:, rows(c)], wdn_b[rows(c), :])
            y = d if y is None else y + d
        y_ref[...] = y

    @pl.when(active & jnp.logical_not(first))
    def _():
        x_lo, x_hi = unpack_x()
        gather_next(0, MOE_TM)
        half = D_MODEL // 2
        gu = _dot(x_lo, wgu_b[0:half, :]) + _dot(x_hi, wgu_b[half:, :])
        y_ref[...] = _dot(swiglu(gu), wdn_b[...])

    @pl.when(i == pl.num_programs(0) - 1)
    def _():
        for c in range(N_CH):
            chunk_copy(used_e[0], c).wait()
        drain_rows(meta[0] % 2)


def _moe(blk_u, used_e, meta, row_tok, hp, w_gu, w_dn):
    tm = MOE_TM
    blk = lambda i, bu, ue, mt, rt: (jnp.minimum(i, mt[0] - 1), 0)
    return pl.pallas_call(
        _moe_kernel,
        grid_spec=pltpu.PrefetchScalarGridSpec(
            num_scalar_prefetch=4,
            grid=(MOE_NBLK,),
            in_specs=[pl.BlockSpec(memory_space=pl.ANY),
                      pl.BlockSpec(memory_space=pl.ANY),
                      pl.BlockSpec(memory_space=pl.ANY)],
            out_specs=pl.BlockSpec((tm, D_MODEL), blk),
            scratch_shapes=[pltpu.VMEM((N_CH, MOE_KC, D_MODEL), f32),
                            pltpu.VMEM((D_MODEL, 2 * D_EXPERT), bf16),
                            pltpu.VMEM((D_EXPERT, D_MODEL), bf16),
                            pltpu.VMEM((2, tm, D_MODEL // 2), jnp.uint32),
                            pltpu.SemaphoreType.DMA((N_CH,)),
                            pltpu.SemaphoreType.DMA((2,))]),
        out_shape=jax.ShapeDtypeStruct((MOE_ROWS, D_MODEL), f32),
        compiler_params=_cparams(("arbitrary",)),
        name="moe",
    )(blk_u, used_e, meta, row_tok, hp, w_gu, w_dn)


COMB_TM = 256


def _combine_kernel(pos_ref, x1_ref, gt_ref, wc_ref, g_ref, y_ref, o_ref, ybuf_even, ybuf_odd, sems):
    i = pl.program_id(0)
    last = pl.num_programs(0) - 1
    nxt = jnp.minimum(i + 1, last)
    bufs = (ybuf_even, ybuf_odd)

    def row_copy(tile, r, k, s):
        src = pos_ref[k * N_TOK + tile * COMB_TM + r]
        return pltpu.make_async_copy(y_ref.at[pl.ds(src, 1), :], bufs[s].at[k, pl.ds(r, 1), :], sems.at[s])

    def drain(s):
        def body(g, carry):
            for _ in range(DMA_UNROLL):
                row_copy(0, 0, 0, s).wait()
                row_copy(0, 0, 0, s).wait()
            return carry
        lax.fori_loop(0, COMB_TM // DMA_UNROLL, body, 0)

    @pl.when(i == 0)
    def _():
        def body(g, carry):
            for u in range(DMA_UNROLL):
                r = g * DMA_UNROLL + u
                row_copy(0, r, 0, 0).start()
                row_copy(0, r, 1, 0).start()
            return carry
        lax.fori_loop(0, COMB_TM // DMA_UNROLL, body, 0)

    def step(s):
        drain(s)
        for r in range(COMB_TM):
            row_copy(nxt, r, 0, 1 - s).start()
            row_copy(nxt, r, 1, 1 - s).start()
        ffn = wc_ref[:, 0:1] * bufs[s][0] + wc_ref[:, 1:2] * bufs[s][1]
        xo = x1_ref[...] + gt_ref[...] * ffn
        ms = jnp.mean(xo * xo, axis=-1, keepdims=True)
        o_ref[...] = xo * lax.rsqrt(ms + RMS_EPS) * g_ref[...]

        @pl.when(i == last)
        def _():
            drain(1 - s)

    @pl.when(i % 2 == 0)
    def _():
        step(0)

    @pl.when(i % 2 == 1)
    def _():
        step(1)


def _combine(pos_flat, x1, mod3, w_col, g_final, y):
    tm = COMB_TM
    per_b = SEQ // tm
    return pl.pallas_call(
        _combine_kernel,
        grid_spec=pltpu.PrefetchScalarGridSpec(
            num_scalar_prefetch=1,
            grid=(N_TOK // tm,),
            in_specs=[pl.BlockSpec((tm, D_MODEL), lambda i, pos: (i, 0)),
                      pl.BlockSpec((None, 1, D_MODEL), lambda i, pos: (i // per_b, 0, 5)),
                      pl.BlockSpec((tm, 8), lambda i, pos: (i, 0)),
                      pl.BlockSpec((1, D_MODEL), lambda i, pos: (0, 0)),
                      pl.BlockSpec(memory_space=pl.ANY)],
            out_specs=pl.BlockSpec((tm, D_MODEL), lambda i, pos: (i, 0)),
            scratch_shapes=[pltpu.VMEM((TOP_K, tm, D_MODEL), f32),
                            pltpu.VMEM((TOP_K, tm, D_MODEL), f32),
                            pltpu.SemaphoreType.DMA((2,))]),
        out_shape=jax.ShapeDtypeStruct((N_TOK, D_MODEL), f32),
        compiler_params=_cparams(("arbitrary",)),
        name="combine",
    )(pos_flat, x1, mod3, w_col, g_final, y)


def kernel(x, c, positions, w_ada, b_ada, g_mix, w_in, swa_sinks, diff_lambda_q1, diff_lambda_k1, diff_lambda_q2, diff_lambda_k2, diff_subln_g, w_branch_a, w_branch_b, w_out, g_ffn, w_router_group, b_router_group, w_router_expert, b_router_expert, w_exp_gate_up, w_exp_down, g_final):
    layer = 0
    x2 = x.reshape(N_TOK, D_MODEL)

    mod = _ada(c, w_ada[layer], b_ada[layer])
    mod3 = mod.reshape(BATCH, 1, 6 * D_MODEL)

    w = w_in[layer]
    s_qa, s_ka, s_va, s_qd, s_kd, s_vd, s_ga = np.cumsum([1024, 128, 128, 1024, 1024, 1024, 2048])
    w_main = jnp.concatenate([w[:, :s_qa], w[:, s_va:]], axis=1).astype(bf16)
    kv_heads = [w[:, s_qa + h * HEAD_DIM:s_qa + (h + 1) * HEAD_DIM] for h in range(2 * SWA_KV_HEADS)]
    w_kv = jnp.concatenate([h for h in kv_heads for _ in range(2)], axis=1).astype(bf16)
    half = HEAD_DIM // 2
    inv_freq = ROPE_THETA ** (-jnp.arange(0, HEAD_DIM, 2, dtype=f32) / HEAD_DIM)
    invf = jnp.tile(inv_freq, LANES // half).reshape(1, LANES)
    pos4 = jnp.repeat(positions.reshape(N_TOK // 4, 4).astype(i32), half, axis=1)
    cos4, sin4 = _rope_tables(pos4, invf)
    cos_t = jnp.tile(cos4.reshape(N_TOK, half), (1, LANES // half))
    sin_t = jnp.tile(sin4.reshape(N_TOK, half), (1, LANES // half))

    proj = _proj(x2, mod3, g_mix[layer].reshape(1, D_MODEL), cos_t, sin_t, w_main, w_kv)
    o_a = _swa(swa_sinks[layer], proj)
    o_b = _diff(diff_lambda_q1[layer].reshape(1, HEAD_DIM), diff_lambda_k1[layer].reshape(1, HEAD_DIM),
                diff_lambda_q2[layer].reshape(1, HEAD_DIM), diff_lambda_k2[layer].reshape(1, HEAD_DIM),
                diff_subln_g[layer].reshape(LANES, 1), proj)

    wr = jnp.concatenate([w_router_group[layer], w_router_expert[layer]], axis=1)
    wr_t = jnp.pad(wr.T, ((0, ROUTER_ROWS - wr.shape[1]), (0, 0)))
    br = jnp.concatenate([b_router_group[layer], b_router_expert[layer]])
    br_col = jnp.pad(br, (0, ROUTER_ROWS - br.shape[0])).reshape(ROUTER_ROWS, 1)
    x1, hp, logits_t = _merge(o_a, o_b, proj, x2, mod3, g_ffn[layer].reshape(1, D_MODEL),
                              w_branch_a[layer].astype(bf16), w_branch_b[layer].astype(bf16),
                              w_out[layer].astype(bf16), wr_t, br_col)

    ints, wts, counts = _route(logits_t)
    cnt = counts[:, 0]
    pos, blk_u, used_e, meta, pstart = _plan(cnt, ints)
    pos_flat = pos[0:TOP_K].reshape(TOP_K * N_TOK)
    row_tok = _invert(pos_flat, cnt, pstart, meta)
    y = _moe(blk_u, used_e, meta, row_tok, hp, w_exp_gate_up[layer], w_exp_down[layer])
    out = _combine(pos_flat, x1, mod3, wts.T, g_final.reshape(1, D_MODEL), y)
    return out.reshape(BATCH, SEQ, D_MODEL)
```

```python
import functools
import math

import jax
import jax.numpy as jnp
import numpy as np
from jax import lax
from jax.experimental import pallas as pl
from jax.experimental.pallas import tpu as pltpu

D_MODEL = 2048
BATCH = 8
SEQ = 2048
N_TOK = BATCH * SEQ
HEAD_DIM = 64
ROPE_THETA = 10000.0
RMS_EPS = 1e-6
NEG_INF = -1e30
SWA_Q_HEADS = 16
SWA_KV_HEADS = 2
SWA_GROUP = SWA_Q_HEADS // SWA_KV_HEADS
WINDOW = 128
DIFF_HEADS = 8
N_GROUPS = 8
EXPERTS_PER_GROUP = 8
N_EXPERTS = N_GROUPS * EXPERTS_PER_GROUP
TOP_K = 2
D_EXPERT = D_MODEL // 2
LAMBDA_INIT = 0.8 - 0.6 * math.exp(-0.3 * 0)
LOG2_E = math.log2(math.e)

LANES = 128
VMEM_LIMIT = 56 * 1024 * 1024

PROJ_TN = 512
PROJ_MAIN_TILES = 16
PROJ_COLS = (PROJ_MAIN_TILES + 1) * PROJ_TN
COL_QA, COL_QD, COL_KD, COL_VD, COL_GA, COL_GB, COL_KVA = 0, 1024, 2048, 3072, 4096, 6144, 8192

MOE_TM = 256
MOE_ROWS = N_TOK * TOP_K + N_EXPERTS * MOE_TM
MOE_NBLK = MOE_ROWS // MOE_TM
ROUTE_T = 2048
ROUTER_ROWS = 128

f32 = jnp.float32
bf16 = jnp.bfloat16
i32 = jnp.int32


def _cparams(sem, vmem=VMEM_LIMIT):
    return pltpu.CompilerParams(dimension_semantics=sem, vmem_limit_bytes=vmem)


def _dot(a, b):
    return jnp.dot(a, b, preferred_element_type=f32)


def _dot_nt(a, b):
    return lax.dot_general(a, b, (((1,), (1,)), ((), ())), preferred_element_type=f32)


def _ada_kernel(c_ref, w_ref, b_ref, o_ref):
    c = c_ref[...]
    ca = (c * jax.nn.sigmoid(c)).astype(bf16)
    o_ref[...] = _dot(ca, w_ref[...].astype(bf16)) + b_ref[...]


def _ada(c, w_ada, b_ada):
    tn = 512
    n = w_ada.shape[1]
    return pl.pallas_call(
        _ada_kernel,
        grid=(n // tn,),
        in_specs=[pl.BlockSpec((BATCH, D_MODEL), lambda j: (0, 0)),
                  pl.BlockSpec((D_MODEL, tn), lambda j: (0, j)),
                  pl.BlockSpec((1, tn), lambda j: (0, j))],
        out_specs=pl.BlockSpec((BATCH, tn), lambda j: (0, j)),
        out_shape=jax.ShapeDtypeStruct((BATCH, n), f32),
        compiler_params=_cparams(("arbitrary",)),
        name="ada",
    )(c, w_ada, b_ada.reshape(1, n))


PROJ_TM = 1024


def _rope128(xc, cos, sin_signed, first_half):
    r_prev = pltpu.roll(xc, 32, 1)
    r_next = pltpu.roll(xc, 96, 1)
    return xc * cos + jnp.where(first_half, r_next, r_prev) * sin_signed


def _rope_table_kernel(pos_ref, invf_ref, cos_ref, sin_ref):
    ang = pos_ref[...].astype(f32) * invf_ref[...]
    cos_ref[...] = jnp.cos(ang)
    sin_ref[...] = jnp.sin(ang)


def _rope_tables(pos4, invf):
    rows = pos4.shape[0]
    tr = 512
    spec = pl.BlockSpec((tr, LANES), lambda i: (i, 0))
    return pl.pallas_call(
        _rope_table_kernel,
        grid=(rows // tr,),
        in_specs=[spec, pl.BlockSpec((1, LANES), lambda i: (0, 0))],
        out_specs=[spec, spec],
        out_shape=[jax.ShapeDtypeStruct((rows, LANES), f32)] * 2,
        compiler_params=_cparams(("arbitrary",)),
        name="rope_tables",
    )(pos4, invf)


def _proj_kernel(x_ref, sc_ref, sh_ref, g_ref, cos_ref, sin_ref, w_ref, wkv_ref, o_ref, h_scr, sin_scr):
    j = pl.program_id(1)
    lane = lax.broadcasted_iota(i32, (1, LANES), 1)
    first_half = (lane % HEAD_DIM) < (HEAD_DIM // 2)

    @pl.when(j == 0)
    def _():
        rc = 256
        for r in range(PROJ_TM // rc):
            x = x_ref[r * rc:(r + 1) * rc, :]
            ms = jnp.mean(x * x, axis=-1, keepdims=True)
            y = x * lax.rsqrt(ms + RMS_EPS) * g_ref[...]
            h = y * (1.0 + sc_ref[...]) + sh_ref[...]
            h_scr[r * rc:(r + 1) * rc, :] = h.astype(bf16)
        sin_scr[...] = sin_ref[...] * jnp.where(first_half, -1.0, 1.0)

    hw = PROJ_TN // 2

    def rope_half(acc, off, scale):
        for c in range(hw // LANES):
            xc = acc[:, c * LANES:(c + 1) * LANES]
            o_ref[:, off + c * LANES:off + (c + 1) * LANES] = (
                _rope128(xc, cos_ref[...], sin_scr[...], first_half) * scale).astype(bf16)

    def cast_half(acc, off, scale):
        o_ref[:, off:off + hw] = acc.astype(bf16)

    def tile(w, epilogues, scale=1.0):
        for s, epilogue in enumerate(epilogues):
            epilogue(_dot(h_scr[...], w[:, s * hw:(s + 1) * hw]), s * hw, scale)

    @pl.when(j < 2)
    def _():
        tile(w_ref, (rope_half, rope_half), 1.0 / math.sqrt(HEAD_DIM))

    @pl.when((j >= 2) & (j < 4))
    def _():
        tile(w_ref, (rope_half, rope_half), LOG2_E / math.sqrt(HEAD_DIM))

    @pl.when((j >= 4) & (j < 6))
    def _():
        tile(w_ref, (rope_half, rope_half))

    @pl.when((j >= 6) & (j < PROJ_MAIN_TILES))
    def _():
        tile(w_ref, (cast_half, cast_half))

    @pl.when(j == PROJ_MAIN_TILES)
    def _():
        tile(wkv_ref, (rope_half, cast_half))


def _proj(x2, mod3, g_mix, cos_t, sin_t, w_main, w_kv):
    tm = PROJ_TM
    per_b = SEQ // tm
    return pl.pallas_call(
        _proj_kernel,
        grid=(N_TOK // tm, PROJ_COLS // PROJ_TN),
        in_specs=[pl.BlockSpec((tm, D_MODEL), lambda i, j: (i, 0)),
                  pl.BlockSpec((None, 1, D_MODEL), lambda i, j: (i // per_b, 0, 1)),
                  pl.BlockSpec((None, 1, D_MODEL), lambda i, j: (i // per_b, 0, 0)),
                  pl.BlockSpec((1, D_MODEL), lambda i, j: (0, 0)),
                  pl.BlockSpec((tm, LANES), lambda i, j: (i, 0)),
                  pl.BlockSpec((tm, LANES), lambda i, j: (i, 0)),
                  pl.BlockSpec((D_MODEL, PROJ_TN), lambda i, j: (0, jnp.minimum(j, PROJ_MAIN_TILES - 1))),
                  pl.BlockSpec((D_MODEL, PROJ_TN), lambda i, j: (0, 0))],
        out_specs=pl.BlockSpec((tm, PROJ_TN), lambda i, j: (i, j)),
        out_shape=jax.ShapeDtypeStruct((N_TOK, PROJ_COLS), bf16),
        scratch_shapes=[pltpu.VMEM((tm, D_MODEL), bf16),
                        pltpu.VMEM((tm, LANES), f32)],
        compiler_params=_cparams(("arbitrary", "arbitrary")),
        name="proj",
    )(x2, mod3, mod3, g_mix, cos_t, sin_t, w_main, w_kv)


def _swa_kernel(sink_ref, q_ref, kv_ref, kvp_ref, o_ref):
    n = pl.program_id(1)
    w = WINDOW
    qi = lax.broadcasted_iota(i32, (w, 2 * w), 0)
    kj = lax.broadcasted_iota(i32, (w, 2 * w), 1)
    rel = qi + w - kj
    first_key = jnp.where(n > 0, 0, w)
    valid = (rel >= 0) & (rel < w) & (kj >= first_key)
    lane = lax.broadcasted_iota(i32, (w, LANES), 1)
    low = lane < HEAD_DIM
    pairs = SWA_GROUP // 2
    for kvh in range(SWA_KV_HEADS):
        ks = slice(kvh * LANES, (kvh + 1) * LANES)
        vs = slice((SWA_KV_HEADS + kvh) * LANES, (SWA_KV_HEADS + kvh + 1) * LANES)
        kk = jnp.concatenate([kvp_ref[:, ks], kv_ref[:, ks]], axis=0)
        vv = jnp.concatenate([kvp_ref[:, vs], kv_ref[:, vs]], axis=0)
        parts = []
        for pair in range(pairs):
            gp = kvh * pairs + pair
            q2 = q_ref[:, gp * LANES:(gp + 1) * LANES]
            zero = jnp.zeros_like(q2)
            parts += [jnp.where(low, q2, zero), jnp.where(low, zero, q2)]
        s_all = _dot_nt(jnp.concatenate(parts, axis=0), kk)
        ps, dens = [], []
        for g in range(SWA_GROUP):
            s = jnp.where(valid, s_all[g * w:(g + 1) * w, :], NEG_INF)
            sink = sink_ref[kvh * SWA_GROUP + g]
            m = jnp.maximum(jnp.max(s, axis=-1, keepdims=True), sink)
            p = jnp.exp(s - m)
            dens.append(jnp.sum(p, axis=-1, keepdims=True) + jnp.exp(sink - m))
            ps.append(p.astype(bf16))
        o_all = _dot(jnp.concatenate(ps, axis=0), vv)
        for pair in range(pairs):
            gp = kvh * pairs + pair
            o_even = o_all[(2 * pair) * w:(2 * pair + 1) * w, :] / dens[2 * pair]
            o_odd = o_all[(2 * pair + 1) * w:(2 * pair + 2) * w, :] / dens[2 * pair + 1]
            o_ref[:, gp * LANES:(gp + 1) * LANES] = jnp.where(low, o_even, o_odd).astype(bf16)


def _swa(sinks, proj):
    nb = SEQ // WINDOW
    kv_col = COL_KVA // (4 * LANES)
    return pl.pallas_call(
        _swa_kernel,
        grid=(BATCH, nb),
        in_specs=[pl.BlockSpec(memory_space=pltpu.SMEM),
                  pl.BlockSpec((WINDOW, SWA_Q_HEADS * HEAD_DIM), lambda b, n: (b * nb + n, 0)),
                  pl.BlockSpec((WINDOW, 4 * LANES), lambda b, n: (b * nb + n, kv_col)),
                  pl.BlockSpec((WINDOW, 4 * LANES), lambda b, n: (jnp.maximum(b * nb + n - 1, 0), kv_col))],
        out_specs=pl.BlockSpec((WINDOW, SWA_Q_HEADS * HEAD_DIM), lambda b, n: (b * nb + n, 0)),
        out_shape=jax.ShapeDtypeStruct((N_TOK, SWA_Q_HEADS * HEAD_DIM), bf16),
        compiler_params=_cparams(("arbitrary", "arbitrary")),
        name="swa",
    )(sinks, proj, proj, proj)


DIFF_T = 256


def _diff_kernel(lq1_ref, lk1_ref, lq2_ref, lk2_ref, gcol_ref, q_ref, k_ref, v_ref, o_ref, vt_scr):
    t = DIFF_T
    nt = SEQ // t
    for j in range(nt):
        vt_scr[:, j * t:(j + 1) * t] = v_ref[j * t:(j + 1) * t, :].T
    lam = (jnp.exp(jnp.sum(lq1_ref[...] * lk1_ref[...], axis=-1, keepdims=True))
           - jnp.exp(jnp.sum(lq2_ref[...] * lk2_ref[...], axis=-1, keepdims=True)) + LAMBDA_INIT)
    lane = lax.broadcasted_iota(i32, (t, LANES), 1)
    causal = lax.broadcasted_iota(i32, (t, t), 0) <= lax.broadcasted_iota(i32, (t, t), 1)

    def one_map(qz, qi):
        lo = qi * t
        s_d = jnp.where(causal, _dot_nt(k_ref[lo:lo + t, :], qz), NEG_INF)
        m = jnp.max(s_d, axis=0, keepdims=True)
        if qi > 0:
            s_u = _dot_nt(k_ref[0:lo, :], qz)
            m = jnp.maximum(m, jnp.max(s_u, axis=0, keepdims=True))
        p_d = jnp.exp2(s_d - m)
        l = jnp.sum(p_d, axis=0, keepdims=True)
        p_u = None
        if qi > 0:
            p_u = jnp.exp2(s_u - m)
            l = l + jnp.sum(p_u, axis=0, keepdims=True)
        return p_d, p_u, 1.0 / l

    for qi in range(nt):
        lo = qi * t
        q = q_ref[lo:lo + t, :]
        zero = jnp.zeros_like(q)
        pd1, pu1, r1 = one_map(jnp.where(lane < HEAD_DIM, q, zero), qi)
        pd2, pu2, r2 = one_map(jnp.where(lane >= HEAD_DIM, q, zero), qi)
        r2 = lam * r2
        o = _dot(vt_scr[:, lo:lo + t], (pd1 * r1 - pd2 * r2).astype(bf16))
        if qi > 0:
            o = o + _dot(vt_scr[:, 0:lo], (pu1 * r1 - pu2 * r2).astype(bf16))
        y = o * lax.rsqrt(jnp.mean(o * o, axis=0, keepdims=True) + RMS_EPS) * gcol_ref[...]
        o_ref[qi * t:(qi + 1) * t, :] = (y * (1.0 - LAMBDA_INIT)).T.astype(bf16)


def _diff(lq1, lk1, lq2, lk2, subln_gcol, proj):
    vec = lambda n: pl.BlockSpec((1, n), lambda b, h: (0, 0))
    return pl.pallas_call(
        _diff_kernel,
        grid=(BATCH, DIFF_HEADS),
        in_specs=[vec(HEAD_DIM), vec(HEAD_DIM), vec(HEAD_DIM), vec(HEAD_DIM),
                  pl.BlockSpec((LANES, 1), lambda b, h: (0, 0)),
                  pl.BlockSpec((SEQ, LANES), lambda b, h: (b, COL_QD // LANES + h)),
                  pl.BlockSpec((SEQ, LANES), lambda b, h: (b, COL_KD // LANES + h)),
                  pl.BlockSpec((SEQ, LANES), lambda b, h: (b, COL_VD // LANES + h))],
        out_specs=pl.BlockSpec((SEQ, LANES), lambda b, h: (b, h)),
        out_shape=jax.ShapeDtypeStruct((N_TOK, DIFF_HEADS * LANES), bf16),
        scratch_shapes=[pltpu.VMEM((LANES, SEQ), bf16)],
        compiler_params=_cparams(("arbitrary", "arbitrary")),
        name="diff",
    )(lq1, lk1, lq2, lk2, subln_gcol, proj, proj, proj)


MERGE_TM = 256


def _merge_kernel(oa_ref, ob_ref, ga_ref, gb_ref, x_ref, gt_ref, sc_ref, sh_ref, g_ref,
                  wa_ref, wb_ref, wo_ref, wr_ref, br_ref, x1_ref, hp_ref, lt_ref, y_even, y_odd):
    i = pl.program_id(0)

    @pl.when(i == 0)
    def _():
        y_odd[...] = jnp.zeros_like(y_odd)

    def step(y_cur, y_prev):
        x1 = x_ref[...] + gt_ref[...] * y_prev[...]
        x1_ref[...] = x1
        ms = jnp.mean(x1 * x1, axis=-1, keepdims=True)
        h2 = (x1 * lax.rsqrt(ms + RMS_EPS) * g_ref[...]) * (1.0 + sc_ref[...]) + sh_ref[...]
        hb = lax.bitcast_convert_type(h2.astype(bf16).astype(f32), jnp.uint32)
        half = D_MODEL // 2
        hp_ref[...] = (hb[:, :half] >> 16) | (hb[:, half:] & jnp.uint32(0xFFFF0000))
        lt_ref[...] = _dot_nt(wr_ref[...].astype(bf16), h2.astype(bf16)) + br_ref[...]

        a = _dot(oa_ref[...], wa_ref[...])
        b = _dot(ob_ref[...], wb_ref[...])
        merged = (jax.nn.sigmoid(ga_ref[...].astype(f32)) * a + jax.nn.sigmoid(gb_ref[...].astype(f32)) * b)
        y_cur[...] = _dot(merged.astype(bf16), wo_ref[...])

    @pl.when(i % 2 == 0)
    def _():
        step(y_even, y_odd)

    @pl.when(i % 2 == 1)
    def _():
        step(y_odd, y_even)


def _merge(o_a, o_b, proj, x2, mod3, g_ffn, wa, wb, wo, wr_t, br_col):
    tm = MERGE_TM
    per_b = SEQ // tm
    nt = N_TOK // tm
    cur = lambda i: jnp.minimum(i, nt - 1)
    prev = lambda i: jnp.maximum(i - 1, 0)
    const = lambda shape: pl.BlockSpec(shape, lambda i: (0,) * len(shape), pipeline_mode=pl.Buffered(1))
    modspec = lambda chunk: pl.BlockSpec((None, 1, D_MODEL), lambda i: (prev(i) // per_b, 0, chunk))
    return pl.pallas_call(
        _merge_kernel,
        grid=(nt + 1,),
        in_specs=[pl.BlockSpec((tm, 1024), lambda i: (cur(i), 0)),
                  pl.BlockSpec((tm, 1024), lambda i: (cur(i), 0)),
                  pl.BlockSpec((tm, D_MODEL), lambda i: (cur(i), COL_GA // D_MODEL)),
                  pl.BlockSpec((tm, D_MODEL), lambda i: (cur(i), COL_GB // D_MODEL)),
                  pl.BlockSpec((tm, D_MODEL), lambda i: (prev(i), 0)),
                  modspec(2), modspec(4), modspec(3),
                  const((1, D_MODEL)),
                  const((1024, D_MODEL)), const((1024, D_MODEL)), const((D_MODEL, D_MODEL)),
                  const((ROUTER_ROWS, D_MODEL)), const((ROUTER_ROWS, 1))],
        out_specs=[pl.BlockSpec((tm, D_MODEL), lambda i: (prev(i), 0)),
                   pl.BlockSpec((tm, D_MODEL // 2), lambda i: (prev(i), 0)),
                   pl.BlockSpec((ROUTER_ROWS, tm), lambda i: (0, prev(i)))],
        out_shape=[jax.ShapeDtypeStruct((N_TOK, D_MODEL), f32),
                   jax.ShapeDtypeStruct((N_TOK, D_MODEL // 2), jnp.uint32),
                   jax.ShapeDtypeStruct((ROUTER_ROWS, N_TOK), f32)],
        scratch_shapes=[pltpu.VMEM((tm, D_MODEL), f32), pltpu.VMEM((tm, D_MODEL), f32)],
        compiler_params=_cparams(("arbitrary",)),
        name="merge",
    )(o_a, o_b, proj, proj, x2, mod3, mod3, mod3, g_ffn, wa, wb, wo, wr_t, br_col)


def _first_argmax(v, iota_f):
    vmax = jnp.max(v, axis=0, keepdims=True)
    idx = jnp.min(jnp.where(v == vmax, iota_f, float(v.shape[0])), axis=0, keepdims=True)
    return vmax, idx


def _route_kernel(lt_ref, ints_ref, w_ref, cnt_ref, carry):
    @pl.when(pl.program_id(0) == 0)
    def _():
        carry[...] = jnp.zeros_like(carry)

    cw = 256
    io8 = lax.broadcasted_iota(i32, (N_GROUPS, cw), 0).astype(f32)
    io64 = lax.broadcasted_iota(i32, (N_EXPERTS, cw), 0).astype(f32)
    tri = jnp.where(lax.broadcasted_iota(i32, (cw, cw), 0) < lax.broadcasted_iota(i32, (cw, cw), 1),
                    1.0, 0.0).astype(bf16)
    ints_ref[4:8, :] = jnp.zeros((4, ROUTE_T), i32)
    w_ref[2:8, :] = jnp.zeros((6, ROUTE_T), f32)
    for c in range(ROUTE_T // cw):
        sl = slice(c * cw, (c + 1) * cw)
        gl = lt_ref[0:N_GROUPS, sl]
        gmax, gsel = _first_argmax(gl, io8)
        gw = 1.0 / jnp.sum(jnp.exp(gl - gmax), axis=0, keepdims=True)
        es = jnp.zeros((EXPERTS_PER_GROUP, cw), f32)
        for g in range(N_GROUPS):
            lo = N_GROUPS + g * EXPERTS_PER_GROUP
            es = jnp.where(gsel == float(g), lt_ref[lo:lo + EXPERTS_PER_GROUP, sl], es)
        v1, i1 = _first_argmax(es, io8)
        es2 = jnp.where(io8 == i1, -jnp.inf, es)
        v2, i2 = _first_argmax(es2, io8)
        e = jnp.exp(v2 - v1)
        w_ref[0:1, sl] = gw / (1.0 + e)
        w_ref[1:2, sl] = gw * e / (1.0 + e)
        e1 = gsel * float(EXPERTS_PER_GROUP) + i1
        e2 = gsel * float(EXPERTS_PER_GROUP) + i2
        oh1 = io64 == e1
        oh2 = io64 == e2
        oc = jnp.where(oh1, 1.0, jnp.where(oh2, 1.0, 0.0))
        pre = _dot(oc.astype(bf16), tri) + carry[...]
        ints_ref[0:1, sl] = e1.astype(i32)
        ints_ref[1:2, sl] = e2.astype(i32)
        ints_ref[2:3, sl] = jnp.sum(jnp.where(oh1, pre, 0.0), axis=0, keepdims=True).astype(i32)
        ints_ref[3:4, sl] = jnp.sum(jnp.where(oh2, pre, 0.0), axis=0, keepdims=True).astype(i32)
        carry[...] = carry[...] + jnp.sum(oc, axis=1, keepdims=True)
    cnt_ref[...] = jnp.broadcast_to(carry[...], (N_EXPERTS, LANES)).astype(i32)


def _route(logits_t):
    t = ROUTE_T
    return pl.pallas_call(
        _route_kernel,
        grid=(N_TOK // t,),
        in_specs=[pl.BlockSpec((ROUTER_ROWS, t), lambda c: (0, c))],
        out_specs=[pl.BlockSpec((8, t), lambda c: (0, c)),
                   pl.BlockSpec((8, t), lambda c: (0, c)),
                   pl.BlockSpec((N_EXPERTS, LANES), lambda c: (0, 0))],
        out_shape=[jax.ShapeDtypeStruct((8, N_TOK), i32),
                   jax.ShapeDtypeStruct((8, N_TOK), f32),
                   jax.ShapeDtypeStruct((N_EXPERTS, LANES), i32)],
        scratch_shapes=[pltpu.VMEM((N_EXPERTS, 1), f32)],
        compiler_params=_cparams(("arbitrary",)),
        name="route",
    )(logits_t)


def _plan_kernel(cnt_ref, ints_ref, pos_ref, blk_u_ref, used_e_ref, meta_ref, pstart_s):
    @pl.when(pl.program_id(0) == 0)
    def _():
        def per_expert(e, carry):
            b, u = carry
            nb = (cnt_ref[e] + (MOE_TM - 1)) // MOE_TM
            pstart_s[e] = b * MOE_TM
            used_e_ref[u] = e

            def mark(k, c):
                blk_u_ref[b + k] = u
                return c

            lax.fori_loop(0, nb, mark, 0)
            return b + nb, u + jnp.minimum(nb, 1)

        nb_total, nu = lax.fori_loop(0, N_EXPERTS, per_expert, (0, 0))
        meta_ref[0] = nb_total
        meta_ref[1] = nu
        last_e = used_e_ref[nu - 1]

        def fill_e(k, c):
            used_e_ref[k] = last_e
            return c

        lax.fori_loop(nu, N_EXPERTS, fill_e, 0)

        def fill_b(k, c):
            blk_u_ref[k] = nu - 1
            return c

        lax.fori_loop(nb_total, MOE_NBLK, fill_b, 0)

    t = ROUTE_T
    iocol = lax.broadcasted_iota(i32, (N_EXPERTS, 1), 0)

    def build(e, col):
        return jnp.where(iocol == e, pstart_s[e], col)

    pcol = lax.fori_loop(0, N_EXPERTS, build, jnp.zeros((N_EXPERTS, 1), i32)).astype(f32)
    io64 = lax.broadcasted_iota(i32, (N_EXPERTS, t), 0)
    e1 = ints_ref[0:1, :]
    e2 = ints_ref[1:2, :]
    pos_ref[0:1, :] = (jnp.sum(jnp.where(io64 == e1, pcol, 0.0), axis=0, keepdims=True).astype(i32)
                       + ints_ref[2:3, :])
    pos_ref[1:2, :] = (jnp.sum(jnp.where(io64 == e2, pcol, 0.0), axis=0, keepdims=True).astype(i32)
                       + ints_ref[3:4, :])
    pos_ref[2:8, :] = jnp.zeros((6, t), i32)


def _plan(counts, ints):
    t = ROUTE_T
    return pl.pallas_call(
        _plan_kernel,
        grid_spec=pltpu.PrefetchScalarGridSpec(
            num_scalar_prefetch=1,
            grid=(N_TOK // t,),
            in_specs=[pl.BlockSpec((8, t), lambda c, cnt: (0, c))],
            out_specs=[pl.BlockSpec((8, t), lambda c, cnt: (0, c)),
                       pl.BlockSpec(memory_space=pltpu.SMEM),
                       pl.BlockSpec(memory_space=pltpu.SMEM),
                       pl.BlockSpec(memory_space=pltpu.SMEM)],
            scratch_shapes=[pltpu.SMEM((N_EXPERTS,), i32)]),
        out_shape=[jax.ShapeDtypeStruct((8, N_TOK), i32),
                   jax.ShapeDtypeStruct((MOE_NBLK,), i32),
                   jax.ShapeDtypeStruct((N_EXPERTS,), i32),
                   jax.ShapeDtypeStruct((2,), i32)],
        compiler_params=_cparams(("arbitrary",)),
        name="plan",
    )(counts, ints)


DISP_CHUNK = 512
DMA_UNROLL = 8


def _dispatch_kernel(pos_ref, hp_ref, xs_in_ref, xs_ref, sem):
    del xs_in_ref
    base = pl.program_id(0) * DISP_CHUNK

    def row_copy(r, k):
        dst = pos_ref[k * N_TOK + base + r]
        return pltpu.make_async_copy(hp_ref.at[pl.ds(r, 1), :], xs_ref.at[pl.ds(dst, 1), :], sem)

    def issue(g, carry):
        for u in range(DMA_UNROLL):
            r = g * DMA_UNROLL + u
            row_copy(r, 0).start()
            row_copy(r, 1).start()
        return carry

    lax.fori_loop(0, DISP_CHUNK // DMA_UNROLL, issue, 0)

    def drain(g, carry):
        for _ in range(DMA_UNROLL):
            row_copy(0, 0).wait()
            row_copy(0, 0).wait()
        return carry

    lax.fori_loop(0, DISP_CHUNK // DMA_UNROLL, drain, 0)


def _dispatch(pos_flat, hp, xs_zero):
    return pl.pallas_call(
        _dispatch_kernel,
        grid_spec=pltpu.PrefetchScalarGridSpec(
            num_scalar_prefetch=1,
            grid=(N_TOK // DISP_CHUNK,),
            in_specs=[pl.BlockSpec((DISP_CHUNK, D_MODEL // 2), lambda i, pos: (i, 0)),
                      pl.BlockSpec(memory_space=pl.ANY)],
            out_specs=pl.BlockSpec(memory_space=pl.ANY),
            scratch_shapes=[pltpu.SemaphoreType.DMA(())]),
        out_shape=jax.ShapeDtypeStruct((MOE_ROWS, D_MODEL // 2), jnp.uint32),
        input_output_aliases={2: 0},
        compiler_params=_cparams(("arbitrary",)),
        name="dispatch",
    )(pos_flat, hp, xs_zero)


MOE_KC = 512
N_GU_CH = D_MODEL // MOE_KC
N_DN_CH = D_EXPERT // MOE_KC
N_CH = N_GU_CH + N_DN_CH


def _moe_kernel(blk_u, used_e, meta, xs_ref, wgu_hbm, wdn_hbm, y_ref, ring, wgu_b, wdn_b, sems):
    i = pl.program_id(0)
    u = blk_u[i]
    first = (i == 0) | (u != blk_u[jnp.maximum(i - 1, 0)])
    active = i < meta[0]

    def chunk_copy(e, c):
        if c < N_GU_CH:
            src = wgu_hbm.at[e, pl.ds(c * MOE_KC, MOE_KC), :]
        else:
            src = wdn_hbm.at[e, pl.ds((c - N_GU_CH) * MOE_KC, MOE_KC), :]
        return pltpu.make_async_copy(src, ring.at[c], sems.at[c])

    @pl.when(i == 0)
    def _():
        for c in range(N_CH):
            chunk_copy(used_e[0], c).start()

    def unpack_x():
        packed = xs_ref[...]
        x_lo = lax.bitcast_convert_type(packed << 16, f32).astype(bf16)
        x_hi = lax.bitcast_convert_type(packed & jnp.uint32(0xFFFF0000), f32).astype(bf16)
        return x_lo, x_hi

    def swiglu(gu):
        g = gu[:, :D_EXPERT]
        return (g * jax.nn.sigmoid(g) * gu[:, D_EXPERT:]).astype(bf16)

    def rows(c):
        return slice(c * MOE_KC, (c + 1) * MOE_KC)

    @pl.when(active & first)
    def _():
        e_next = used_e[jnp.minimum(u + 1, meta[1] - 1)]

        def take(c):
            chunk_copy(e_next, c).wait()
            wb = ring[c].astype(bf16)
            if c < N_GU_CH:
                wgu_b[rows(c), :] = wb
            else:
                wdn_b[rows(c - N_GU_CH), :] = wb
            chunk_copy(e_next, c).start()

        x_lo, x_hi = unpack_x()
        per_half = N_GU_CH // 2
        x_chunks = [xh[:, rows(c)] for xh in (x_lo, x_hi) for c in range(per_half)]
        take(0)
        gu = None
        for c in range(N_GU_CH):
            take(c + 1)
            d = _dot(x_chunks[c], wgu_b[rows(c), :])
            gu = d if gu is None else gu + d
        a = swiglu(gu)
        y = None
        for c in range(N_DN_CH):
            if c + 1 < N_DN_CH:
                take(N_GU_CH + c + 1)
            d = _dot(a[:, rows(c)], wdn_b[rows(c), :])
            y = d if y is None else y + d
        y_ref[...] = y

    @pl.when(active & jnp.logical_not(first))
    def _():
        x_lo, x_hi = unpack_x()
        half = D_MODEL // 2
        gu = _dot(x_lo, wgu_b[0:half, :]) + _dot(x_hi, wgu_b[half:, :])
        y_ref[...] = _dot(swiglu(gu), wdn_b[...])

    @pl.when(i == pl.num_programs(0) - 1)
    def _():
        for c in range(N_CH):
            chunk_copy(used_e[0], c).wait()


def _moe(blk_u, used_e, meta, xs, w_gu, w_dn):
    tm = MOE_TM
    blk = lambda i, bu, ue, mt: (jnp.minimum(i, mt[0] - 1), 0)
    return pl.pallas_call(
        _moe_kernel,
        grid_spec=pltpu.PrefetchScalarGridSpec(
            num_scalar_prefetch=3,
            grid=(MOE_NBLK,),
            in_specs=[pl.BlockSpec((tm, D_MODEL // 2), blk),
                      pl.BlockSpec(memory_space=pl.ANY),
                      pl.BlockSpec(memory_space=pl.ANY)],
            out_specs=pl.BlockSpec((tm, D_MODEL), blk),
            scratch_shapes=[pltpu.VMEM((N_CH, MOE_KC, D_MODEL), f32),
                            pltpu.VMEM((D_MODEL, 2 * D_EXPERT), bf16),
                            pltpu.VMEM((D_EXPERT, D_MODEL), bf16),
                            pltpu.SemaphoreType.DMA((N_CH,))]),
        out_shape=jax.ShapeDtypeStruct((MOE_ROWS, D_MODEL), f32),
        compiler_params=_cparams(("arbitrary",)),
        name="moe",
    )(blk_u, used_e, meta, xs, w_gu, w_dn)


COMB_TM = 256


def _combine_kernel(pos_ref, x1_ref, gt_ref, wc_ref, g_ref, y_ref, o_ref, ybuf_even, ybuf_odd, sems):
    i = pl.program_id(0)
    last = pl.num_programs(0) - 1
    nxt = jnp.minimum(i + 1, last)
    bufs = (ybuf_even, ybuf_odd)

    def row_copy(tile, r, k, s):
        src = pos_ref[k * N_TOK + tile * COMB_TM + r]
        return pltpu.make_async_copy(y_ref.at[pl.ds(src, 1), :], bufs[s].at[k, pl.ds(r, 1), :], sems.at[s])

    def drain(s):
        def body(g, carry):
            for _ in range(DMA_UNROLL):
                row_copy(0, 0, 0, s).wait()
                row_copy(0, 0, 0, s).wait()
            return carry
        lax.fori_loop(0, COMB_TM // DMA_UNROLL, body, 0)

    @pl.when(i == 0)
    def _():
        def body(g, carry):
            for u in range(DMA_UNROLL):
                r = g * DMA_UNROLL + u
                row_copy(0, r, 0, 0).start()
                row_copy(0, r, 1, 0).start()
            return carry
        lax.fori_loop(0, COMB_TM // DMA_UNROLL, body, 0)

    def step(s):
        drain(s)
        for r in range(COMB_TM):
            row_copy(nxt, r, 0, 1 - s).start()
            row_copy(nxt, r, 1, 1 - s).start()
        ffn = wc_ref[:, 0:1] * bufs[s][0] + wc_ref[:, 1:2] * bufs[s][1]
        xo = x1_ref[...] + gt_ref[...] * ffn
        ms = jnp.mean(xo * xo, axis=-1, keepdims=True)
        o_ref[...] = xo * lax.rsqrt(ms + RMS_EPS) * g_ref[...]

        @pl.when(i == last)
        def _():
            drain(1 - s)

    @pl.when(i % 2 == 0)
    def _():
        step(0)

    @pl.when(i % 2 == 1)
    def _():
        step(1)


def _combine(pos_flat, x1, mod3, w_col, g_final, y):
    tm = COMB_TM
    per_b = SEQ // tm
    return pl.pallas_call(
        _combine_kernel,
        grid_spec=pltpu.PrefetchScalarGridSpec(
            num_scalar_prefetch=1,
            grid=(N_TOK // tm,),
            in_specs=[pl.BlockSpec((tm, D_MODEL), lambda i, pos: (i, 0)),
                      pl.BlockSpec((None, 1, D_MODEL), lambda i, pos: (i // per_b, 0, 5)),
                      pl.BlockSpec((tm, 8), lambda i, pos: (i, 0)),
                      pl.BlockSpec((1, D_MODEL), lambda i, pos: (0, 0)),
                      pl.BlockSpec(memory_space=pl.ANY)],
            out_specs=pl.BlockSpec((tm, D_MODEL), lambda i, pos: (i, 0)),
            scratch_shapes=[pltpu.VMEM((TOP_K, tm, D_MODEL), f32),
                            pltpu.VMEM((TOP_K, tm, D_MODEL), f32),
                            pltpu.SemaphoreType.DMA((2,))]),
        out_shape=jax.ShapeDtypeStruct((N_TOK, D_MODEL), f32),
        compiler_params=_cparams(("arbitrary",)),
        name="combine",
    )(pos_flat, x1, mod3, w_col, g_final, y)


def kernel(x, c, positions, w_ada, b_ada, g_mix, w_in, swa_sinks, diff_lambda_q1, diff_lambda_k1, diff_lambda_q2, diff_lambda_k2, diff_subln_g, w_branch_a, w_branch_b, w_out, g_ffn, w_router_group, b_router_group, w_router_expert, b_router_expert, w_exp_gate_up, w_exp_down, g_final):
    layer = 0
    x2 = x.reshape(N_TOK, D_MODEL)

    mod = _ada(c, w_ada[layer], b_ada[layer])
    mod3 = mod.reshape(BATCH, 1, 6 * D_MODEL)

    w = w_in[layer]
    s_qa, s_ka, s_va, s_qd, s_kd, s_vd, s_ga = np.cumsum([1024, 128, 128, 1024, 1024, 1024, 2048])
    w_main = jnp.concatenate([w[:, :s_qa], w[:, s_va:]], axis=1).astype(bf16)
    kv_heads = [w[:, s_qa + h * HEAD_DIM:s_qa + (h + 1) * HEAD_DIM] for h in range(2 * SWA_KV_HEADS)]
    w_kv = jnp.concatenate([h for h in kv_heads for _ in range(2)], axis=1).astype(bf16)
    half = HEAD_DIM // 2
    inv_freq = ROPE_THETA ** (-jnp.arange(0, HEAD_DIM, 2, dtype=f32) / HEAD_DIM)
    invf = jnp.tile(inv_freq, LANES // half).reshape(1, LANES)
    pos4 = jnp.repeat(positions.reshape(N_TOK // 4, 4).astype(i32), half, axis=1)
    cos4, sin4 = _rope_tables(pos4, invf)
    cos_t = jnp.tile(cos4.reshape(N_TOK, half), (1, LANES // half))
    sin_t = jnp.tile(sin4.reshape(N_TOK, half), (1, LANES // half))

    proj = _proj(x2, mod3, g_mix[layer].reshape(1, D_MODEL), cos_t, sin_t, w_main, w_kv)
    o_a = _swa(swa_sinks[layer], proj)
    o_b = _diff(diff_lambda_q1[layer].reshape(1, HEAD_DIM), diff_lambda_k1[layer].reshape(1, HEAD_DIM),
                diff_lambda_q2[layer].reshape(1, HEAD_DIM), diff_lambda_k2[layer].reshape(1, HEAD_DIM),
                diff_subln_g[layer].reshape(LANES, 1), proj)

    wr = jnp.concatenate([w_router_group[layer], w_router_expert[layer]], axis=1)
    wr_t = jnp.pad(wr.T, ((0, ROUTER_ROWS - wr.shape[1]), (0, 0)))
    br = jnp.concatenate([b_router_group[layer], b_router_expert[layer]])
    br_col = jnp.pad(br, (0, ROUTER_ROWS - br.shape[0])).reshape(ROUTER_ROWS, 1)
    x1, hp, logits_t = _merge(o_a, o_b, proj, x2, mod3, g_ffn[layer].reshape(1, D_MODEL),
                              w_branch_a[layer].astype(bf16), w_branch_b[layer].astype(bf16),
                              w_out[layer].astype(bf16), wr_t, br_col)

    ints, wts, counts = _route(logits_t)
    pos, blk_u, used_e, meta = _plan(counts[:, 0], ints)
    pos_flat = pos[0:TOP_K].reshape(TOP_K * N_TOK)
    xs = _dispatch(pos_flat, hp, jnp.zeros((MOE_ROWS, D_MODEL // 2), jnp.uint32))
    y = _moe(blk_u, used_e, meta, xs, w_exp_gate_up[layer], w_exp_down[layer])
    out = _combine(pos_flat, x1, mod3, wts.T, g_final.reshape(1, D_MODEL), y)
    return out.reshape(BATCH, SEQ, D_MODEL)
```

```python
import math

import jax
import jax.numpy as jnp
from jax import lax
from jax.experimental import pallas as pl
from jax.experimental.pallas import tpu as pltpu

D_MODEL = 2048
BATCH = 8
SEQ = 2048
N_TOK = BATCH * SEQ
HEAD_DIM = 64
ROPE_THETA = 10000.0
RMS_EPS = 1e-6
NEG_INF = -1e30
SWA_Q_HEADS = 16
SWA_KV_HEADS = 2
SWA_GROUP = SWA_Q_HEADS // SWA_KV_HEADS
WINDOW = 128
DIFF_HEADS = 8
N_GROUPS = 8
EXPERTS_PER_GROUP = 8
N_EXPERTS = N_GROUPS * EXPERTS_PER_GROUP
TOP_K = 2
D_EXPERT = D_MODEL // 2
LAMBDA_INIT = 0.8 - 0.6 * math.exp(-0.3 * 0)
LOG2_E = math.log2(math.e)

LANES = 128
VMEM_LIMIT = 56 * 1024 * 1024

PROJ_TN = 512
PROJ_MAIN_TILES = 16
PROJ_COLS = (PROJ_MAIN_TILES + 1) * PROJ_TN
COL_QA, COL_QD, COL_KD, COL_VD, COL_GA, COL_GB, COL_KVA = 0, 1024, 2048, 3072, 4096, 6144, 8192

MOE_TM = 256
MOE_ROWS = N_TOK * TOP_K + N_EXPERTS * MOE_TM
MOE_NBLK = MOE_ROWS // MOE_TM
ROUTE_T = 2048
ROUTER_ROWS = 128

f32 = jnp.float32
bf16 = jnp.bfloat16
i32 = jnp.int32


def _cparams(sem, vmem=VMEM_LIMIT):
    return pltpu.CompilerParams(dimension_semantics=sem, vmem_limit_bytes=vmem)


def _dot(a, b):
    return jnp.dot(a, b, preferred_element_type=f32)


def _dot_nt(a, b):
    return lax.dot_general(a, b, (((1,), (1,)), ((), ())), preferred_element_type=f32)


def _ada_kernel(c_ref, w_ref, b_ref, o_ref):
    c = c_ref[...]
    ca = (c * jax.nn.sigmoid(c)).astype(bf16)
    o_ref[...] = _dot(ca, w_ref[...].astype(bf16)) + b_ref[...]


def _ada(c, w_ada, b_ada):
    tn = 512
    n = w_ada.shape[1]
    return pl.pallas_call(
        _ada_kernel,
        grid=(n // tn,),
        in_specs=[pl.BlockSpec((BATCH, D_MODEL), lambda j: (0, 0)),
                  pl.BlockSpec((D_MODEL, tn), lambda j: (0, j)),
                  pl.BlockSpec((1, tn), lambda j: (0, j))],
        out_specs=pl.BlockSpec((BATCH, tn), lambda j: (0, j)),
        out_shape=jax.ShapeDtypeStruct((BATCH, n), f32),
        compiler_params=_cparams(("arbitrary",)),
        name="ada",
    )(c, w_ada, b_ada.reshape(1, n))


PROJ_TM = 1024


def _rope128(xc, cos, sin_signed, first_half):
    r_prev = pltpu.roll(xc, 32, 1)
    r_next = pltpu.roll(xc, 96, 1)
    return xc * cos + jnp.where(first_half, r_next, r_prev) * sin_signed


def _rope_table_kernel(pos_ref, invf_ref, cos_ref, sin_ref):
    ang = pos_ref[...].astype(f32) * invf_ref[...]
    cos_ref[...] = jnp.cos(ang)
    sin_ref[...] = jnp.sin(ang)


def _rope_tables(pos4, invf):
    rows = pos4.shape[0]
    tr = 512
    spec = pl.BlockSpec((tr, LANES), lambda i: (i, 0))
    return pl.pallas_call(
        _rope_table_kernel,
        grid=(rows // tr,),
        in_specs=[spec, pl.BlockSpec((1, LANES), lambda i: (0, 0))],
        out_specs=[spec, spec],
        out_shape=[jax.ShapeDtypeStruct((rows, LANES), f32)] * 2,
        compiler_params=_cparams(("arbitrary",)),
        name="rope_tables",
    )(pos4, invf)


def _proj_kernel(x_ref, sc_ref, sh_ref, g_ref, cos_ref, sin_ref, w0_ref, w1_ref, wkv_ref, o_ref, h_scr, sin_scr):
    j = pl.program_id(1)
    main = (w0_ref, w1_ref)
    lane = lax.broadcasted_iota(i32, (1, LANES), 1)
    first_half = (lane % HEAD_DIM) < (HEAD_DIM // 2)

    @pl.when(j == 0)
    def _():
        rc = 256
        for r in range(PROJ_TM // rc):
            x = x_ref[r * rc:(r + 1) * rc, :]
            ms = jnp.mean(x * x, axis=-1, keepdims=True)
            y = x * lax.rsqrt(ms + RMS_EPS) * g_ref[...]
            h = y * (1.0 + sc_ref[...]) + sh_ref[...]
            h_scr[r * rc:(r + 1) * rc, :] = h.astype(bf16)
        sin_scr[...] = sin_ref[...] * jnp.where(first_half, -1.0, 1.0)

    hw = PROJ_TN // 2

    def rope_half(acc, off, scale):
        for c in range(hw // LANES):
            xc = acc[:, c * LANES:(c + 1) * LANES]
            o_ref[:, off + c * LANES:off + (c + 1) * LANES] = (
                _rope128(xc, cos_ref[...], sin_scr[...], first_half) * scale).astype(bf16)

    def cast_half(acc, off, scale):
        o_ref[:, off:off + hw] = acc.astype(bf16)

    def tile(halves, epilogues, scale=1.0):
        for s, (w, epilogue) in enumerate(zip(halves, epilogues)):
            epilogue(_dot(h_scr[...], w[...]), s * hw, scale)

    @pl.when(j < 2)
    def _():
        tile(main, (rope_half, rope_half), 1.0 / math.sqrt(HEAD_DIM))

    @pl.when((j >= 2) & (j < 4))
    def _():
        tile(main, (rope_half, rope_half), LOG2_E / math.sqrt(HEAD_DIM))

    @pl.when((j >= 4) & (j < 6))
    def _():
        tile(main, (rope_half, rope_half))

    @pl.when((j >= 6) & (j < PROJ_MAIN_TILES))
    def _():
        tile(main, (cast_half, cast_half))

    @pl.when(j == PROJ_MAIN_TILES)
    def _():
        tile((wkv_ref.at[:, 0:hw], wkv_ref.at[:, hw:2 * hw]), (rope_half, cast_half))


def _proj(x2, mod3, g_mix, cos_t, sin_t, w_bf, w_kv):
    tm = PROJ_TM
    per_b = SEQ // tm
    hw = PROJ_TN // 2
    skip = (SWA_Q_HEADS * HEAD_DIM + 2 * SWA_KV_HEADS * HEAD_DIM) // hw - SWA_Q_HEADS * HEAD_DIM // hw

    def wcol(half):
        def index(i, j):
            jj = jnp.minimum(j, PROJ_MAIN_TILES - 1)
            return 0, 2 * jj + half + jnp.where(jj >= SWA_Q_HEADS * HEAD_DIM // PROJ_TN, skip, 0)
        return index
    return pl.pallas_call(
        _proj_kernel,
        grid=(N_TOK // tm, PROJ_COLS // PROJ_TN),
        in_specs=[pl.BlockSpec((tm, D_MODEL), lambda i, j: (i, 0)),
                  pl.BlockSpec((None, 1, D_MODEL), lambda i, j: (i // per_b, 0, 1)),
                  pl.BlockSpec((None, 1, D_MODEL), lambda i, j: (i // per_b, 0, 0)),
                  pl.BlockSpec((1, D_MODEL), lambda i, j: (0, 0)),
                  pl.BlockSpec((tm, LANES), lambda i, j: (i, 0)),
                  pl.BlockSpec((tm, LANES), lambda i, j: (i, 0)),
                  pl.BlockSpec((D_MODEL, hw), wcol(0)),
                  pl.BlockSpec((D_MODEL, hw), wcol(1)),
                  pl.BlockSpec((D_MODEL, PROJ_TN), lambda i, j: (0, 0))],
        out_specs=pl.BlockSpec((tm, PROJ_TN), lambda i, j: (i, j)),
        out_shape=jax.ShapeDtypeStruct((N_TOK, PROJ_COLS), bf16),
        scratch_shapes=[pltpu.VMEM((tm, D_MODEL), bf16),
                        pltpu.VMEM((tm, LANES), f32)],
        compiler_params=_cparams(("arbitrary", "arbitrary")),
        name="proj",
    )(x2, mod3, mod3, g_mix, cos_t, sin_t, w_bf, w_bf, w_kv)


def _swa_kernel(sink_ref, q_ref, kv_ref, kvp_ref, o_ref):
    n = pl.program_id(1)
    w = WINDOW
    qi = lax.broadcasted_iota(i32, (w, 2 * w), 0)
    kj = lax.broadcasted_iota(i32, (w, 2 * w), 1)
    rel = qi + w - kj
    first_key = jnp.where(n > 0, 0, w)
    valid = (rel >= 0) & (rel < w) & (kj >= first_key)
    lane = lax.broadcasted_iota(i32, (w, LANES), 1)
    low = lane < HEAD_DIM
    pairs = SWA_GROUP // 2
    for kvh in range(SWA_KV_HEADS):
        ks = slice(kvh * LANES, (kvh + 1) * LANES)
        vs = slice((SWA_KV_HEADS + kvh) * LANES, (SWA_KV_HEADS + kvh + 1) * LANES)
        kk = jnp.concatenate([kvp_ref[:, ks], kv_ref[:, ks]], axis=0)
        vv = jnp.concatenate([kvp_ref[:, vs], kv_ref[:, vs]], axis=0)
        parts = []
        for pair in range(pairs):
            gp = kvh * pairs + pair
            q2 = q_ref[:, gp * LANES:(gp + 1) * LANES]
            zero = jnp.zeros_like(q2)
            parts += [jnp.where(low, q2, zero), jnp.where(low, zero, q2)]
        s_all = _dot_nt(jnp.concatenate(parts, axis=0), kk)
        ps, dens = [], []
        for g in range(SWA_GROUP):
            s = jnp.where(valid, s_all[g * w:(g + 1) * w, :], NEG_INF)
            sink = sink_ref[kvh * SWA_GROUP + g]
            m = jnp.maximum(jnp.max(s, axis=-1, keepdims=True), sink)
            p = jnp.exp(s - m)
            dens.append(jnp.sum(p, axis=-1, keepdims=True) + jnp.exp(sink - m))
            ps.append(p.astype(bf16))
        o_all = _dot(jnp.concatenate(ps, axis=0), vv)
        for pair in range(pairs):
            gp = kvh * pairs + pair
            o_even = o_all[(2 * pair) * w:(2 * pair + 1) * w, :] / dens[2 * pair]
            o_odd = o_all[(2 * pair + 1) * w:(2 * pair + 2) * w, :] / dens[2 * pair + 1]
            o_ref[:, gp * LANES:(gp + 1) * LANES] = jnp.where(low, o_even, o_odd).astype(bf16)


def _swa(sinks, proj):
    nb = SEQ // WINDOW
    kv_col = COL_KVA // (4 * LANES)
    return pl.pallas_call(
        _swa_kernel,
        grid=(BATCH, nb),
        in_specs=[pl.BlockSpec(memory_space=pltpu.SMEM),
                  pl.BlockSpec((WINDOW, SWA_Q_HEADS * HEAD_DIM), lambda b, n: (b * nb + n, 0)),
                  pl.BlockSpec((WINDOW, 4 * LANES), lambda b, n: (b * nb + n, kv_col)),
                  pl.BlockSpec((WINDOW, 4 * LANES), lambda b, n: (jnp.maximum(b * nb + n - 1, 0), kv_col))],
        out_specs=pl.BlockSpec((WINDOW, SWA_Q_HEADS * HEAD_DIM), lambda b, n: (b * nb + n, 0)),
        out_shape=jax.ShapeDtypeStruct((N_TOK, SWA_Q_HEADS * HEAD_DIM), bf16),
        compiler_params=_cparams(("arbitrary", "arbitrary")),
        name="swa",
    )(sinks, proj, proj, proj)


DIFF_T = 256


def _diff_kernel(lq1_ref, lk1_ref, lq2_ref, lk2_ref, gcol_ref, q_ref, k_ref, v_ref, o_ref, vt_scr):
    t = DIFF_T
    nt = SEQ // t
    for j in range(nt):
        vt_scr[:, j * t:(j + 1) * t] = v_ref[j * t:(j + 1) * t, :].T
    lam = (jnp.exp(jnp.sum(lq1_ref[...] * lk1_ref[...], axis=-1, keepdims=True))
           - jnp.exp(jnp.sum(lq2_ref[...] * lk2_ref[...], axis=-1, keepdims=True)) + LAMBDA_INIT)
    lane = lax.broadcasted_iota(i32, (t, LANES), 1)
    causal = lax.broadcasted_iota(i32, (t, t), 0) <= lax.broadcasted_iota(i32, (t, t), 1)

    def one_map(qz, qi):
        lo = qi * t
        s_d = jnp.where(causal, _dot_nt(k_ref[lo:lo + t, :], qz), NEG_INF)
        m = jnp.max(s_d, axis=0, keepdims=True)
        if qi > 0:
            s_u = _dot_nt(k_ref[0:lo, :], qz)
            m = jnp.maximum(m, jnp.max(s_u, axis=0, keepdims=True))
        p_d = jnp.exp2(s_d - m)
        l = jnp.sum(p_d, axis=0, keepdims=True)
        p_u = None
        if qi > 0:
            p_u = jnp.exp2(s_u - m)
            l = l + jnp.sum(p_u, axis=0, keepdims=True)
        return p_d, p_u, l

    for qi in range(nt):
        lo = qi * t
        q = q_ref[lo:lo + t, :]
        zero = jnp.zeros_like(q)
        pd1, pu1, l1 = one_map(jnp.where(lane < HEAD_DIM, q, zero), qi)
        pd2, pu2, l2 = one_map(jnp.where(lane >= HEAD_DIM, q, zero), qi)
        c2 = lam * l1 / l2
        o = _dot(vt_scr[:, lo:lo + t], (pd1 - pd2 * c2).astype(bf16))
        if qi > 0:
            o = o + _dot(vt_scr[:, 0:lo], (pu1 - pu2 * c2).astype(bf16))
        o = o / l1
        y = o * lax.rsqrt(jnp.mean(o * o, axis=0, keepdims=True) + RMS_EPS) * gcol_ref[...]
        o_ref[qi * t:(qi + 1) * t, :] = (y * (1.0 - LAMBDA_INIT)).T.astype(bf16)


def _diff(lq1, lk1, lq2, lk2, subln_gcol, proj):
    vec = lambda n: pl.BlockSpec((1, n), lambda b, h: (0, 0))
    return pl.pallas_call(
        _diff_kernel,
        grid=(BATCH, DIFF_HEADS),
        in_specs=[vec(HEAD_DIM), vec(HEAD_DIM), vec(HEAD_DIM), vec(HEAD_DIM),
                  pl.BlockSpec((LANES, 1), lambda b, h: (0, 0)),
                  pl.BlockSpec((SEQ, LANES), lambda b, h: (b, COL_QD // LANES + h)),
                  pl.BlockSpec((SEQ, LANES), lambda b, h: (b, COL_KD // LANES + h)),
                  pl.BlockSpec((SEQ, LANES), lambda b, h: (b, COL_VD // LANES + h))],
        out_specs=pl.BlockSpec((SEQ, LANES), lambda b, h: (b, h)),
        out_shape=jax.ShapeDtypeStruct((N_TOK, DIFF_HEADS * LANES), bf16),
        scratch_shapes=[pltpu.VMEM((LANES, SEQ), bf16)],
        compiler_params=_cparams(("arbitrary", "arbitrary")),
        name="diff",
    )(lq1, lk1, lq2, lk2, subln_gcol, proj, proj, proj)


MERGE_TM = 256


def _merge_kernel(oa_ref, ob_ref, ga_ref, gb_ref, x_ref, gt_ref, sc_ref, sh_ref, g_ref,
                  wa_ref, wb_ref, wo_ref, wr_ref, br_ref, x1_ref, hp_ref, lt_ref, y_even, y_odd):
    i = pl.program_id(0)

    @pl.when(i == 0)
    def _():
        y_odd[...] = jnp.zeros_like(y_odd)

    def step(y_cur, y_prev):
        x1 = x_ref[...] + gt_ref[...] * y_prev[...]
        x1_ref[...] = x1
        ms = jnp.mean(x1 * x1, axis=-1, keepdims=True)
        h2 = (x1 * lax.rsqrt(ms + RMS_EPS) * g_ref[...]) * (1.0 + sc_ref[...]) + sh_ref[...]
        hb = lax.bitcast_convert_type(h2.astype(bf16).astype(f32), jnp.uint32)
        half = D_MODEL // 2
        hp_ref[...] = (hb[:, :half] >> 16) | (hb[:, half:] & jnp.uint32(0xFFFF0000))
        lt_ref[...] = _dot_nt(wr_ref[...].astype(bf16), h2.astype(bf16)) + br_ref[...]

        a = _dot(oa_ref[...], wa_ref[...])
        b = _dot(ob_ref[...], wb_ref[...])
        merged = (jax.nn.sigmoid(ga_ref[...].astype(f32)) * a + jax.nn.sigmoid(gb_ref[...].astype(f32)) * b)
        y_cur[...] = _dot(merged.astype(bf16), wo_ref[...])

    @pl.when(i % 2 == 0)
    def _():
        step(y_even, y_odd)

    @pl.when(i % 2 == 1)
    def _():
        step(y_odd, y_even)


def _merge(o_a, o_b, proj, x2, mod3, g_ffn, wa, wb, wo, wr_t, br_col):
    tm = MERGE_TM
    per_b = SEQ // tm
    nt = N_TOK // tm
    cur = lambda i: jnp.minimum(i, nt - 1)
    prev = lambda i: jnp.maximum(i - 1, 0)
    const = lambda shape: pl.BlockSpec(shape, lambda i: (0,) * len(shape), pipeline_mode=pl.Buffered(1))
    modspec = lambda chunk: pl.BlockSpec((None, 1, D_MODEL), lambda i: (prev(i) // per_b, 0, chunk))
    return pl.pallas_call(
        _merge_kernel,
        grid=(nt + 1,),
        in_specs=[pl.BlockSpec((tm, 1024), lambda i: (cur(i), 0)),
                  pl.BlockSpec((tm, 1024), lambda i: (cur(i), 0)),
                  pl.BlockSpec((tm, D_MODEL), lambda i: (cur(i), COL_GA // D_MODEL)),
                  pl.BlockSpec((tm, D_MODEL), lambda i: (cur(i), COL_GB // D_MODEL)),
                  pl.BlockSpec((tm, D_MODEL), lambda i: (prev(i), 0)),
                  modspec(2), modspec(4), modspec(3),
                  const((1, D_MODEL)),
                  const((1024, D_MODEL)), const((1024, D_MODEL)), const((D_MODEL, D_MODEL)),
                  const((ROUTER_ROWS, D_MODEL)), const((ROUTER_ROWS, 1))],
        out_specs=[pl.BlockSpec((tm, D_MODEL), lambda i: (prev(i), 0)),
                   pl.BlockSpec((tm, D_MODEL // 2), lambda i: (prev(i), 0)),
                   pl.BlockSpec((ROUTER_ROWS, tm), lambda i: (0, prev(i)))],
        out_shape=[jax.ShapeDtypeStruct((N_TOK, D_MODEL), f32),
                   jax.ShapeDtypeStruct((N_TOK, D_MODEL // 2), jnp.uint32),
                   jax.ShapeDtypeStruct((ROUTER_ROWS, N_TOK), f32)],
        scratch_shapes=[pltpu.VMEM((tm, D_MODEL), f32), pltpu.VMEM((tm, D_MODEL), f32)],
        compiler_params=_cparams(("arbitrary",)),
        name="merge",
    )(o_a, o_b, proj, proj, x2, mod3, mod3, mod3, g_ffn, wa, wb, wo, wr_t, br_col)


def _first_argmax(v, iota_f):
    vmax = jnp.max(v, axis=0, keepdims=True)
    idx = jnp.min(jnp.where(v == vmax, iota_f, float(v.shape[0])), axis=0, keepdims=True)
    return vmax, idx


def _route_kernel(lt_ref, ints_ref, w_ref, cnt_ref, carry):
    @pl.when(pl.program_id(0) == 0)
    def _():
        carry[...] = jnp.zeros_like(carry)

    cw = 256
    io8 = lax.broadcasted_iota(i32, (N_GROUPS, cw), 0).astype(f32)
    io64 = lax.broadcasted_iota(i32, (N_EXPERTS, cw), 0).astype(f32)
    tri = jnp.where(lax.broadcasted_iota(i32, (cw, cw), 0) < lax.broadcasted_iota(i32, (cw, cw), 1),
                    1.0, 0.0).astype(bf16)
    ints_ref[4:8, :] = jnp.zeros((4, ROUTE_T), i32)
    w_ref[2:8, :] = jnp.zeros((6, ROUTE_T), f32)
    for c in range(ROUTE_T // cw):
        sl = slice(c * cw, (c + 1) * cw)
        gl = lt_ref[0:N_GROUPS, sl]
        gmax, gsel = _first_argmax(gl, io8)
        gw = 1.0 / jnp.sum(jnp.exp(gl - gmax), axis=0, keepdims=True)
        es = jnp.zeros((EXPERTS_PER_GROUP, cw), f32)
        for g in range(N_GROUPS):
            lo = N_GROUPS + g * EXPERTS_PER_GROUP
            es = jnp.where(gsel == float(g), lt_ref[lo:lo + EXPERTS_PER_GROUP, sl], es)
        v1, i1 = _first_argmax(es, io8)
        es2 = jnp.where(io8 == i1, -jnp.inf, es)
        v2, i2 = _first_argmax(es2, io8)
        e = jnp.exp(v2 - v1)
        w_ref[0:1, sl] = gw / (1.0 + e)
        w_ref[1:2, sl] = gw * e / (1.0 + e)
        e1 = gsel * float(EXPERTS_PER_GROUP) + i1
        e2 = gsel * float(EXPERTS_PER_GROUP) + i2
        oh1 = io64 == e1
        oh2 = io64 == e2
        oc = jnp.where(oh1, 1.0, jnp.where(oh2, 1.0, 0.0))
        pre = _dot(oc.astype(bf16), tri) + carry[...]
        ints_ref[0:1, sl] = e1.astype(i32)
        ints_ref[1:2, sl] = e2.astype(i32)
        ints_ref[2:3, sl] = jnp.sum(jnp.where(oh1, pre, 0.0), axis=0, keepdims=True).astype(i32)
        ints_ref[3:4, sl] = jnp.sum(jnp.where(oh2, pre, 0.0), axis=0, keepdims=True).astype(i32)
        carry[...] = carry[...] + jnp.sum(oc, axis=1, keepdims=True)
    cnt_ref[...] = jnp.broadcast_to(carry[...], (N_EXPERTS, LANES)).astype(i32)


def _route(logits_t):
    t = ROUTE_T
    return pl.pallas_call(
        _route_kernel,
        grid=(N_TOK // t,),
        in_specs=[pl.BlockSpec((ROUTER_ROWS, t), lambda c: (0, c))],
        out_specs=[pl.BlockSpec((8, t), lambda c: (0, c)),
                   pl.BlockSpec((8, t), lambda c: (0, c)),
                   pl.BlockSpec((N_EXPERTS, LANES), lambda c: (0, 0))],
        out_shape=[jax.ShapeDtypeStruct((8, N_TOK), i32),
                   jax.ShapeDtypeStruct((8, N_TOK), f32),
                   jax.ShapeDtypeStruct((N_EXPERTS, LANES), i32)],
        scratch_shapes=[pltpu.VMEM((N_EXPERTS, 1), f32)],
        compiler_params=_cparams(("arbitrary",)),
        name="route",
    )(logits_t)


def _plan_kernel(cnt_ref, ints_ref, pos_ref, blk_u_ref, used_e_ref, meta_ref, pstart_s):
    @pl.when(pl.program_id(0) == 0)
    def _():
        def per_expert(e, carry):
            b, u = carry
            nb = (cnt_ref[e] + (MOE_TM - 1)) // MOE_TM
            pstart_s[e] = b * MOE_TM
            used_e_ref[u] = e

            def mark(k, c):
                blk_u_ref[b + k] = u
                return c

            lax.fori_loop(0, nb, mark, 0)
            return b + nb, u + jnp.minimum(nb, 1)

        nb_total, nu = lax.fori_loop(0, N_EXPERTS, per_expert, (0, 0))
        meta_ref[0] = nb_total
        meta_ref[1] = nu
        last_e = used_e_ref[nu - 1]

        def fill_e(k, c):
            used_e_ref[k] = last_e
            return c

        lax.fori_loop(nu, N_EXPERTS, fill_e, 0)

        def fill_b(k, c):
            blk_u_ref[k] = nu - 1
            return c

        lax.fori_loop(nb_total, MOE_NBLK, fill_b, 0)

    t = ROUTE_T
    iocol = lax.broadcasted_iota(i32, (N_EXPERTS, 1), 0)

    def build(e, col):
        return jnp.where(iocol == e, pstart_s[e], col)

    pcol = lax.fori_loop(0, N_EXPERTS, build, jnp.zeros((N_EXPERTS, 1), i32)).astype(f32)
    io64 = lax.broadcasted_iota(i32, (N_EXPERTS, t), 0)
    e1 = ints_ref[0:1, :]
    e2 = ints_ref[1:2, :]
    pos_ref[0:1, :] = (jnp.sum(jnp.where(io64 == e1, pcol, 0.0), axis=0, keepdims=True).astype(i32)
                       + ints_ref[2:3, :])
    pos_ref[1:2, :] = (jnp.sum(jnp.where(io64 == e2, pcol, 0.0), axis=0, keepdims=True).astype(i32)
                       + ints_ref[3:4, :])
    pos_ref[2:8, :] = jnp.zeros((6, t), i32)


def _plan(counts, ints):
    t = ROUTE_T
    return pl.pallas_call(
        _plan_kernel,
        grid_spec=pltpu.PrefetchScalarGridSpec(
            num_scalar_prefetch=1,
            grid=(N_TOK // t,),
            in_specs=[pl.BlockSpec((8, t), lambda c, cnt: (0, c))],
            out_specs=[pl.BlockSpec((8, t), lambda c, cnt: (0, c)),
                       pl.BlockSpec(memory_space=pltpu.SMEM),
                       pl.BlockSpec(memory_space=pltpu.SMEM),
                       pl.BlockSpec(memory_space=pltpu.SMEM),
                       pl.BlockSpec(memory_space=pltpu.SMEM)]),
        out_shape=[jax.ShapeDtypeStruct((8, N_TOK), i32),
                   jax.ShapeDtypeStruct((MOE_NBLK,), i32),
                   jax.ShapeDtypeStruct((N_EXPERTS,), i32),
                   jax.ShapeDtypeStruct((2,), i32),
                   jax.ShapeDtypeStruct((N_EXPERTS,), i32)],
        compiler_params=_cparams(("arbitrary",)),
        name="plan",
    )(counts, ints)


DISP_CHUNK = 512
DMA_UNROLL = 8


PAD_ZERO_ROWS = 128


def _dispatch_kernel(pos_ref, cnt_ref, pstart_ref, hp_ref, xs_ref, zbuf, sem, zsem):
    base = pl.program_id(0) * DISP_CHUNK

    def zero_pads(wait):
        def go(copy, cond):
            @pl.when(cond)
            def _():
                copy.wait() if wait else copy.start()

        def per_expert(e, carry):
            n = cnt_ref[e]
            first = pstart_ref[e] + n
            pad = (-n) & (MOE_TM - 1)
            lead = (-first) & 7
            for k in range(7):
                go(pltpu.make_async_copy(zbuf.at[pl.ds(0, 1), :], xs_ref.at[pl.ds(first + k, 1), :], zsem),
                   (k < lead) & (k < pad))
            off = first + lead
            rest = pad - lead
            size = PAD_ZERO_ROWS
            while size >= 8:
                go(pltpu.make_async_copy(zbuf.at[pl.ds(0, size), :],
                                         xs_ref.at[pl.ds(pl.multiple_of(off, 8), size), :], zsem),
                   (rest & size) != 0)
                off = off + (rest & size)
                size //= 2
            return carry

        lax.fori_loop(0, N_EXPERTS, per_expert, 0)

    @pl.when(pl.program_id(0) == 0)
    def _():
        zbuf[...] = jnp.zeros_like(zbuf)
        zero_pads(wait=False)
        zero_pads(wait=True)

    def row_copy(r, k):
        dst = pos_ref[k * N_TOK + base + r]
        return pltpu.make_async_copy(hp_ref.at[pl.ds(r, 1), :], xs_ref.at[pl.ds(dst, 1), :], sem)

    def issue(g, carry):
        for u in range(DMA_UNROLL):
            r = g * DMA_UNROLL + u
            row_copy(r, 0).start()
            row_copy(r, 1).start()
        return carry

    lax.fori_loop(0, DISP_CHUNK // DMA_UNROLL, issue, 0)

    def drain(g, carry):
        for _ in range(DMA_UNROLL):
            row_copy(0, 0).wait()
            row_copy(0, 0).wait()
        return carry

    lax.fori_loop(0, DISP_CHUNK // DMA_UNROLL, drain, 0)


def _dispatch(pos_flat, counts, pstart, hp):
    return pl.pallas_call(
        _dispatch_kernel,
        grid_spec=pltpu.PrefetchScalarGridSpec(
            num_scalar_prefetch=3,
            grid=(N_TOK // DISP_CHUNK,),
            in_specs=[pl.BlockSpec((DISP_CHUNK, D_MODEL // 2), lambda i, pos, cnt, ps: (i, 0))],
            out_specs=pl.BlockSpec(memory_space=pl.ANY),
            scratch_shapes=[pltpu.VMEM((PAD_ZERO_ROWS, D_MODEL // 2), jnp.uint32),
                            pltpu.SemaphoreType.DMA(()),
                            pltpu.SemaphoreType.DMA(())]),
        out_shape=jax.ShapeDtypeStruct((MOE_ROWS, D_MODEL // 2), jnp.uint32),
        compiler_params=_cparams(("arbitrary",)),
        name="dispatch",
    )(pos_flat, counts, pstart, hp)


MOE_KC = 512
N_GU_CH = D_MODEL // MOE_KC
N_DN_CH = D_EXPERT // MOE_KC
N_CH = N_GU_CH + N_DN_CH


def _moe_kernel(blk_u, used_e, meta, xs_ref, wgu_hbm, wdn_hbm, y_ref, ring, wgu_b, wdn_b, sems):
    i = pl.program_id(0)
    u = blk_u[i]
    first = (i == 0) | (u != blk_u[jnp.maximum(i - 1, 0)])
    active = i < meta[0]

    def chunk_copy(e, c):
        if c < N_GU_CH:
            src = wgu_hbm.at[e, pl.ds(c * MOE_KC, MOE_KC), :]
        else:
            src = wdn_hbm.at[e, pl.ds((c - N_GU_CH) * MOE_KC, MOE_KC), :]
        return pltpu.make_async_copy(src, ring.at[c], sems.at[c])

    @pl.when(i == 0)
    def _():
        for c in range(N_CH):
            chunk_copy(used_e[0], c).start()

    def unpack_x():
        packed = xs_ref[...]
        x_lo = lax.bitcast_convert_type(packed << 16, f32).astype(bf16)
        x_hi = lax.bitcast_convert_type(packed & jnp.uint32(0xFFFF0000), f32).astype(bf16)
        return x_lo, x_hi

    def swiglu(gu):
        g = gu[:, :D_EXPERT]
        return (g * jax.nn.sigmoid(g) * gu[:, D_EXPERT:]).astype(bf16)

    def rows(c):
        return slice(c * MOE_KC, (c + 1) * MOE_KC)

    @pl.when(active & first)
    def _():
        e_next = used_e[jnp.minimum(u + 1, meta[1] - 1)]

        def take(c):
            chunk_copy(e_next, c).wait()
            wb = ring[c].astype(bf16)
            if c < N_GU_CH:
                wgu_b[rows(c), :] = wb
            else:
                wdn_b[rows(c - N_GU_CH), :] = wb
            chunk_copy(e_next, c).start()

        x_lo, x_hi = unpack_x()
        per_half = N_GU_CH // 2
        x_chunks = [xh[:, rows(c)] for xh in (x_lo, x_hi) for c in range(per_half)]
        take(0)
        gu = None
        for c in range(N_GU_CH):
            take(c + 1)
            d = _dot(x_chunks[c], wgu_b[rows(c), :])
            gu = d if gu is None else gu + d
        a = swiglu(gu)
        y = None
        for c in range(N_DN_CH):
            if c + 1 < N_DN_CH:
                take(N_GU_CH + c + 1)
            d = _dot(a[:, rows(c)], wdn_b[rows(c), :])
            y = d if y is None else y + d
        y_ref[...] = y

    @pl.when(active & jnp.logical_not(first))
    def _():
        x_lo, x_hi = unpack_x()
        half = D_MODEL // 2
        gu = _dot(x_lo, wgu_b[0:half, :]) + _dot(x_hi, wgu_b[half:, :])
        y_ref[...] = _dot(swiglu(gu), wdn_b[...])

    @pl.when(i == pl.num_programs(0) - 1)
    def _():
        for c in range(N_CH):
            chunk_copy(used_e[0], c).wait()


def _moe(blk_u, used_e, meta, xs, w_gu, w_dn):
    tm = MOE_TM
    blk = lambda i, bu, ue, mt: (jnp.minimum(i, mt[0] - 1), 0)
    return pl.pallas_call(
        _moe_kernel,
        grid_spec=pltpu.PrefetchScalarGridSpec(
            num_scalar_prefetch=3,
            grid=(MOE_NBLK,),
            in_specs=[pl.BlockSpec((tm, D_MODEL // 2), blk),
                      pl.BlockSpec(memory_space=pl.ANY),
                      pl.BlockSpec(memory_space=pl.ANY)],
            out_specs=pl.BlockSpec((tm, D_MODEL), blk),
            scratch_shapes=[pltpu.VMEM((N_CH, MOE_KC, D_MODEL), f32),
                            pltpu.VMEM((D_MODEL, 2 * D_EXPERT), bf16),
                            pltpu.VMEM((D_EXPERT, D_MODEL), bf16),
                            pltpu.SemaphoreType.DMA((N_CH,))]),
        out_shape=jax.ShapeDtypeStruct((MOE_ROWS, D_MODEL), f32),
        compiler_params=_cparams(("arbitrary",)),
        name="moe",
    )(blk_u, used_e, meta, xs, w_gu, w_dn)


COMB_TM = 256


def _combine_kernel(pos_ref, x1_ref, gt_ref, wc_ref, g_ref, y_ref, o_ref, ybuf_even, ybuf_odd, sems):
    i = pl.program_id(0)
    last = pl.num_programs(0) - 1
    nxt = jnp.minimum(i + 1, last)
    bufs = (ybuf_even, ybuf_odd)

    def row_copy(tile, r, k, s):
        src = pos_ref[k * N_TOK + tile * COMB_TM + r]
        return pltpu.make_async_copy(y_ref.at[pl.ds(src, 1), :], bufs[s].at[k, pl.ds(r, 1), :], sems.at[s])

    def drain(s):
        def body(g, carry):
            for _ in range(DMA_UNROLL):
                row_copy(0, 0, 0, s).wait()
                row_copy(0, 0, 0, s).wait()
            return carry
        lax.fori_loop(0, COMB_TM // DMA_UNROLL, body, 0)

    @pl.when(i == 0)
    def _():
        def body(g, carry):
            for u in range(DMA_UNROLL):
                r = g * DMA_UNROLL + u
                row_copy(0, r, 0, 0).start()
                row_copy(0, r, 1, 0).start()
            return carry
        lax.fori_loop(0, COMB_TM // DMA_UNROLL, body, 0)

    def step(s):
        drain(s)
        for r in range(COMB_TM):
            row_copy(nxt, r, 0, 1 - s).start()
            row_copy(nxt, r, 1, 1 - s).start()
        ffn = wc_ref[:, 0:1] * bufs[s][0] + wc_ref[:, 1:2] * bufs[s][1]
        xo = x1_ref[...] + gt_ref[...] * ffn
        ms = jnp.mean(xo * xo, axis=-1, keepdims=True)
        o_ref[...] = xo * lax.rsqrt(ms + RMS_EPS) * g_ref[...]

        @pl.when(i == last)
        def _():
            drain(1 - s)

    @pl.when(i % 2 == 0)
    def _():
        step(0)

    @pl.when(i % 2 == 1)
    def _():
        step(1)


def _combine(pos_flat, x1, mod3, w_col, g_final, y):
    tm = COMB_TM
    per_b = SEQ // tm
    return pl.pallas_call(
        _combine_kernel,
        grid_spec=pltpu.PrefetchScalarGridSpec(
            num_scalar_prefetch=1,
            grid=(N_TOK // tm,),
            in_specs=[pl.BlockSpec((tm, D_MODEL), lambda i, pos: (i, 0)),
                      pl.BlockSpec((None, 1, D_MODEL), lambda i, pos: (i // per_b, 0, 5)),
                      pl.BlockSpec((tm, 8), lambda i, pos: (i, 0)),
                      pl.BlockSpec((1, D_MODEL), lambda i, pos: (0, 0)),
                      pl.BlockSpec(memory_space=pl.ANY)],
            out_specs=pl.BlockSpec((tm, D_MODEL), lambda i, pos: (i, 0)),
            scratch_shapes=[pltpu.VMEM((TOP_K, tm, D_MODEL), f32),
                            pltpu.VMEM((TOP_K, tm, D_MODEL), f32),
                            pltpu.SemaphoreType.DMA((2,))]),
        out_shape=jax.ShapeDtypeStruct((N_TOK, D_MODEL), f32),
        compiler_params=_cparams(("arbitrary",)),
        name="combine",
    )(pos_flat, x1, mod3, w_col, g_final, y)


def kernel(x, c, positions, w_ada, b_ada, g_mix, w_in, swa_sinks, diff_lambda_q1, diff_lambda_k1, diff_lambda_q2, diff_lambda_k2, diff_subln_g, w_branch_a, w_branch_b, w_out, g_ffn, w_router_group, b_router_group, w_router_expert, b_router_expert, w_exp_gate_up, w_exp_down, g_final):
    layer = 0
    x2 = x.reshape(N_TOK, D_MODEL)

    mod = _ada(c, w_ada[layer], b_ada[layer])
    mod3 = mod.reshape(BATCH, 1, 6 * D_MODEL)

    w_bf = w_in[layer].astype(bf16)
    s_qa = SWA_Q_HEADS * HEAD_DIM
    kv_heads = [w_bf[:, s_qa + h * HEAD_DIM:s_qa + (h + 1) * HEAD_DIM] for h in range(2 * SWA_KV_HEADS)]
    w_kv = jnp.concatenate([h for h in kv_heads for _ in range(2)], axis=1)
    half = HEAD_DIM // 2
    inv_freq = ROPE_THETA ** (-jnp.arange(0, HEAD_DIM, 2, dtype=f32) / HEAD_DIM)
    invf = jnp.tile(inv_freq, LANES // half).reshape(1, LANES)
    pos4 = jnp.repeat(positions.reshape(N_TOK // 4, 4).astype(i32), half, axis=1)
    cos4, sin4 = _rope_tables(pos4, invf)
    cos_t = jnp.tile(cos4.reshape(N_TOK, half), (1, LANES // half))
    sin_t = jnp.tile(sin4.reshape(N_TOK, half), (1, LANES // half))

    proj = _proj(x2, mod3, g_mix[layer].reshape(1, D_MODEL), cos_t, sin_t, w_bf, w_kv)
    o_a = _swa(swa_sinks[layer], proj)
    o_b = _diff(diff_lambda_q1[layer].reshape(1, HEAD_DIM), diff_lambda_k1[layer].reshape(1, HEAD_DIM),
                diff_lambda_q2[layer].reshape(1, HEAD_DIM), diff_lambda_k2[layer].reshape(1, HEAD_DIM),
                diff_subln_g[layer].reshape(LANES, 1), proj)

    wr = jnp.concatenate([w_router_group[layer], w_router_expert[layer]], axis=1)
    wr_t = jnp.pad(wr.T, ((0, ROUTER_ROWS - wr.shape[1]), (0, 0)))
    br = jnp.concatenate([b_router_group[layer], b_router_expert[layer]])
    br_col = jnp.pad(br, (0, ROUTER_ROWS - br.shape[0])).reshape(ROUTER_ROWS, 1)
    x1, hp, logits_t = _merge(o_a, o_b, proj, x2, mod3, g_ffn[layer].reshape(1, D_MODEL),
                              w_branch_a[layer].astype(bf16), w_branch_b[layer].astype(bf16),
                              w_out[layer].astype(bf16), wr_t, br_col)

    ints, wts, counts = _route(logits_t)
    cnt = counts[:, 0]
    pos, blk_u, used_e, meta, pstart = _plan(cnt, ints)
    pos_flat = pos[0:TOP_K].reshape(TOP_K * N_TOK)
    xs = _dispatch(pos_flat, cnt, pstart, hp)
    y = _moe(blk_u, used_e, meta, xs, w_exp_gate_up[layer], w_exp_down[layer])
    out = _combine(pos_flat, x1, mod3, wts.T, g_final.reshape(1, D_MODEL), y)
    return out.reshape(BATCH, SEQ, D_MODEL)
```

```python
import math

import jax
import jax.numpy as jnp
from jax import lax
from jax.experimental import pallas as pl
from jax.experimental.pallas import tpu as pltpu

D_MODEL = 2048
BATCH = 8
SEQ = 2048
N_TOK = BATCH * SEQ
HEAD_DIM = 64
ROPE_THETA = 10000.0
RMS_EPS = 1e-6
NEG_INF = -1e30
SWA_Q_HEADS = 16
SWA_KV_HEADS = 2
SWA_GROUP = SWA_Q_HEADS // SWA_KV_HEADS
WINDOW = 128
DIFF_HEADS = 8
N_GROUPS = 8
EXPERTS_PER_GROUP = 8
N_EXPERTS = N_GROUPS * EXPERTS_PER_GROUP
TOP_K = 2
D_EXPERT = D_MODEL // 2
LAMBDA_INIT = 0.8 - 0.6 * math.exp(-0.3 * 0)
LOG2_E = math.log2(math.e)

LANES = 128
VMEM_LIMIT = 56 * 1024 * 1024

PROJ_TN = 1024
PROJ_Q = 256
PROJ_MAIN_TILES = 8
PROJ_MAIN_COLS = PROJ_MAIN_TILES * PROJ_TN
PROJ_KV_COLS = 512
COL_QA, COL_QD, COL_KD, COL_VD, COL_GA, COL_GB = 0, 1024, 2048, 3072, 4096, 6144

MOE_TM = 256
MOE_ROWS = N_TOK * TOP_K + N_EXPERTS * MOE_TM
MOE_NBLK = MOE_ROWS // MOE_TM
ROUTE_T = 2048
ROUTER_ROWS = 128

f32 = jnp.float32
bf16 = jnp.bfloat16
i32 = jnp.int32


def _cparams(sem, vmem=VMEM_LIMIT):
    return pltpu.CompilerParams(dimension_semantics=sem, vmem_limit_bytes=vmem)


def _dot(a, b):
    return jnp.dot(a, b, preferred_element_type=f32)


def _dot_nt(a, b):
    return lax.dot_general(a, b, (((1,), (1,)), ((), ())), preferred_element_type=f32)


def _ada_kernel(c_ref, w_ref, b_ref, o_ref):
    c = c_ref[...]
    ca = (c * jax.nn.sigmoid(c)).astype(bf16)
    o_ref[...] = _dot(ca, w_ref[...].astype(bf16)) + b_ref[...]


def _ada(c, w_ada, b_ada):
    tn = 512
    n = w_ada.shape[1]
    return pl.pallas_call(
        _ada_kernel,
        grid=(n // tn,),
        in_specs=[pl.BlockSpec((BATCH, D_MODEL), lambda j: (0, 0)),
                  pl.BlockSpec((D_MODEL, tn), lambda j: (0, j)),
                  pl.BlockSpec((1, tn), lambda j: (0, j))],
        out_specs=pl.BlockSpec((BATCH, tn), lambda j: (0, j)),
        out_shape=jax.ShapeDtypeStruct((BATCH, n), f32),
        compiler_params=_cparams(("arbitrary",)),
        name="ada",
    )(c, w_ada, b_ada.reshape(1, n))


PROJ_TM = 1024


def _rope128(xc, cos, sin_signed, first_half):
    r_prev = pltpu.roll(xc, 32, 1)
    r_next = pltpu.roll(xc, 96, 1)
    return xc * cos + jnp.where(first_half, r_next, r_prev) * sin_signed


def _rope_table_kernel(pos_ref, invf_ref, cos_ref, sin_ref):
    ang = pos_ref[...].astype(f32) * invf_ref[...]
    cos_ref[...] = jnp.cos(ang)
    sin_ref[...] = jnp.sin(ang)


def _rope_tables(pos4, invf):
    rows = pos4.shape[0]
    tr = 512
    spec = pl.BlockSpec((tr, LANES), lambda i: (i, 0))
    return pl.pallas_call(
        _rope_table_kernel,
        grid=(rows // tr,),
        in_specs=[spec, pl.BlockSpec((1, LANES), lambda i: (0, 0))],
        out_specs=[spec, spec],
        out_shape=[jax.ShapeDtypeStruct((rows, LANES), f32)] * 2,
        compiler_params=_cparams(("arbitrary",)),
        name="rope_tables",
    )(pos4, invf)


def _proj_kernel(x_ref, sc_ref, sh_ref, g_ref, cos_ref, sin_ref, w0_ref, w1_ref, w2_ref, w3_ref, wkv_ref,
                 o_ref, okv_ref, h_scr, sin_scr):
    j = pl.program_id(1)
    main = (w0_ref, w1_ref, w2_ref, w3_ref)
    lane = lax.broadcasted_iota(i32, (1, LANES), 1)
    first_half = (lane % HEAD_DIM) < (HEAD_DIM // 2)

    @pl.when(j == 0)
    def _():
        rc = 256
        for r in range(PROJ_TM // rc):
            x = x_ref[r * rc:(r + 1) * rc, :]
            ms = jnp.mean(x * x, axis=-1, keepdims=True)
            y = x * lax.rsqrt(ms + RMS_EPS) * g_ref[...]
            h = y * (1.0 + sc_ref[...]) + sh_ref[...]
            h_scr[r * rc:(r + 1) * rc, :] = h.astype(bf16)
        sin_scr[...] = sin_ref[...] * jnp.where(first_half, -1.0, 1.0)

    qw = PROJ_Q

    def rope_q(out, acc, off, scale):
        for c in range(qw // LANES):
            xc = acc[:, c * LANES:(c + 1) * LANES]
            out[:, off + c * LANES:off + (c + 1) * LANES] = (
                _rope128(xc, cos_ref[...], sin_scr[...], first_half) * scale).astype(bf16)

    def cast_q(out, acc, off, scale):
        out[:, off:off + qw] = acc.astype(bf16)

    def tile(out, quarters, epilogues, scale=1.0):
        for s, (w, epilogue) in enumerate(zip(quarters, epilogues)):
            epilogue(out, _dot(h_scr[...], w[...]), s * qw, scale)

    @pl.when(j == 0)
    def _():
        tile(o_ref, main, (rope_q,) * 4, 1.0 / math.sqrt(HEAD_DIM))

    @pl.when(j == 1)
    def _():
        tile(o_ref, main, (rope_q,) * 4, LOG2_E / math.sqrt(HEAD_DIM))

    @pl.when(j == 2)
    def _():
        tile(o_ref, main, (rope_q,) * 4)

    @pl.when((j >= 3) & (j < PROJ_MAIN_TILES))
    def _():
        tile(o_ref, main, (cast_q,) * 4)

    @pl.when(j == PROJ_MAIN_TILES)
    def _():
        tile(okv_ref, (wkv_ref.at[:, 0:qw], wkv_ref.at[:, qw:2 * qw]), (rope_q, cast_q))


def _proj(x2, mod3, g_mix, cos_t, sin_t, w_bf, w_kv):
    tm = PROJ_TM
    per_b = SEQ // tm
    per_tile = PROJ_TN // PROJ_Q
    kv_blocks = 2 * SWA_KV_HEADS * HEAD_DIM // PROJ_Q

    def wcol(quarter):
        def index(i, j):
            jj = jnp.minimum(j, PROJ_MAIN_TILES - 1)
            return 0, per_tile * jj + quarter + jnp.where(jj >= SWA_Q_HEADS * HEAD_DIM // PROJ_TN, kv_blocks, 0)
        return index

    main_blk = lambda i, j: (i, jnp.minimum(j, PROJ_MAIN_TILES - 1))
    return pl.pallas_call(
        _proj_kernel,
        grid=(N_TOK // tm, PROJ_MAIN_TILES + 1),
        in_specs=[pl.BlockSpec((tm, D_MODEL), lambda i, j: (i, 0)),
                  pl.BlockSpec((None, 1, D_MODEL), lambda i, j: (i // per_b, 0, 1)),
                  pl.BlockSpec((None, 1, D_MODEL), lambda i, j: (i // per_b, 0, 0)),
                  pl.BlockSpec((1, D_MODEL), lambda i, j: (0, 0)),
                  pl.BlockSpec((tm, LANES), lambda i, j: (i, 0)),
                  pl.BlockSpec((tm, LANES), lambda i, j: (i, 0))]
                 + [pl.BlockSpec((D_MODEL, PROJ_Q), wcol(q)) for q in range(per_tile)]
                 + [pl.BlockSpec((D_MODEL, PROJ_KV_COLS), lambda i, j: (0, 0))],
        out_specs=[pl.BlockSpec((tm, PROJ_TN), main_blk),
                   pl.BlockSpec((tm, PROJ_KV_COLS), lambda i, j: (i, 0))],
        out_shape=[jax.ShapeDtypeStruct((N_TOK, PROJ_MAIN_COLS), bf16),
                   jax.ShapeDtypeStruct((N_TOK, PROJ_KV_COLS), bf16)],
        scratch_shapes=[pltpu.VMEM((tm, D_MODEL), bf16),
                        pltpu.VMEM((tm, LANES), f32)],
        compiler_params=_cparams(("arbitrary", "arbitrary")),
        name="proj",
    )(x2, mod3, mod3, g_mix, cos_t, sin_t, w_bf, w_bf, w_bf, w_bf, w_kv)


def _swa_kernel(sink_ref, q_ref, kv_ref, kvp_ref, o_ref):
    n = pl.program_id(1)
    w = WINDOW
    qi = lax.broadcasted_iota(i32, (w, 2 * w), 0)
    kj = lax.broadcasted_iota(i32, (w, 2 * w), 1)
    rel = qi + w - kj
    first_key = jnp.where(n > 0, 0, w)
    valid = (rel >= 0) & (rel < w) & (kj >= first_key)
    lane = lax.broadcasted_iota(i32, (w, LANES), 1)
    low = lane < HEAD_DIM
    pairs = SWA_GROUP // 2
    for kvh in range(SWA_KV_HEADS):
        ks = slice(kvh * LANES, (kvh + 1) * LANES)
        vs = slice((SWA_KV_HEADS + kvh) * LANES, (SWA_KV_HEADS + kvh + 1) * LANES)
        kk = jnp.concatenate([kvp_ref[:, ks], kv_ref[:, ks]], axis=0)
        vv = jnp.concatenate([kvp_ref[:, vs], kv_ref[:, vs]], axis=0)
        parts = []
        for pair in range(pairs):
            gp = kvh * pairs + pair
            q2 = q_ref[:, gp * LANES:(gp + 1) * LANES]
            zero = jnp.zeros_like(q2)
            parts += [jnp.where(low, q2, zero), jnp.where(low, zero, q2)]
        s_all = _dot_nt(jnp.concatenate(parts, axis=0), kk)
        ps, dens = [], []
        for g in range(SWA_GROUP):
            s = jnp.where(valid, s_all[g * w:(g + 1) * w, :], NEG_INF)
            sink = sink_ref[kvh * SWA_GROUP + g]
            m = jnp.maximum(jnp.max(s, axis=-1, keepdims=True), sink)
            p = jnp.exp(s - m)
            dens.append(jnp.sum(p, axis=-1, keepdims=True) + jnp.exp(sink - m))
            ps.append(p.astype(bf16))
        o_all = _dot(jnp.concatenate(ps, axis=0), vv)
        for pair in range(pairs):
            gp = kvh * pairs + pair
            o_even = o_all[(2 * pair) * w:(2 * pair + 1) * w, :] / dens[2 * pair]
            o_odd = o_all[(2 * pair + 1) * w:(2 * pair + 2) * w, :] / dens[2 * pair + 1]
            o_ref[:, gp * LANES:(gp + 1) * LANES] = jnp.where(low, o_even, o_odd).astype(bf16)


def _swa(sinks, proj, proj_kv):
    nb = SEQ // WINDOW
    return pl.pallas_call(
        _swa_kernel,
        grid=(BATCH, nb),
        in_specs=[pl.BlockSpec(memory_space=pltpu.SMEM),
                  pl.BlockSpec((WINDOW, SWA_Q_HEADS * HEAD_DIM), lambda b, n: (b * nb + n, 0)),
                  pl.BlockSpec((WINDOW, PROJ_KV_COLS), lambda b, n: (b * nb + n, 0)),
                  pl.BlockSpec((WINDOW, PROJ_KV_COLS), lambda b, n: (jnp.maximum(b * nb + n - 1, 0), 0))],
        out_specs=pl.BlockSpec((WINDOW, SWA_Q_HEADS * HEAD_DIM), lambda b, n: (b * nb + n, 0)),
        out_shape=jax.ShapeDtypeStruct((N_TOK, SWA_Q_HEADS * HEAD_DIM), bf16),
        compiler_params=_cparams(("arbitrary", "arbitrary")),
        name="swa",
    )(sinks, proj, proj_kv, proj_kv)


DIFF_T = 256


def _diff_kernel(lq1_ref, lk1_ref, lq2_ref, lk2_ref, gcol_ref, q_ref, k_ref, v_ref, o_ref, vt_scr):
    t = DIFF_T
    nt = SEQ // t
    for j in range(nt):
        vt_scr[:, j * t:(j + 1) * t] = v_ref[j * t:(j + 1) * t, :].T
    lam = (jnp.exp(jnp.sum(lq1_ref[...] * lk1_ref[...], axis=-1, keepdims=True))
           - jnp.exp(jnp.sum(lq2_ref[...] * lk2_ref[...], axis=-1, keepdims=True)) + LAMBDA_INIT)
    lane = lax.broadcasted_iota(i32, (t, LANES), 1)
    causal = lax.broadcasted_iota(i32, (t, t), 0) <= lax.broadcasted_iota(i32, (t, t), 1)

    def one_map(qz, qi):
        lo = qi * t
        s_d = jnp.where(causal, _dot_nt(k_ref[lo:lo + t, :], qz), NEG_INF)
        m = jnp.max(s_d, axis=0, keepdims=True)
        if qi > 0:
            s_u = _dot_nt(k_ref[0:lo, :], qz)
            m = jnp.maximum(m, jnp.max(s_u, axis=0, keepdims=True))
        p_d = jnp.exp2(s_d - m)
        l = jnp.sum(p_d, axis=0, keepdims=True)
        p_u = None
        if qi > 0:
            p_u = jnp.exp2(s_u - m)
            l = l + jnp.sum(p_u, axis=0, keepdims=True)
        return p_d, p_u, l

    for qi in range(nt):
        lo = qi * t
        q = q_ref[lo:lo + t, :]
        zero = jnp.zeros_like(q)
        pd1, pu1, l1 = one_map(jnp.where(lane < HEAD_DIM, q, zero), qi)
        pd2, pu2, l2 = one_map(jnp.where(lane >= HEAD_DIM, q, zero), qi)
        c2 = lam * l1 / l2
        o = _dot(vt_scr[:, lo:lo + t], (pd1 - pd2 * c2).astype(bf16))
        if qi > 0:
            o = o + _dot(vt_scr[:, 0:lo], (pu1 - pu2 * c2).astype(bf16))
        o = o / l1
        y = o * lax.rsqrt(jnp.mean(o * o, axis=0, keepdims=True) + RMS_EPS) * gcol_ref[...]
        o_ref[qi * t:(qi + 1) * t, :] = (y * (1.0 - LAMBDA_INIT)).T.astype(bf16)


def _diff(lq1, lk1, lq2, lk2, subln_gcol, proj):
    vec = lambda n: pl.BlockSpec((1, n), lambda b, h: (0, 0))
    return pl.pallas_call(
        _diff_kernel,
        grid=(BATCH, DIFF_HEADS),
        in_specs=[vec(HEAD_DIM), vec(HEAD_DIM), vec(HEAD_DIM), vec(HEAD_DIM),
                  pl.BlockSpec((LANES, 1), lambda b, h: (0, 0)),
                  pl.BlockSpec((SEQ, LANES), lambda b, h: (b, COL_QD // LANES + h)),
                  pl.BlockSpec((SEQ, LANES), lambda b, h: (b, COL_KD // LANES + h)),
                  pl.BlockSpec((SEQ, LANES), lambda b, h: (b, COL_VD // LANES + h))],
        out_specs=pl.BlockSpec((SEQ, LANES), lambda b, h: (b, h)),
        out_shape=jax.ShapeDtypeStruct((N_TOK, DIFF_HEADS * LANES), bf16),
        scratch_shapes=[pltpu.VMEM((LANES, SEQ), bf16)],
        compiler_params=_cparams(("arbitrary", "arbitrary")),
        name="diff",
    )(lq1, lk1, lq2, lk2, subln_gcol, proj, proj, proj)


MERGE_TM = 256


def _merge_kernel(oa_ref, ob_ref, ga_ref, gb_ref, x_ref, gt_ref, sc_ref, sh_ref, g_ref,
                  wa_ref, wb_ref, wo_ref, wr_ref, br_ref, x1_ref, hp_ref, lt_ref, y_even, y_odd):
    i = pl.program_id(0)

    @pl.when(i == 0)
    def _():
        y_odd[...] = jnp.zeros_like(y_odd)

    def step(y_cur, y_prev):
        x1 = x_ref[...] + gt_ref[...] * y_prev[...]
        x1_ref[...] = x1
        ms = jnp.mean(x1 * x1, axis=-1, keepdims=True)
        h2 = (x1 * lax.rsqrt(ms + RMS_EPS) * g_ref[...]) * (1.0 + sc_ref[...]) + sh_ref[...]
        hb = lax.bitcast_convert_type(h2.astype(bf16).astype(f32), jnp.uint32)
        half = D_MODEL // 2
        hp_ref[...] = (hb[:, :half] >> 16) | (hb[:, half:] & jnp.uint32(0xFFFF0000))
        lt_ref[...] = _dot_nt(wr_ref[...].astype(bf16), h2.astype(bf16)) + br_ref[...]

        a = _dot(oa_ref[...], wa_ref[...])
        b = _dot(ob_ref[...], wb_ref[...])
        merged = (jax.nn.sigmoid(ga_ref[...].astype(f32)) * a + jax.nn.sigmoid(gb_ref[...].astype(f32)) * b)
        y_cur[...] = _dot(merged.astype(bf16), wo_ref[...])

    @pl.when(i % 2 == 0)
    def _():
        step(y_even, y_odd)

    @pl.when(i % 2 == 1)
    def _():
        step(y_odd, y_even)


def _merge(o_a, o_b, proj, x2, mod3, g_ffn, wa, wb, wo, wr_t, br_col):
    tm = MERGE_TM
    per_b = SEQ // tm
    nt = N_TOK // tm
    cur = lambda i: jnp.minimum(i, nt - 1)
    prev = lambda i: jnp.maximum(i - 1, 0)
    const = lambda shape: pl.BlockSpec(shape, lambda i: (0,) * len(shape), pipeline_mode=pl.Buffered(1))
    modspec = lambda chunk: pl.BlockSpec((None, 1, D_MODEL), lambda i: (prev(i) // per_b, 0, chunk))
    return pl.pallas_call(
        _merge_kernel,
        grid=(nt + 1,),
        in_specs=[pl.BlockSpec((tm, 1024), lambda i: (cur(i), 0)),
                  pl.BlockSpec((tm, 1024), lambda i: (cur(i), 0)),
                  pl.BlockSpec((tm, D_MODEL), lambda i: (cur(i), COL_GA // D_MODEL)),
                  pl.BlockSpec((tm, D_MODEL), lambda i: (cur(i), COL_GB // D_MODEL)),
                  pl.BlockSpec((tm, D_MODEL), lambda i: (prev(i), 0)),
                  modspec(2), modspec(4), modspec(3),
                  const((1, D_MODEL)),
                  const((1024, D_MODEL)), const((1024, D_MODEL)), const((D_MODEL, D_MODEL)),
                  const((ROUTER_ROWS, D_MODEL)), const((ROUTER_ROWS, 1))],
        out_specs=[pl.BlockSpec((tm, D_MODEL), lambda i: (prev(i), 0)),
                   pl.BlockSpec((tm, D_MODEL // 2), lambda i: (prev(i), 0)),
                   pl.BlockSpec((ROUTER_ROWS, tm), lambda i: (0, prev(i)))],
        out_shape=[jax.ShapeDtypeStruct((N_TOK, D_MODEL), f32),
                   jax.ShapeDtypeStruct((N_TOK, D_MODEL // 2), jnp.uint32),
                   jax.ShapeDtypeStruct((ROUTER_ROWS, N_TOK), f32)],
        scratch_shapes=[pltpu.VMEM((tm, D_MODEL), f32), pltpu.VMEM((tm, D_MODEL), f32)],
        compiler_params=_cparams(("arbitrary",)),
        name="merge",
    )(o_a, o_b, proj, proj, x2, mod3, mod3, mod3, g_ffn, wa, wb, wo, wr_t, br_col)


def _first_argmax(v, iota_f):
    vmax = jnp.max(v, axis=0, keepdims=True)
    idx = jnp.min(jnp.where(v == vmax, iota_f, float(v.shape[0])), axis=0, keepdims=True)
    return vmax, idx


def _route_kernel(lt_ref, ints_ref, w_ref, cnt_ref, carry):
    @pl.when(pl.program_id(0) == 0)
    def _():
        carry[...] = jnp.zeros_like(carry)

    cw = 256
    io8 = lax.broadcasted_iota(i32, (N_GROUPS, cw), 0).astype(f32)
    io64 = lax.broadcasted_iota(i32, (N_EXPERTS, cw), 0).astype(f32)
    tri = jnp.where(lax.broadcasted_iota(i32, (cw, cw), 0) < lax.broadcasted_iota(i32, (cw, cw), 1),
                    1.0, 0.0).astype(bf16)
    ints_ref[4:8, :] = jnp.zeros((4, ROUTE_T), i32)
    w_ref[2:8, :] = jnp.zeros((6, ROUTE_T), f32)
    for c in range(ROUTE_T // cw):
        sl = slice(c * cw, (c + 1) * cw)
        gl = lt_ref[0:N_GROUPS, sl]
        gmax, gsel = _first_argmax(gl, io8)
        gw = 1.0 / jnp.sum(jnp.exp(gl - gmax), axis=0, keepdims=True)
        es = jnp.zeros((EXPERTS_PER_GROUP, cw), f32)
        for g in range(N_GROUPS):
            lo = N_GROUPS + g * EXPERTS_PER_GROUP
            es = jnp.where(gsel == float(g), lt_ref[lo:lo + EXPERTS_PER_GROUP, sl], es)
        v1, i1 = _first_argmax(es, io8)
        es2 = jnp.where(io8 == i1, -jnp.inf, es)
        v2, i2 = _first_argmax(es2, io8)
        e = jnp.exp(v2 - v1)
        w_ref[0:1, sl] = gw / (1.0 + e)
        w_ref[1:2, sl] = gw * e / (1.0 + e)
        e1 = gsel * float(EXPERTS_PER_GROUP) + i1
        e2 = gsel * float(EXPERTS_PER_GROUP) + i2
        oh1 = io64 == e1
        oh2 = io64 == e2
        oc = jnp.where(oh1, 1.0, jnp.where(oh2, 1.0, 0.0))
        pre = _dot(oc.astype(bf16), tri) + carry[...]
        ints_ref[0:1, sl] = e1.astype(i32)
        ints_ref[1:2, sl] = e2.astype(i32)
        ints_ref[2:3, sl] = jnp.sum(jnp.where(oh1, pre, 0.0), axis=0, keepdims=True).astype(i32)
        ints_ref[3:4, sl] = jnp.sum(jnp.where(oh2, pre, 0.0), axis=0, keepdims=True).astype(i32)
        carry[...] = carry[...] + jnp.sum(oc, axis=1, keepdims=True)
    cnt_ref[...] = jnp.broadcast_to(carry[...], (N_EXPERTS, LANES)).astype(i32)


def _route(logits_t):
    t = ROUTE_T
    return pl.pallas_call(
        _route_kernel,
        grid=(N_TOK // t,),
        in_specs=[pl.BlockSpec((ROUTER_ROWS, t), lambda c: (0, c))],
        out_specs=[pl.BlockSpec((8, t), lambda c: (0, c)),
                   pl.BlockSpec((8, t), lambda c: (0, c)),
                   pl.BlockSpec((N_EXPERTS, LANES), lambda c: (0, 0))],
        out_shape=[jax.ShapeDtypeStruct((8, N_TOK), i32),
                   jax.ShapeDtypeStruct((8, N_TOK), f32),
                   jax.ShapeDtypeStruct((N_EXPERTS, LANES), i32)],
        scratch_shapes=[pltpu.VMEM((N_EXPERTS, 1), f32)],
        compiler_params=_cparams(("arbitrary",)),
        name="route",
    )(logits_t)


def _plan_kernel(cnt_ref, ints_ref, pos_ref, blk_u_ref, used_e_ref, meta_ref, pstart_s):
    @pl.when(pl.program_id(0) == 0)
    def _():
        def per_expert(e, carry):
            b, u = carry
            nb = (cnt_ref[e] + (MOE_TM - 1)) // MOE_TM
            pstart_s[e] = b * MOE_TM
            used_e_ref[u] = e

            def mark(k, c):
                blk_u_ref[b + k] = u
                return c

            lax.fori_loop(0, nb, mark, 0)
            return b + nb, u + jnp.minimum(nb, 1)

        nb_total, nu = lax.fori_loop(0, N_EXPERTS, per_expert, (0, 0))
        meta_ref[0] = nb_total
        meta_ref[1] = nu
        last_e = used_e_ref[nu - 1]

        def fill_e(k, c):
            used_e_ref[k] = last_e
            return c

        lax.fori_loop(nu, N_EXPERTS, fill_e, 0)

        def fill_b(k, c):
            blk_u_ref[k] = nu - 1
            return c

        lax.fori_loop(nb_total, MOE_NBLK, fill_b, 0)

    t = ROUTE_T
    iocol = lax.broadcasted_iota(i32, (N_EXPERTS, 1), 0)

    def build(e, col):
        return jnp.where(iocol == e, pstart_s[e], col)

    pcol = lax.fori_loop(0, N_EXPERTS, build, jnp.zeros((N_EXPERTS, 1), i32)).astype(f32)
    io64 = lax.broadcasted_iota(i32, (N_EXPERTS, t), 0)
    e1 = ints_ref[0:1, :]
    e2 = ints_ref[1:2, :]
    pos_ref[0:1, :] = (jnp.sum(jnp.where(io64 == e1, pcol, 0.0), axis=0, keepdims=True).astype(i32)
                       + ints_ref[2:3, :])
    pos_ref[1:2, :] = (jnp.sum(jnp.where(io64 == e2, pcol, 0.0), axis=0, keepdims=True).astype(i32)
                       + ints_ref[3:4, :])
    pos_ref[2:8, :] = jnp.zeros((6, t), i32)


def _plan(counts, ints):
    t = ROUTE_T
    return pl.pallas_call(
        _plan_kernel,
        grid_spec=pltpu.PrefetchScalarGridSpec(
            num_scalar_prefetch=1,
            grid=(N_TOK // t,),
            in_specs=[pl.BlockSpec((8, t), lambda c, cnt: (0, c))],
            out_specs=[pl.BlockSpec((8, t), lambda c, cnt: (0, c)),
                       pl.BlockSpec(memory_space=pltpu.SMEM),
                       pl.BlockSpec(memory_space=pltpu.SMEM),
                       pl.BlockSpec(memory_space=pltpu.SMEM),
                       pl.BlockSpec(memory_space=pltpu.SMEM)]),
        out_shape=[jax.ShapeDtypeStruct((8, N_TOK), i32),
                   jax.ShapeDtypeStruct((MOE_NBLK,), i32),
                   jax.ShapeDtypeStruct((N_EXPERTS,), i32),
                   jax.ShapeDtypeStruct((2,), i32),
                   jax.ShapeDtypeStruct((N_EXPERTS,), i32)],
        compiler_params=_cparams(("arbitrary",)),
        name="plan",
    )(counts, ints)


DISP_CHUNK = 512
DMA_UNROLL = 8


PAD_ZERO_ROWS = 128


def _dispatch_kernel(pos_ref, cnt_ref, pstart_ref, hp_ref, xs_ref, zbuf, sem, zsem):
    base = pl.program_id(0) * DISP_CHUNK

    def zero_pads(wait):
        def go(copy, cond):
            @pl.when(cond)
            def _():
                copy.wait() if wait else copy.start()

        def per_expert(e, carry):
            n = cnt_ref[e]
            first = pstart_ref[e] + n
            pad = (-n) & (MOE_TM - 1)
            lead = (-first) & 7
            for k in range(7):
                go(pltpu.make_async_copy(zbuf.at[pl.ds(0, 1), :], xs_ref.at[pl.ds(first + k, 1), :], zsem),
                   (k < lead) & (k < pad))
            off = first + lead
            rest = pad - lead
            size = PAD_ZERO_ROWS
            while size >= 8:
                go(pltpu.make_async_copy(zbuf.at[pl.ds(0, size), :],
                                         xs_ref.at[pl.ds(pl.multiple_of(off, 8), size), :], zsem),
                   (rest & size) != 0)
                off = off + (rest & size)
                size //= 2
            return carry

        lax.fori_loop(0, N_EXPERTS, per_expert, 0)

    @pl.when(pl.program_id(0) == 0)
    def _():
        zbuf[...] = jnp.zeros_like(zbuf)
        zero_pads(wait=False)
        zero_pads(wait=True)

    def row_copy(r, k):
        dst = pos_ref[k * N_TOK + base + r]
        return pltpu.make_async_copy(hp_ref.at[pl.ds(r, 1), :], xs_ref.at[pl.ds(dst, 1), :], sem)

    def issue(g, carry):
        for u in range(DMA_UNROLL):
            r = g * DMA_UNROLL + u
            row_copy(r, 0).start()
            row_copy(r, 1).start()
        return carry

    lax.fori_loop(0, DISP_CHUNK // DMA_UNROLL, issue, 0)

    def drain(g, carry):
        for _ in range(DMA_UNROLL):
            row_copy(0, 0).wait()
            row_copy(0, 0).wait()
        return carry

    lax.fori_loop(0, DISP_CHUNK // DMA_UNROLL, drain, 0)


def _dispatch(pos_flat, counts, pstart, hp):
    return pl.pallas_call(
        _dispatch_kernel,
        grid_spec=pltpu.PrefetchScalarGridSpec(
            num_scalar_prefetch=3,
            grid=(N_TOK // DISP_CHUNK,),
            in_specs=[pl.BlockSpec((DISP_CHUNK, D_MODEL // 2), lambda i, pos, cnt, ps: (i, 0))],
            out_specs=pl.BlockSpec(memory_space=pl.ANY),
            scratch_shapes=[pltpu.VMEM((PAD_ZERO_ROWS, D_MODEL // 2), jnp.uint32),
                            pltpu.SemaphoreType.DMA(()),
                            pltpu.SemaphoreType.DMA(())]),
        out_shape=jax.ShapeDtypeStruct((MOE_ROWS, D_MODEL // 2), jnp.uint32),
        compiler_params=_cparams(("arbitrary",)),
        name="dispatch",
    )(pos_flat, counts, pstart, hp)


MOE_KC = 512
N_GU_CH = D_MODEL // MOE_KC
N_DN_CH = D_EXPERT // MOE_KC
N_CH = N_GU_CH + N_DN_CH


def _moe_kernel(blk_u, used_e, meta, xs_ref, wgu_hbm, wdn_hbm, y_ref, ring, wgu_b, wdn_b, sems):
    i = pl.program_id(0)
    u = blk_u[i]
    first = (i == 0) | (u != blk_u[jnp.maximum(i - 1, 0)])
    active = i < meta[0]

    def chunk_copy(e, c):
        if c < N_GU_CH:
            src = wgu_hbm.at[e, pl.ds(c * MOE_KC, MOE_KC), :]
        else:
            src = wdn_hbm.at[e, pl.ds((c - N_GU_CH) * MOE_KC, MOE_KC), :]
        return pltpu.make_async_copy(src, ring.at[c], sems.at[c])

    @pl.when(i == 0)
    def _():
        for c in range(N_CH):
            chunk_copy(used_e[0], c).start()

    def unpack_x():
        packed = xs_ref[...]
        x_lo = lax.bitcast_convert_type(packed << 16, f32).astype(bf16)
        x_hi = lax.bitcast_convert_type(packed & jnp.uint32(0xFFFF0000), f32).astype(bf16)
        return x_lo, x_hi

    def swiglu(gu):
        g = gu[:, :D_EXPERT]
        return (g * jax.nn.sigmoid(g) * gu[:, D_EXPERT:]).astype(bf16)

    def rows(c):
        return slice(c * MOE_KC, (c + 1) * MOE_KC)

    @pl.when(active & first)
    def _():
        e_next = used_e[jnp.minimum(u + 1, meta[1] - 1)]

        def take(c):
            chunk_copy(e_next, c).wait()
            wb = ring[c].astype(bf16)
            if c < N_GU_CH:
                wgu_b[rows(c), :] = wb
            else:
                wdn_b[rows(c - N_GU_CH), :] = wb
            chunk_copy(e_next, c).start()

        x_lo, x_hi = unpack_x()
        per_half = N_GU_CH // 2
        x_chunks = [xh[:, rows(c)] for xh in (x_lo, x_hi) for c in range(per_half)]
        take(0)
        gu = None
        for c in range(N_GU_CH):
            take(c + 1)
            d = _dot(x_chunks[c], wgu_b[rows(c), :])
            gu = d if gu is None else gu + d
        a = swiglu(gu)
        y = None
        for c in range(N_DN_CH):
            if c + 1 < N_DN_CH:
                take(N_GU_CH + c + 1)
            d = _dot(a[:, rows(c)], wdn_b[rows(c), :])
            y = d if y is None else y + d
        y_ref[...] = y

    @pl.when(active & jnp.logical_not(first))
    def _():
        x_lo, x_hi = unpack_x()
        half = D_MODEL // 2
        gu = _dot(x_lo, wgu_b[0:half, :]) + _dot(x_hi, wgu_b[half:, :])
        y_ref[...] = _dot(swiglu(gu), wdn_b[...])

    @pl.when(i == pl.num_programs(0) - 1)
    def _():
        for c in range(N_CH):
            chunk_copy(used_e[0], c).wait()


def _moe(blk_u, used_e, meta, xs, w_gu, w_dn):
    tm = MOE_TM
    blk = lambda i, bu, ue, mt: (jnp.minimum(i, mt[0] - 1), 0)
    return pl.pallas_call(
        _moe_kernel,
        grid_spec=pltpu.PrefetchScalarGridSpec(
            num_scalar_prefetch=3,
            grid=(MOE_NBLK,),
            in_specs=[pl.BlockSpec((tm, D_MODEL // 2), blk),
                      pl.BlockSpec(memory_space=pl.ANY),
                      pl.BlockSpec(memory_space=pl.ANY)],
            out_specs=pl.BlockSpec((tm, D_MODEL), blk),
            scratch_shapes=[pltpu.VMEM((N_CH, MOE_KC, D_MODEL), f32),
                            pltpu.VMEM((D_MODEL, 2 * D_EXPERT), bf16),
                            pltpu.VMEM((D_EXPERT, D_MODEL), bf16),
                            pltpu.SemaphoreType.DMA((N_CH,))]),
        out_shape=jax.ShapeDtypeStruct((MOE_ROWS, D_MODEL), f32),
        compiler_params=_cparams(("arbitrary",)),
        name="moe",
    )(blk_u, used_e, meta, xs, w_gu, w_dn)


COMB_TM = 256


def _combine_kernel(pos_ref, x1_ref, gt_ref, wc_ref, g_ref, y_ref, o_ref, ybuf_even, ybuf_odd, sems):
    i = pl.program_id(0)
    last = pl.num_programs(0) - 1
    nxt = jnp.minimum(i + 1, last)
    bufs = (ybuf_even, ybuf_odd)

    def row_copy(tile, r, k, s):
        src = pos_ref[k * N_TOK + tile * COMB_TM + r]
        return pltpu.make_async_copy(y_ref.at[pl.ds(src, 1), :], bufs[s].at[k, pl.ds(r, 1), :], sems.at[s])

    def drain(s):
        def body(g, carry):
            for _ in range(DMA_UNROLL):
                row_copy(0, 0, 0, s).wait()
                row_copy(0, 0, 0, s).wait()
            return carry
        lax.fori_loop(0, COMB_TM // DMA_UNROLL, body, 0)

    @pl.when(i == 0)
    def _():
        def body(g, carry):
            for u in range(DMA_UNROLL):
                r = g * DMA_UNROLL + u
                row_copy(0, r, 0, 0).start()
                row_copy(0, r, 1, 0).start()
            return carry
        lax.fori_loop(0, COMB_TM // DMA_UNROLL, body, 0)

    def step(s):
        drain(s)
        for r in range(COMB_TM):
            row_copy(nxt, r, 0, 1 - s).start()
            row_copy(nxt, r, 1, 1 - s).start()
        ffn = wc_ref[:, 0:1] * bufs[s][0] + wc_ref[:, 1:2] * bufs[s][1]
        xo = x1_ref[...] + gt_ref[...] * ffn
        ms = jnp.mean(xo * xo, axis=-1, keepdims=True)
        o_ref[...] = xo * lax.rsqrt(ms + RMS_EPS) * g_ref[...]

        @pl.when(i == last)
        def _():
            drain(1 - s)

    @pl.when(i % 2 == 0)
    def _():
        step(0)

    @pl.when(i % 2 == 1)
    def _():
        step(1)


def _combine(pos_flat, x1, mod3, w_col, g_final, y):
    tm = COMB_TM
    per_b = SEQ // tm
    return pl.pallas_call(
        _combine_kernel,
        grid_spec=pltpu.PrefetchScalarGridSpec(
            num_scalar_prefetch=1,
            grid=(N_TOK // tm,),
            in_specs=[pl.BlockSpec((tm, D_MODEL), lambda i, pos: (i, 0)),
                      pl.BlockSpec((None, 1, D_MODEL), lambda i, pos: (i // per_b, 0, 5)),
                      pl.BlockSpec((tm, 8), lambda i, pos: (i, 0)),
                      pl.BlockSpec((1, D_MODEL), lambda i, pos: (0, 0)),
                      pl.BlockSpec(memory_space=pl.ANY)],
            out_specs=pl.BlockSpec((tm, D_MODEL), lambda i, pos: (i, 0)),
            scratch_shapes=[pltpu.VMEM((TOP_K, tm, D_MODEL), f32),
                            pltpu.VMEM((TOP_K, tm, D_MODEL), f32),
                            pltpu.SemaphoreType.DMA((2,))]),
        out_shape=jax.ShapeDtypeStruct((N_TOK, D_MODEL), f32),
        compiler_params=_cparams(("arbitrary",)),
        name="combine",
    )(pos_flat, x1, mod3, w_col, g_final, y)


def kernel(x, c, positions, w_ada, b_ada, g_mix, w_in, swa_sinks, diff_lambda_q1, diff_lambda_k1, diff_lambda_q2, diff_lambda_k2, diff_subln_g, w_branch_a, w_branch_b, w_out, g_ffn, w_router_group, b_router_group, w_router_expert, b_router_expert, w_exp_gate_up, w_exp_down, g_final):
    layer = 0
    x2 = x.reshape(N_TOK, D_MODEL)

    mod = _ada(c, w_ada[layer], b_ada[layer])
    mod3 = mod.reshape(BATCH, 1, 6 * D_MODEL)

    w_bf = w_in[layer].astype(bf16)
    s_qa = SWA_Q_HEADS * HEAD_DIM
    kv_heads = [w_bf[:, s_qa + h * HEAD_DIM:s_qa + (h + 1) * HEAD_DIM] for h in range(2 * SWA_KV_HEADS)]
    w_kv = jnp.concatenate([h for h in kv_heads for _ in range(2)], axis=1)
    half = HEAD_DIM // 2
    inv_freq = ROPE_THETA ** (-jnp.arange(0, HEAD_DIM, 2, dtype=f32) / HEAD_DIM)
    invf = jnp.tile(inv_freq, LANES // half).reshape(1, LANES)
    pos4 = jnp.repeat(positions.reshape(N_TOK // 4, 4).astype(i32), half, axis=1)
    cos4, sin4 = _rope_tables(pos4, invf)
    cos_t = jnp.tile(cos4.reshape(N_TOK, half), (1, LANES // half))
    sin_t = jnp.tile(sin4.reshape(N_TOK, half), (1, LANES // half))

    proj, proj_kv = _proj(x2, mod3, g_mix[layer].reshape(1, D_MODEL), cos_t, sin_t, w_bf, w_kv)
    o_a = _swa(swa_sinks[layer], proj, proj_kv)
    o_b = _diff(diff_lambda_q1[layer].reshape(1, HEAD_DIM), diff_lambda_k1[layer].reshape(1, HEAD_DIM),
                diff_lambda_q2[layer].reshape(1, HEAD_DIM), diff_lambda_k2[layer].reshape(1, HEAD_DIM),
                diff_subln_g[layer].reshape(LANES, 1), proj)

    wr = jnp.concatenate([w_router_group[layer], w_router_expert[layer]], axis=1)
    wr_t = jnp.pad(wr.T, ((0, ROUTER_ROWS - wr.shape[1]), (0, 0)))
    br = jnp.concatenate([b_router_group[layer], b_router_expert[layer]])
    br_col = jnp.pad(br, (0, ROUTER_ROWS - br.shape[0])).reshape(ROUTER_ROWS, 1)
    x1, hp, logits_t = _merge(o_a, o_b, proj, x2, mod3, g_ffn[layer].reshape(1, D_MODEL),
                              w_branch_a[layer].astype(bf16), w_branch_b[layer].astype(bf16),
                              w_out[layer].astype(bf16), wr_t, br_col)

    ints, wts, counts = _route(logits_t)
    cnt = counts[:, 0]
    pos, blk_u, used_e, meta, pstart = _plan(cnt, ints)
    pos_flat = pos[0:TOP_K].reshape(TOP_K * N_TOK)
    xs = _dispatch(pos_flat, cnt, pstart, hp)
    y = _moe(blk_u, used_e, meta, xs, w_exp_gate_up[layer], w_exp_down[layer])
    out = _combine(pos_flat, x1, mod3, wts.T, g_final.reshape(1, D_MODEL), y)
    return out.reshape(BATCH, SEQ, D_MODEL)
```

```python
import math

import jax
import jax.numpy as jnp
from jax import lax
from jax.experimental import pallas as pl
from jax.experimental.pallas import tpu as pltpu

D_MODEL = 2048
BATCH = 8
SEQ = 2048
N_TOK = BATCH * SEQ
HEAD_DIM = 64
ROPE_THETA = 10000.0
RMS_EPS = 1e-6
NEG_INF = -1e30
SWA_Q_HEADS = 16
SWA_KV_HEADS = 2
SWA_GROUP = SWA_Q_HEADS // SWA_KV_HEADS
WINDOW = 128
DIFF_HEADS = 8
N_GROUPS = 8
EXPERTS_PER_GROUP = 8
N_EXPERTS = N_GROUPS * EXPERTS_PER_GROUP
TOP_K = 2
D_EXPERT = D_MODEL // 2
LAMBDA_INIT = 0.8 - 0.6 * math.exp(-0.3 * 0)
LOG2_E = math.log2(math.e)

LANES = 128
VMEM_LIMIT = 56 * 1024 * 1024

PROJ_TN = 1024
PROJ_Q = 256
PROJ_MAIN_TILES = 8
PROJ_MAIN_COLS = PROJ_MAIN_TILES * PROJ_TN
PROJ_KV_COLS = 512
COL_QA, COL_QD, COL_KD, COL_VD, COL_GA, COL_GB = 0, 1024, 2048, 3072, 4096, 6144

MOE_TM = 256
MOE_ROWS = N_TOK * TOP_K + N_EXPERTS * MOE_TM
MOE_NBLK = MOE_ROWS // MOE_TM
ROUTE_T = 2048
ROUTER_ROWS = 128

f32 = jnp.float32
bf16 = jnp.bfloat16
i32 = jnp.int32


def _cparams(sem, vmem=VMEM_LIMIT):
    return pltpu.CompilerParams(dimension_semantics=sem, vmem_limit_bytes=vmem)


def _dot(a, b):
    return jnp.dot(a, b, preferred_element_type=f32)


def _dot_nt(a, b):
    return lax.dot_general(a, b, (((1,), (1,)), ((), ())), preferred_element_type=f32)


def _ada_kernel(c_ref, w_ref, b_ref, o_ref):
    c = c_ref[...]
    ca = (c * jax.nn.sigmoid(c)).astype(bf16)
    o_ref[...] = _dot(ca, w_ref[...].astype(bf16)) + b_ref[...]


def _ada(c, w_ada, b_ada):
    tn = 512
    n = w_ada.shape[1]
    return pl.pallas_call(
        _ada_kernel,
        grid=(n // tn,),
        in_specs=[pl.BlockSpec((BATCH, D_MODEL), lambda j: (0, 0)),
                  pl.BlockSpec((D_MODEL, tn), lambda j: (0, j)),
                  pl.BlockSpec((1, tn), lambda j: (0, j))],
        out_specs=pl.BlockSpec((BATCH, tn), lambda j: (0, j)),
        out_shape=jax.ShapeDtypeStruct((BATCH, n), f32),
        compiler_params=_cparams(("arbitrary",)),
        name="ada",
    )(c, w_ada, b_ada.reshape(1, n))


PROJ_TM = 1024


def _rope128(xc, cos, sin_signed, first_half):
    r_prev = pltpu.roll(xc, 32, 1)
    r_next = pltpu.roll(xc, 96, 1)
    return xc * cos + jnp.where(first_half, r_next, r_prev) * sin_signed


def _rope_table_kernel(pos_ref, invf_ref, cos_ref, sin_ref):
    ang = pos_ref[...].astype(f32) * invf_ref[...]
    cos_ref[...] = jnp.cos(ang)
    sin_ref[...] = jnp.sin(ang)


def _rope_tables(pos4, invf):
    rows = pos4.shape[0]
    tr = 512
    spec = pl.BlockSpec((tr, LANES), lambda i: (i, 0))
    return pl.pallas_call(
        _rope_table_kernel,
        grid=(rows // tr,),
        in_specs=[spec, pl.BlockSpec((1, LANES), lambda i: (0, 0))],
        out_specs=[spec, spec],
        out_shape=[jax.ShapeDtypeStruct((rows, LANES), f32)] * 2,
        compiler_params=_cparams(("arbitrary",)),
        name="rope_tables",
    )(pos4, invf)


def _proj_kernel(x_ref, sc_ref, sh_ref, g_ref, cos_ref, sin_ref, w0_ref, w1_ref, w2_ref, w3_ref, wkv_ref,
                 o_ref, okv_ref, h_scr, sin_scr):
    j = pl.program_id(1)
    main = (w0_ref, w1_ref, w2_ref, w3_ref)
    lane = lax.broadcasted_iota(i32, (1, LANES), 1)
    first_half = (lane % HEAD_DIM) < (HEAD_DIM // 2)

    @pl.when(j == 0)
    def _():
        rc = 256
        for r in range(PROJ_TM // rc):
            x = x_ref[r * rc:(r + 1) * rc, :]
            ms = jnp.mean(x * x, axis=-1, keepdims=True)
            y = x * lax.rsqrt(ms + RMS_EPS) * g_ref[...]
            h = y * (1.0 + sc_ref[...]) + sh_ref[...]
            h_scr[r * rc:(r + 1) * rc, :] = h.astype(bf16)
        sin_scr[...] = sin_ref[...] * jnp.where(first_half, -1.0, 1.0)

    qw = PROJ_Q

    def rope_q(out, acc, off, scale):
        for c in range(qw // LANES):
            xc = acc[:, c * LANES:(c + 1) * LANES]
            out[:, off + c * LANES:off + (c + 1) * LANES] = (
                _rope128(xc, cos_ref[...], sin_scr[...], first_half) * scale).astype(bf16)

    def cast_q(out, acc, off, scale):
        out[:, off:off + qw] = acc.astype(bf16)

    def tile(out, quarters, epilogues, scale=1.0):
        for s, (w, epilogue) in enumerate(zip(quarters, epilogues)):
            epilogue(out, _dot(h_scr[...], w[...]), s * qw, scale)

    @pl.when(j == 0)
    def _():
        tile(o_ref, main, (rope_q,) * 4, 1.0 / math.sqrt(HEAD_DIM))

    @pl.when(j == 1)
    def _():
        tile(o_ref, main, (rope_q,) * 4, LOG2_E / math.sqrt(HEAD_DIM))

    @pl.when(j == 2)
    def _():
        tile(o_ref, main, (rope_q,) * 4)

    @pl.when((j >= 3) & (j < PROJ_MAIN_TILES))
    def _():
        tile(o_ref, main, (cast_q,) * 4)

    @pl.when(j == PROJ_MAIN_TILES)
    def _():
        tile(okv_ref, (wkv_ref.at[:, 0:qw], wkv_ref.at[:, qw:2 * qw]), (rope_q, cast_q))


def _proj(x2, mod3, g_mix, cos_t, sin_t, w_bf, w_kv):
    tm = PROJ_TM
    per_b = SEQ // tm
    per_tile = PROJ_TN // PROJ_Q
    kv_blocks = 2 * SWA_KV_HEADS * HEAD_DIM // PROJ_Q

    def wcol(quarter):
        def index(i, j):
            jj = jnp.minimum(j, PROJ_MAIN_TILES - 1)
            return 0, per_tile * jj + quarter + jnp.where(jj >= SWA_Q_HEADS * HEAD_DIM // PROJ_TN, kv_blocks, 0)
        return index

    main_blk = lambda i, j: (i, jnp.minimum(j, PROJ_MAIN_TILES - 1))
    return pl.pallas_call(
        _proj_kernel,
        grid=(N_TOK // tm, PROJ_MAIN_TILES + 1),
        in_specs=[pl.BlockSpec((tm, D_MODEL), lambda i, j: (i, 0)),
                  pl.BlockSpec((None, 1, D_MODEL), lambda i, j: (i // per_b, 0, 1)),
                  pl.BlockSpec((None, 1, D_MODEL), lambda i, j: (i // per_b, 0, 0)),
                  pl.BlockSpec((1, D_MODEL), lambda i, j: (0, 0)),
                  pl.BlockSpec((tm, LANES), lambda i, j: (i, 0)),
                  pl.BlockSpec((tm, LANES), lambda i, j: (i, 0))]
                 + [pl.BlockSpec((D_MODEL, PROJ_Q), wcol(q)) for q in range(per_tile)]
                 + [pl.BlockSpec((D_MODEL, PROJ_KV_COLS), lambda i, j: (0, 0))],
        out_specs=[pl.BlockSpec((tm, PROJ_TN), main_blk),
                   pl.BlockSpec((tm, PROJ_KV_COLS), lambda i, j: (i, 0))],
        out_shape=[jax.ShapeDtypeStruct((N_TOK, PROJ_MAIN_COLS), bf16),
                   jax.ShapeDtypeStruct((N_TOK, PROJ_KV_COLS), bf16)],
        scratch_shapes=[pltpu.VMEM((tm, D_MODEL), bf16),
                        pltpu.VMEM((tm, LANES), f32)],
        compiler_params=_cparams(("arbitrary", "arbitrary")),
        name="proj",
    )(x2, mod3, mod3, g_mix, cos_t, sin_t, w_bf, w_bf, w_bf, w_bf, w_kv)


def _swa_kernel(sink_ref, q_ref, kv_ref, kvp_ref, o_ref):
    n = pl.program_id(1)
    w = WINDOW
    qi = lax.broadcasted_iota(i32, (w, 2 * w), 0)
    kj = lax.broadcasted_iota(i32, (w, 2 * w), 1)
    rel = qi + w - kj
    first_key = jnp.where(n > 0, 0, w)
    valid = (rel >= 0) & (rel < w) & (kj >= first_key)
    lane = lax.broadcasted_iota(i32, (w, LANES), 1)
    low = lane < HEAD_DIM
    pairs = SWA_GROUP // 2
    for kvh in range(SWA_KV_HEADS):
        ks = slice(kvh * LANES, (kvh + 1) * LANES)
        vs = slice((SWA_KV_HEADS + kvh) * LANES, (SWA_KV_HEADS + kvh + 1) * LANES)
        kk = jnp.concatenate([kvp_ref[:, ks], kv_ref[:, ks]], axis=0)
        vv = jnp.concatenate([kvp_ref[:, vs], kv_ref[:, vs]], axis=0)
        parts = []
        for pair in range(pairs):
            gp = kvh * pairs + pair
            q2 = q_ref[:, gp * LANES:(gp + 1) * LANES]
            zero = jnp.zeros_like(q2)
            parts += [jnp.where(low, q2, zero), jnp.where(low, zero, q2)]
        s_all = _dot_nt(jnp.concatenate(parts, axis=0), kk)
        ps, dens = [], []
        for g in range(SWA_GROUP):
            s = jnp.where(valid, s_all[g * w:(g + 1) * w, :], NEG_INF)
            sink = sink_ref[kvh * SWA_GROUP + g]
            m = jnp.maximum(jnp.max(s, axis=-1, keepdims=True), sink)
            p = jnp.exp(s - m)
            dens.append(jnp.sum(p, axis=-1, keepdims=True) + jnp.exp(sink - m))
            ps.append(p.astype(bf16))
        o_all = _dot(jnp.concatenate(ps, axis=0), vv)
        for pair in range(pairs):
            gp = kvh * pairs + pair
            o_even = o_all[(2 * pair) * w:(2 * pair + 1) * w, :] / dens[2 * pair]
            o_odd = o_all[(2 * pair + 1) * w:(2 * pair + 2) * w, :] / dens[2 * pair + 1]
            o_ref[:, gp * LANES:(gp + 1) * LANES] = jnp.where(low, o_even, o_odd).astype(bf16)


def _swa(sinks, proj, proj_kv):
    nb = SEQ // WINDOW
    return pl.pallas_call(
        _swa_kernel,
        grid=(BATCH, nb),
        in_specs=[pl.BlockSpec(memory_space=pltpu.SMEM),
                  pl.BlockSpec((WINDOW, SWA_Q_HEADS * HEAD_DIM), lambda b, n: (b * nb + n, 0)),
                  pl.BlockSpec((WINDOW, PROJ_KV_COLS), lambda b, n: (b * nb + n, 0)),
                  pl.BlockSpec((WINDOW, PROJ_KV_COLS), lambda b, n: (jnp.maximum(b * nb + n - 1, 0), 0))],
        out_specs=pl.BlockSpec((WINDOW, SWA_Q_HEADS * HEAD_DIM), lambda b, n: (b * nb + n, 0)),
        out_shape=jax.ShapeDtypeStruct((N_TOK, SWA_Q_HEADS * HEAD_DIM), bf16),
        compiler_params=_cparams(("arbitrary", "arbitrary")),
        name="swa",
    )(sinks, proj, proj_kv, proj_kv)


DIFF_T = 256


def _diff_kernel(lq1_ref, lk1_ref, lq2_ref, lk2_ref, gcol_ref, q_ref, k_ref, v_ref, o_ref, vt_scr):
    t = DIFF_T
    nt = SEQ // t
    for j in range(nt):
        vt_scr[:, j * t:(j + 1) * t] = v_ref[j * t:(j + 1) * t, :].T
    lam = (jnp.exp(jnp.sum(lq1_ref[...] * lk1_ref[...], axis=-1, keepdims=True))
           - jnp.exp(jnp.sum(lq2_ref[...] * lk2_ref[...], axis=-1, keepdims=True)) + LAMBDA_INIT)
    lane = lax.broadcasted_iota(i32, (t, LANES), 1)
    causal = lax.broadcasted_iota(i32, (t, t), 0) <= lax.broadcasted_iota(i32, (t, t), 1)

    def one_map(qz, qi):
        lo = qi * t
        s_d = jnp.where(causal, _dot_nt(k_ref[lo:lo + t, :], qz), NEG_INF)
        m = jnp.max(s_d, axis=0, keepdims=True)
        if qi > 0:
            s_u = _dot_nt(k_ref[0:lo, :], qz)
            m = jnp.maximum(m, jnp.max(s_u, axis=0, keepdims=True))
        p_d = jnp.exp2(s_d - m)
        l = jnp.sum(p_d, axis=0, keepdims=True)
        p_u = None
        if qi > 0:
            p_u = jnp.exp2(s_u - m)
            l = l + jnp.sum(p_u, axis=0, keepdims=True)
        return p_d, p_u, l

    for qi in range(nt):
        lo = qi * t
        q = q_ref[lo:lo + t, :]
        zero = jnp.zeros_like(q)
        pd1, pu1, l1 = one_map(jnp.where(lane < HEAD_DIM, q, zero), qi)
        pd2, pu2, l2 = one_map(jnp.where(lane >= HEAD_DIM, q, zero), qi)
        c2 = lam * l1 / l2
        o = _dot(vt_scr[:, lo:lo + t], (pd1 - pd2 * c2).astype(bf16))
        if qi > 0:
            o = o + _dot(vt_scr[:, 0:lo], (pu1 - pu2 * c2).astype(bf16))
        o = o / l1
        y = o * lax.rsqrt(jnp.mean(o * o, axis=0, keepdims=True) + RMS_EPS) * gcol_ref[...]
        o_ref[qi * t:(qi + 1) * t, :] = (y * (1.0 - LAMBDA_INIT)).T.astype(bf16)


def _diff(lq1, lk1, lq2, lk2, subln_gcol, proj):
    vec = lambda n: pl.BlockSpec((1, n), lambda b, h: (0, 0))
    return pl.pallas_call(
        _diff_kernel,
        grid=(BATCH, DIFF_HEADS),
        in_specs=[vec(HEAD_DIM), vec(HEAD_DIM), vec(HEAD_DIM), vec(HEAD_DIM),
                  pl.BlockSpec((LANES, 1), lambda b, h: (0, 0)),
                  pl.BlockSpec((SEQ, LANES), lambda b, h: (b, COL_QD // LANES + h)),
                  pl.BlockSpec((SEQ, LANES), lambda b, h: (b, COL_KD // LANES + h)),
                  pl.BlockSpec((SEQ, LANES), lambda b, h: (b, COL_VD // LANES + h))],
        out_specs=pl.BlockSpec((SEQ, LANES), lambda b, h: (b, h)),
        out_shape=jax.ShapeDtypeStruct((N_TOK, DIFF_HEADS * LANES), bf16),
        scratch_shapes=[pltpu.VMEM((LANES, SEQ), bf16)],
        compiler_params=_cparams(("arbitrary", "arbitrary")),
        name="diff",
    )(lq1, lk1, lq2, lk2, subln_gcol, proj, proj, proj)


MERGE_TM = 256


def _merge_kernel(oa_ref, ob_ref, ga_ref, gb_ref, x_ref, gt_ref, sc_ref, sh_ref, g_ref,
                  wa_ref, wb_ref, wo_ref, wr_ref, br_ref, x1_ref, hp_ref, lt_ref, y_even, y_odd):
    i = pl.program_id(0)

    @pl.when(i == 0)
    def _():
        y_odd[...] = jnp.zeros_like(y_odd)

    def step(y_cur, y_prev):
        x1 = x_ref[...] + gt_ref[...] * y_prev[...]
        x1_ref[...] = x1
        ms = jnp.mean(x1 * x1, axis=-1, keepdims=True)
        h2 = (x1 * lax.rsqrt(ms + RMS_EPS) * g_ref[...]) * (1.0 + sc_ref[...]) + sh_ref[...]
        hb = lax.bitcast_convert_type(h2.astype(bf16).astype(f32), jnp.uint32)
        half = D_MODEL // 2
        hp_ref[...] = (hb[:, :half] >> 16) | (hb[:, half:] & jnp.uint32(0xFFFF0000))
        lt_ref[...] = _dot_nt(wr_ref[...].astype(bf16), h2.astype(bf16)) + br_ref[...]

        a = _dot(oa_ref[...], wa_ref[...])
        b = _dot(ob_ref[...], wb_ref[...])
        merged = (jax.nn.sigmoid(ga_ref[...].astype(f32)) * a + jax.nn.sigmoid(gb_ref[...].astype(f32)) * b)
        y_cur[...] = _dot(merged.astype(bf16), wo_ref[...])

    @pl.when(i % 2 == 0)
    def _():
        step(y_even, y_odd)

    @pl.when(i % 2 == 1)
    def _():
        step(y_odd, y_even)


def _merge(o_a, o_b, proj, x2, mod3, g_ffn, wa, wb, wo, wr_t, br_col):
    tm = MERGE_TM
    per_b = SEQ // tm
    nt = N_TOK // tm
    cur = lambda i: jnp.minimum(i, nt - 1)
    prev = lambda i: jnp.maximum(i - 1, 0)
    const = lambda shape: pl.BlockSpec(shape, lambda i: (0,) * len(shape), pipeline_mode=pl.Buffered(1))
    modspec = lambda chunk: pl.BlockSpec((None, 1, D_MODEL), lambda i: (prev(i) // per_b, 0, chunk))
    return pl.pallas_call(
        _merge_kernel,
        grid=(nt + 1,),
        in_specs=[pl.BlockSpec((tm, 1024), lambda i: (cur(i), 0)),
                  pl.BlockSpec((tm, 1024), lambda i: (cur(i), 0)),
                  pl.BlockSpec((tm, D_MODEL), lambda i: (cur(i), COL_GA // D_MODEL)),
                  pl.BlockSpec((tm, D_MODEL), lambda i: (cur(i), COL_GB // D_MODEL)),
                  pl.BlockSpec((tm, D_MODEL), lambda i: (prev(i), 0)),
                  modspec(2), modspec(4), modspec(3),
                  const((1, D_MODEL)),
                  const((1024, D_MODEL)), const((1024, D_MODEL)), const((D_MODEL, D_MODEL)),
                  const((ROUTER_ROWS, D_MODEL)), const((ROUTER_ROWS, 1))],
        out_specs=[pl.BlockSpec((tm, D_MODEL), lambda i: (prev(i), 0)),
                   pl.BlockSpec((tm, D_MODEL // 2), lambda i: (prev(i), 0)),
                   pl.BlockSpec((ROUTER_ROWS, tm), lambda i: (0, prev(i)))],
        out_shape=[jax.ShapeDtypeStruct((N_TOK, D_MODEL), f32),
                   jax.ShapeDtypeStruct((N_TOK, D_MODEL // 2), jnp.uint32),
                   jax.ShapeDtypeStruct((ROUTER_ROWS, N_TOK), f32)],
        scratch_shapes=[pltpu.VMEM((tm, D_MODEL), f32), pltpu.VMEM((tm, D_MODEL), f32)],
        compiler_params=_cparams(("arbitrary",)),
        name="merge",
    )(o_a, o_b, proj, proj, x2, mod3, mod3, mod3, g_ffn, wa, wb, wo, wr_t, br_col)


def _first_argmax(v, iota_f):
    vmax = jnp.max(v, axis=0, keepdims=True)
    idx = jnp.min(jnp.where(v == vmax, iota_f, float(v.shape[0])), axis=0, keepdims=True)
    return vmax, idx


def _route_kernel(lt_ref, ints_ref, w_ref, cnt_ref, carry):
    @pl.when(pl.program_id(0) == 0)
    def _():
        carry[...] = jnp.zeros_like(carry)

    cw = 256
    io8 = lax.broadcasted_iota(i32, (N_GROUPS, cw), 0).astype(f32)
    io64 = lax.broadcasted_iota(i32, (N_EXPERTS, cw), 0).astype(f32)
    tri = jnp.where(lax.broadcasted_iota(i32, (cw, cw), 0) < lax.broadcasted_iota(i32, (cw, cw), 1),
                    1.0, 0.0).astype(bf16)
    ints_ref[4:8, :] = jnp.zeros((4, ROUTE_T), i32)
    w_ref[2:8, :] = jnp.zeros((6, ROUTE_T), f32)
    for c in range(ROUTE_T // cw):
        sl = slice(c * cw, (c + 1) * cw)
        gl = lt_ref[0:N_GROUPS, sl]
        gmax, gsel = _first_argmax(gl, io8)
        gw = 1.0 / jnp.sum(jnp.exp(gl - gmax), axis=0, keepdims=True)
        es = jnp.zeros((EXPERTS_PER_GROUP, cw), f32)
        for g in range(N_GROUPS):
            lo = N_GROUPS + g * EXPERTS_PER_GROUP
            es = jnp.where(gsel == float(g), lt_ref[lo:lo + EXPERTS_PER_GROUP, sl], es)
        v1, i1 = _first_argmax(es, io8)
        es2 = jnp.where(io8 == i1, -jnp.inf, es)
        v2, i2 = _first_argmax(es2, io8)
        e = jnp.exp(v2 - v1)
        w_ref[0:1, sl] = gw / (1.0 + e)
        w_ref[1:2, sl] = gw * e / (1.0 + e)
        e1 = gsel * float(EXPERTS_PER_GROUP) + i1
        e2 = gsel * float(EXPERTS_PER_GROUP) + i2
        oh1 = io64 == e1
        oh2 = io64 == e2
        oc = jnp.where(oh1, 1.0, jnp.where(oh2, 1.0, 0.0))
        pre = _dot(oc.astype(bf16), tri) + carry[...]
        ints_ref[0:1, sl] = e1.astype(i32)
        ints_ref[1:2, sl] = e2.astype(i32)
        ints_ref[2:3, sl] = jnp.sum(jnp.where(oh1, pre, 0.0), axis=0, keepdims=True).astype(i32)
        ints_ref[3:4, sl] = jnp.sum(jnp.where(oh2, pre, 0.0), axis=0, keepdims=True).astype(i32)
        carry[...] = carry[...] + jnp.sum(oc, axis=1, keepdims=True)
    cnt_ref[...] = jnp.broadcast_to(carry[...], (N_EXPERTS, LANES)).astype(i32)


def _route(logits_t):
    t = ROUTE_T
    return pl.pallas_call(
        _route_kernel,
        grid=(N_TOK // t,),
        in_specs=[pl.BlockSpec((ROUTER_ROWS, t), lambda c: (0, c))],
        out_specs=[pl.BlockSpec((8, t), lambda c: (0, c)),
                   pl.BlockSpec((8, t), lambda c: (0, c)),
                   pl.BlockSpec((N_EXPERTS, LANES), lambda c: (0, 0))],
        out_shape=[jax.ShapeDtypeStruct((8, N_TOK), i32),
                   jax.ShapeDtypeStruct((8, N_TOK), f32),
                   jax.ShapeDtypeStruct((N_EXPERTS, LANES), i32)],
        scratch_shapes=[pltpu.VMEM((N_EXPERTS, 1), f32)],
        compiler_params=_cparams(("arbitrary",)),
        name="route",
    )(logits_t)


def _plan_kernel(cnt_ref, ints_ref, pos_ref, blk_u_ref, used_e_ref, meta_ref, pstart_s):
    @pl.when(pl.program_id(0) == 0)
    def _():
        def per_expert(e, carry):
            b, u = carry
            nb = (cnt_ref[e] + (MOE_TM - 1)) // MOE_TM
            pstart_s[e] = b * MOE_TM
            used_e_ref[u] = e

            def mark(k, c):
                blk_u_ref[b + k] = u
                return c

            lax.fori_loop(0, nb, mark, 0)
            return b + nb, u + jnp.minimum(nb, 1)

        nb_total, nu = lax.fori_loop(0, N_EXPERTS, per_expert, (0, 0))
        meta_ref[0] = nb_total
        meta_ref[1] = nu
        last_e = used_e_ref[nu - 1]

        def fill_e(k, c):
            used_e_ref[k] = last_e
            return c

        lax.fori_loop(nu, N_EXPERTS, fill_e, 0)

        def fill_b(k, c):
            blk_u_ref[k] = nu - 1
            return c

        lax.fori_loop(nb_total, MOE_NBLK, fill_b, 0)

    t = ROUTE_T
    iocol = lax.broadcasted_iota(i32, (N_EXPERTS, 1), 0)

    def build(e, col):
        return jnp.where(iocol == e, pstart_s[e], col)

    pcol = lax.fori_loop(0, N_EXPERTS, build, jnp.zeros((N_EXPERTS, 1), i32)).astype(f32)
    io64 = lax.broadcasted_iota(i32, (N_EXPERTS, t), 0)
    e1 = ints_ref[0:1, :]
    e2 = ints_ref[1:2, :]
    pos_ref[0:1, :] = (jnp.sum(jnp.where(io64 == e1, pcol, 0.0), axis=0, keepdims=True).astype(i32)
                       + ints_ref[2:3, :])
    pos_ref[1:2, :] = (jnp.sum(jnp.where(io64 == e2, pcol, 0.0), axis=0, keepdims=True).astype(i32)
                       + ints_ref[3:4, :])
    pos_ref[2:8, :] = jnp.zeros((6, t), i32)


def _plan(counts, ints):
    t = ROUTE_T
    return pl.pallas_call(
        _plan_kernel,
        grid_spec=pltpu.PrefetchScalarGridSpec(
            num_scalar_prefetch=1,
            grid=(N_TOK // t,),
            in_specs=[pl.BlockSpec((8, t), lambda c, cnt: (0, c))],
            out_specs=[pl.BlockSpec((8, t), lambda c, cnt: (0, c)),
                       pl.BlockSpec(memory_space=pltpu.SMEM),
                       pl.BlockSpec(memory_space=pltpu.SMEM),
                       pl.BlockSpec(memory_space=pltpu.SMEM),
                       pl.BlockSpec(memory_space=pltpu.SMEM)]),
        out_shape=[jax.ShapeDtypeStruct((8, N_TOK), i32),
                   jax.ShapeDtypeStruct((MOE_NBLK,), i32),
                   jax.ShapeDtypeStruct((N_EXPERTS,), i32),
                   jax.ShapeDtypeStruct((2,), i32),
                   jax.ShapeDtypeStruct((N_EXPERTS,), i32)],
        compiler_params=_cparams(("arbitrary",)),
        name="plan",
    )(counts, ints)


DISP_CHUNK = 512
DMA_UNROLL = 8


PAD_ZERO_ROWS = 128


def _dispatch_kernel(pos_ref, cnt_ref, pstart_ref, hp_ref, xs_ref, zbuf, sem, zsem):
    base = pl.program_id(0) * DISP_CHUNK

    def zero_pads(wait):
        def go(copy, cond):
            @pl.when(cond)
            def _():
                copy.wait() if wait else copy.start()

        def per_expert(e, carry):
            n = cnt_ref[e]
            first = pstart_ref[e] + n
            pad = (-n) & (MOE_TM - 1)
            lead = (-first) & 7
            for k in range(7):
                go(pltpu.make_async_copy(zbuf.at[pl.ds(0, 1), :], xs_ref.at[pl.ds(first + k, 1), :], zsem),
                   (k < lead) & (k < pad))
            off = first + lead
            rest = pad - lead
            size = PAD_ZERO_ROWS
            while size >= 8:
                go(pltpu.make_async_copy(zbuf.at[pl.ds(0, size), :],
                                         xs_ref.at[pl.ds(pl.multiple_of(off, 8), size), :], zsem),
                   (rest & size) != 0)
                off = off + (rest & size)
                size //= 2
            return carry

        lax.fori_loop(0, N_EXPERTS, per_expert, 0)

    @pl.when(pl.program_id(0) == 0)
    def _():
        zbuf[...] = jnp.zeros_like(zbuf)
        zero_pads(wait=False)
        zero_pads(wait=True)

    def row_copy(r, k):
        dst = pos_ref[k * N_TOK + base + r]
        return pltpu.make_async_copy(hp_ref.at[pl.ds(r, 1), :], xs_ref.at[pl.ds(dst, 1), :], sem)

    def issue(g, carry):
        for u in range(DMA_UNROLL):
            r = g * DMA_UNROLL + u
            row_copy(r, 0).start()
            row_copy(r, 1).start(priority=1)
        return carry

    lax.fori_loop(0, DISP_CHUNK // DMA_UNROLL, issue, 0)

    def drain(g, carry):
        for _ in range(DMA_UNROLL):
            row_copy(0, 0).wait()
            row_copy(0, 0).wait()
        return carry

    lax.fori_loop(0, DISP_CHUNK // DMA_UNROLL, drain, 0)


def _dispatch(pos_flat, counts, pstart, hp):
    return pl.pallas_call(
        _dispatch_kernel,
        grid_spec=pltpu.PrefetchScalarGridSpec(
            num_scalar_prefetch=3,
            grid=(N_TOK // DISP_CHUNK,),
            in_specs=[pl.BlockSpec((DISP_CHUNK, D_MODEL // 2), lambda i, pos, cnt, ps: (i, 0))],
            out_specs=pl.BlockSpec(memory_space=pl.ANY),
            scratch_shapes=[pltpu.VMEM((PAD_ZERO_ROWS, D_MODEL // 2), jnp.uint32),
                            pltpu.SemaphoreType.DMA(()),
                            pltpu.SemaphoreType.DMA(())]),
        out_shape=jax.ShapeDtypeStruct((MOE_ROWS, D_MODEL // 2), jnp.uint32),
        compiler_params=_cparams(("arbitrary",)),
        name="dispatch",
    )(pos_flat, counts, pstart, hp)


MOE_KC = 512
N_GU_CH = D_MODEL // MOE_KC
N_DN_CH = D_EXPERT // MOE_KC
N_CH = N_GU_CH + N_DN_CH


def _moe_kernel(blk_u, used_e, meta, xs_ref, wgu_hbm, wdn_hbm, y_ref, ring, wgu_b, wdn_b, sems):
    i = pl.program_id(0)
    u = blk_u[i]
    first = (i == 0) | (u != blk_u[jnp.maximum(i - 1, 0)])
    active = i < meta[0]

    def chunk_copy(e, c):
        if c < N_GU_CH:
            src = wgu_hbm.at[e, pl.ds(c * MOE_KC, MOE_KC), :]
        else:
            src = wdn_hbm.at[e, pl.ds((c - N_GU_CH) * MOE_KC, MOE_KC), :]
        return pltpu.make_async_copy(src, ring.at[c], sems.at[c])

    @pl.when(i == 0)
    def _():
        for c in range(N_CH):
            chunk_copy(used_e[0], c).start()

    def unpack_x():
        packed = xs_ref[...]
        x_lo = lax.bitcast_convert_type(packed << 16, f32).astype(bf16)
        x_hi = lax.bitcast_convert_type(packed & jnp.uint32(0xFFFF0000), f32).astype(bf16)
        return x_lo, x_hi

    def swiglu(gu):
        g = gu[:, :D_EXPERT]
        return (g * jax.nn.sigmoid(g) * gu[:, D_EXPERT:]).astype(bf16)

    def rows(c):
        return slice(c * MOE_KC, (c + 1) * MOE_KC)

    @pl.when(active & first)
    def _():
        e_next = used_e[jnp.minimum(u + 1, meta[1] - 1)]

        def take(c):
            chunk_copy(e_next, c).wait()
            wb = ring[c].astype(bf16)
            if c < N_GU_CH:
                wgu_b[rows(c), :] = wb
            else:
                wdn_b[rows(c - N_GU_CH), :] = wb
            chunk_copy(e_next, c).start()

        x_lo, x_hi = unpack_x()
        per_half = N_GU_CH // 2
        x_chunks = [xh[:, rows(c)] for xh in (x_lo, x_hi) for c in range(per_half)]
        take(0)
        gu = None
        for c in range(N_GU_CH):
            take(c + 1)
            d = _dot(x_chunks[c], wgu_b[rows(c), :])
            gu = d if gu is None else gu + d
        a = swiglu(gu)
        y = None
        for c in range(N_DN_CH):
            if c + 1 < N_DN_CH:
                take(N_GU_CH + c + 1)
            d = _dot(a[:, rows(c)], wdn_b[rows(c), :])
            y = d if y is None else y + d
        y_ref[...] = y

    @pl.when(active & jnp.logical_not(first))
    def _():
        x_lo, x_hi = unpack_x()
        half = D_MODEL // 2
        gu = _dot(x_lo, wgu_b[0:half, :]) + _dot(x_hi, wgu_b[half:, :])
        y_ref[...] = _dot(swiglu(gu), wdn_b[...])

    @pl.when(i == pl.num_programs(0) - 1)
    def _():
        for c in range(N_CH):
            chunk_copy(used_e[0], c).wait()


def _moe(blk_u, used_e, meta, xs, w_gu, w_dn):
    tm = MOE_TM
    blk = lambda i, bu, ue, mt: (jnp.minimum(i, mt[0] - 1), 0)
    return pl.pallas_call(
        _moe_kernel,
        grid_spec=pltpu.PrefetchScalarGridSpec(
            num_scalar_prefetch=3,
            grid=(MOE_NBLK,),
            in_specs=[pl.BlockSpec((tm, D_MODEL // 2), blk),
                      pl.BlockSpec(memory_space=pl.ANY),
                      pl.BlockSpec(memory_space=pl.ANY)],
            out_specs=pl.BlockSpec((tm, D_MODEL), blk),
            scratch_shapes=[pltpu.VMEM((N_CH, MOE_KC, D_MODEL), f32),
                            pltpu.VMEM((D_MODEL, 2 * D_EXPERT), bf16),
                            pltpu.VMEM((D_EXPERT, D_MODEL), bf16),
                            pltpu.SemaphoreType.DMA((N_CH,))]),
        out_shape=jax.ShapeDtypeStruct((MOE_ROWS, D_MODEL), f32),
        compiler_params=_cparams(("arbitrary",)),
        name="moe",
    )(blk_u, used_e, meta, xs, w_gu, w_dn)


COMB_TM = 256


def _combine_kernel(pos_ref, x1_ref, gt_ref, wc_ref, g_ref, y_ref, o_ref, ybuf_even, ybuf_odd, sems):
    i = pl.program_id(0)
    last = pl.num_programs(0) - 1
    nxt = jnp.minimum(i + 1, last)
    bufs = (ybuf_even, ybuf_odd)

    def row_copy(tile, r, k, s):
        src = pos_ref[k * N_TOK + tile * COMB_TM + r]
        return pltpu.make_async_copy(y_ref.at[pl.ds(src, 1), :], bufs[s].at[k, pl.ds(r, 1), :], sems.at[s])

    def drain(s):
        def body(g, carry):
            for _ in range(DMA_UNROLL):
                row_copy(0, 0, 0, s).wait()
                row_copy(0, 0, 0, s).wait()
            return carry
        lax.fori_loop(0, COMB_TM // DMA_UNROLL, body, 0)

    @pl.when(i == 0)
    def _():
        def body(g, carry):
            for u in range(DMA_UNROLL):
                r = g * DMA_UNROLL + u
                row_copy(0, r, 0, 0).start()
                row_copy(0, r, 1, 0).start(priority=1)
            return carry
        lax.fori_loop(0, COMB_TM // DMA_UNROLL, body, 0)

    def step(s):
        drain(s)
        for r in range(COMB_TM):
            row_copy(nxt, r, 0, 1 - s).start()
            row_copy(nxt, r, 1, 1 - s).start(priority=1)
        ffn = wc_ref[:, 0:1] * bufs[s][0] + wc_ref[:, 1:2] * bufs[s][1]
        xo = x1_ref[...] + gt_ref[...] * ffn
        ms = jnp.mean(xo * xo, axis=-1, keepdims=True)
        o_ref[...] = xo * lax.rsqrt(ms + RMS_EPS) * g_ref[...]

        @pl.when(i == last)
        def _():
            drain(1 - s)

    @pl.when(i % 2 == 0)
    def _():
        step(0)

    @pl.when(i % 2 == 1)
    def _():
        step(1)


def _combine(pos_flat, x1, mod3, w_col, g_final, y):
    tm = COMB_TM
    per_b = SEQ // tm
    return pl.pallas_call(
        _combine_kernel,
        grid_spec=pltpu.PrefetchScalarGridSpec(
            num_scalar_prefetch=1,
            grid=(N_TOK // tm,),
            in_specs=[pl.BlockSpec((tm, D_MODEL), lambda i, pos: (i, 0)),
                      pl.BlockSpec((None, 1, D_MODEL), lambda i, pos: (i // per_b, 0, 5)),
                      pl.BlockSpec((tm, 8), lambda i, pos: (i, 0)),
                      pl.BlockSpec((1, D_MODEL), lambda i, pos: (0, 0)),
                      pl.BlockSpec(memory_space=pl.ANY)],
            out_specs=pl.BlockSpec((tm, D_MODEL), lambda i, pos: (i, 0)),
            scratch_shapes=[pltpu.VMEM((TOP_K, tm, D_MODEL), f32),
                            pltpu.VMEM((TOP_K, tm, D_MODEL), f32),
                            pltpu.SemaphoreType.DMA((2,))]),
        out_shape=jax.ShapeDtypeStruct((N_TOK, D_MODEL), f32),
        compiler_params=_cparams(("arbitrary",)),
        name="combine",
    )(pos_flat, x1, mod3, w_col, g_final, y)


def kernel(x, c, positions, w_ada, b_ada, g_mix, w_in, swa_sinks, diff_lambda_q1, diff_lambda_k1, diff_lambda_q2, diff_lambda_k2, diff_subln_g, w_branch_a, w_branch_b, w_out, g_ffn, w_router_group, b_router_group, w_router_expert, b_router_expert, w_exp_gate_up, w_exp_down, g_final):
    layer = 0
    x2 = x.reshape(N_TOK, D_MODEL)

    mod = _ada(c, w_ada[layer], b_ada[layer])
    mod3 = mod.reshape(BATCH, 1, 6 * D_MODEL)

    w_bf = w_in[layer].astype(bf16)
    s_qa = SWA_Q_HEADS * HEAD_DIM
    kv_heads = [w_bf[:, s_qa + h * HEAD_DIM:s_qa + (h + 1) * HEAD_DIM] for h in range(2 * SWA_KV_HEADS)]
    w_kv = jnp.concatenate([h for h in kv_heads for _ in range(2)], axis=1)
    half = HEAD_DIM // 2
    inv_freq = ROPE_THETA ** (-jnp.arange(0, HEAD_DIM, 2, dtype=f32) / HEAD_DIM)
    invf = jnp.tile(inv_freq, LANES // half).reshape(1, LANES)
    pos4 = jnp.repeat(positions.reshape(N_TOK // 4, 4).astype(i32), half, axis=1)
    cos4, sin4 = _rope_tables(pos4, invf)
    cos_t = jnp.tile(cos4.reshape(N_TOK, half), (1, LANES // half))
    sin_t = jnp.tile(sin4.reshape(N_TOK, half), (1, LANES // half))

    proj, proj_kv = _proj(x2, mod3, g_mix[layer].reshape(1, D_MODEL), cos_t, sin_t, w_bf, w_kv)
    o_a = _swa(swa_sinks[layer], proj, proj_kv)
    o_b = _diff(diff_lambda_q1[layer].reshape(1, HEAD_DIM), diff_lambda_k1[layer].reshape(1, HEAD_DIM),
                diff_lambda_q2[layer].reshape(1, HEAD_DIM), diff_lambda_k2[layer].reshape(1, HEAD_DIM),
                diff_subln_g[layer].reshape(LANES, 1), proj)

    wr = jnp.concatenate([w_router_group[layer], w_router_expert[layer]], axis=1)
    wr_t = jnp.pad(wr.T, ((0, ROUTER_ROWS - wr.shape[1]), (0, 0)))
    br = jnp.concatenate([b_router_group[layer], b_router_expert[layer]])
    br_col = jnp.pad(br, (0, ROUTER_ROWS - br.shape[0])).reshape(ROUTER_ROWS, 1)
    x1, hp, logits_t = _merge(o_a, o_b, proj, x2, mod3, g_ffn[layer].reshape(1, D_MODEL),
                              w_branch_a[layer].astype(bf16), w_branch_b[layer].astype(bf16),
                              w_out[layer].astype(bf16), wr_t, br_col)

    ints, wts, counts = _route(logits_t)
    cnt = counts[:, 0]
    pos, blk_u, used_e, meta, pstart = _plan(cnt, ints)
    pos_flat = pos[0:TOP_K].reshape(TOP_K * N_TOK)
    xs = _dispatch(pos_flat, cnt, pstart, hp)
    y = _moe(blk_u, used_e, meta, xs, w_exp_gate_up[layer], w_exp_down[layer])
    out = _combine(pos_flat, x1, mod3, wts.T, g_final.reshape(1, D_MODEL), y)
    return out.reshape(BATCH, SEQ, D_MODEL)
```

```python
import math

import jax
import jax.numpy as jnp
from jax import lax
from jax.experimental import pallas as pl
from jax.experimental.pallas import tpu as pltpu

D_MODEL = 2048
BATCH = 8
SEQ = 2048
N_TOK = BATCH * SEQ
HEAD_DIM = 64
ROPE_THETA = 10000.0
RMS_EPS = 1e-6
NEG_INF = -1e30
SWA_Q_HEADS = 16
SWA_KV_HEADS = 2
SWA_GROUP = SWA_Q_HEADS // SWA_KV_HEADS
WINDOW = 128
DIFF_HEADS = 8
N_GROUPS = 8
EXPERTS_PER_GROUP = 8
N_EXPERTS = N_GROUPS * EXPERTS_PER_GROUP
TOP_K = 2
D_EXPERT = D_MODEL // 2
LAMBDA_INIT = 0.8 - 0.6 * math.exp(-0.3 * 0)
LOG2_E = math.log2(math.e)

LANES = 128
VMEM_LIMIT = 56 * 1024 * 1024

PROJ_TN = 1024
PROJ_Q = 256
PROJ_MAIN_TILES = 8
PROJ_MAIN_COLS = PROJ_MAIN_TILES * PROJ_TN
PROJ_KV_COLS = 512
COL_QA, COL_QD, COL_KD, COL_VD, COL_GA, COL_GB = 0, 1024, 2048, 3072, 4096, 6144

MOE_TM = 256
MOE_ROWS = N_TOK * TOP_K + N_EXPERTS * MOE_TM
MOE_NBLK = MOE_ROWS // MOE_TM
ROUTE_T = 2048
ROUTER_ROWS = 128

f32 = jnp.float32
bf16 = jnp.bfloat16
i32 = jnp.int32


def _cparams(sem, vmem=VMEM_LIMIT):
    return pltpu.CompilerParams(dimension_semantics=sem, vmem_limit_bytes=vmem)


def _dot(a, b):
    return jnp.dot(a, b, preferred_element_type=f32)


def _dot_nt(a, b):
    return lax.dot_general(a, b, (((1,), (1,)), ((), ())), preferred_element_type=f32)


def _ada_kernel(c_ref, w_ref, b_ref, o_ref):
    c = c_ref[...]
    ca = (c * jax.nn.sigmoid(c)).astype(bf16)
    o_ref[...] = _dot(ca, w_ref[...].astype(bf16)) + b_ref[...]


def _ada(c, w_ada, b_ada):
    tn = 512
    n = w_ada.shape[1]
    return pl.pallas_call(
        _ada_kernel,
        grid=(n // tn,),
        in_specs=[pl.BlockSpec((BATCH, D_MODEL), lambda j: (0, 0)),
                  pl.BlockSpec((D_MODEL, tn), lambda j: (0, j)),
                  pl.BlockSpec((1, tn), lambda j: (0, j))],
        out_specs=pl.BlockSpec((BATCH, tn), lambda j: (0, j)),
        out_shape=jax.ShapeDtypeStruct((BATCH, n), f32),
        compiler_params=_cparams(("arbitrary",)),
        name="ada",
    )(c, w_ada, b_ada.reshape(1, n))


PROJ_TM = 1024


def _rope128(xc, cos, sin_signed, first_half):
    r_prev = pltpu.roll(xc, 32, 1)
    r_next = pltpu.roll(xc, 96, 1)
    return xc * cos + jnp.where(first_half, r_next, r_prev) * sin_signed


def _rope_table_kernel(pos_ref, invf_ref, cos_ref, sin_ref):
    ang = pos_ref[...].astype(f32) * invf_ref[...]
    cos_ref[...] = jnp.cos(ang)
    sin_ref[...] = jnp.sin(ang)


def _rope_tables(pos4, invf):
    rows = pos4.shape[0]
    tr = 512
    spec = pl.BlockSpec((tr, LANES), lambda i: (i, 0))
    return pl.pallas_call(
        _rope_table_kernel,
        grid=(rows // tr,),
        in_specs=[spec, pl.BlockSpec((1, LANES), lambda i: (0, 0))],
        out_specs=[spec, spec],
        out_shape=[jax.ShapeDtypeStruct((rows, LANES), f32)] * 2,
        compiler_params=_cparams(("arbitrary",)),
        name="rope_tables",
    )(pos4, invf)


def _proj_kernel(x_ref, sc_ref, sh_ref, g_ref, cos_ref, sin_ref, w0_ref, w1_ref, w2_ref, w3_ref, wkv_ref,
                 o_ref, okv_ref, h_scr, sin_scr):
    j = pl.program_id(1)
    main = (w0_ref, w1_ref, w2_ref, w3_ref)
    lane = lax.broadcasted_iota(i32, (1, LANES), 1)
    first_half = (lane % HEAD_DIM) < (HEAD_DIM // 2)

    @pl.when(j == 0)
    def _():
        rc = 256
        for r in range(PROJ_TM // rc):
            x = x_ref[r * rc:(r + 1) * rc, :]
            ms = jnp.mean(x * x, axis=-1, keepdims=True)
            y = x * lax.rsqrt(ms + RMS_EPS) * g_ref[...]
            h = y * (1.0 + sc_ref[...]) + sh_ref[...]
            h_scr[r * rc:(r + 1) * rc, :] = h.astype(bf16)
        sin_scr[...] = sin_ref[...] * jnp.where(first_half, -1.0, 1.0)

    qw = PROJ_Q

    def rope_q(out, acc, off, scale):
        for c in range(qw // LANES):
            xc = acc[:, c * LANES:(c + 1) * LANES]
            out[:, off + c * LANES:off + (c + 1) * LANES] = (
                _rope128(xc, cos_ref[...], sin_scr[...], first_half) * scale).astype(bf16)

    def cast_q(out, acc, off, scale):
        out[:, off:off + qw] = acc.astype(bf16)

    def tile(out, quarters, epilogues, scale=1.0):
        for s, (w, epilogue) in enumerate(zip(quarters, epilogues)):
            epilogue(out, _dot(h_scr[...], w[...]), s * qw, scale)

    @pl.when(j == 0)
    def _():
        tile(o_ref, main, (rope_q,) * 4, 1.0 / math.sqrt(HEAD_DIM))

    @pl.when(j == 1)
    def _():
        tile(o_ref, main, (rope_q,) * 4, LOG2_E / math.sqrt(HEAD_DIM))

    @pl.when(j == 2)
    def _():
        tile(o_ref, main, (rope_q,) * 4)

    @pl.when((j >= 3) & (j < PROJ_MAIN_TILES))
    def _():
        tile(o_ref, main, (cast_q,) * 4)

    @pl.when(j == PROJ_MAIN_TILES)
    def _():
        tile(okv_ref, (wkv_ref.at[:, 0:qw], wkv_ref.at[:, qw:2 * qw]), (rope_q, cast_q))


def _proj(x2, mod3, g_mix, cos_t, sin_t, w_bf, w_kv):
    tm = PROJ_TM
    per_b = SEQ // tm
    per_tile = PROJ_TN // PROJ_Q
    kv_blocks = 2 * SWA_KV_HEADS * HEAD_DIM // PROJ_Q

    def wcol(quarter):
        def index(i, j):
            jj = jnp.minimum(j, PROJ_MAIN_TILES - 1)
            return 0, per_tile * jj + quarter + jnp.where(jj >= SWA_Q_HEADS * HEAD_DIM // PROJ_TN, kv_blocks, 0)
        return index

    main_blk = lambda i, j: (i, jnp.minimum(j, PROJ_MAIN_TILES - 1))
    return pl.pallas_call(
        _proj_kernel,
        grid=(N_TOK // tm, PROJ_MAIN_TILES + 1),
        in_specs=[pl.BlockSpec((tm, D_MODEL), lambda i, j: (i, 0)),
                  pl.BlockSpec((None, 1, D_MODEL), lambda i, j: (i // per_b, 0, 1)),
                  pl.BlockSpec((None, 1, D_MODEL), lambda i, j: (i // per_b, 0, 0)),
                  pl.BlockSpec((1, D_MODEL), lambda i, j: (0, 0)),
                  pl.BlockSpec((tm, LANES), lambda i, j: (i, 0)),
                  pl.BlockSpec((tm, LANES), lambda i, j: (i, 0))]
                 + [pl.BlockSpec((D_MODEL, PROJ_Q), wcol(q)) for q in range(per_tile)]
                 + [pl.BlockSpec((D_MODEL, PROJ_KV_COLS), lambda i, j: (0, 0))],
        out_specs=[pl.BlockSpec((tm, PROJ_TN), main_blk),
                   pl.BlockSpec((tm, PROJ_KV_COLS), lambda i, j: (i, 0))],
        out_shape=[jax.ShapeDtypeStruct((N_TOK, PROJ_MAIN_COLS), bf16),
                   jax.ShapeDtypeStruct((N_TOK, PROJ_KV_COLS), bf16)],
        scratch_shapes=[pltpu.VMEM((tm, D_MODEL), bf16),
                        pltpu.VMEM((tm, LANES), f32)],
        compiler_params=_cparams(("arbitrary", "arbitrary")),
        name="proj",
    )(x2, mod3, mod3, g_mix, cos_t, sin_t, w_bf, w_bf, w_bf, w_bf, w_kv)


def _swa_kernel(sink_ref, q_ref, kv_ref, kvp_ref, o_ref):
    n = pl.program_id(1)
    w = WINDOW
    qi = lax.broadcasted_iota(i32, (w, 2 * w), 0)
    kj = lax.broadcasted_iota(i32, (w, 2 * w), 1)
    rel = qi + w - kj
    first_key = jnp.where(n > 0, 0, w)
    valid = (rel >= 0) & (rel < w) & (kj >= first_key)
    lane = lax.broadcasted_iota(i32, (w, LANES), 1)
    low = lane < HEAD_DIM
    pairs = SWA_GROUP // 2
    for kvh in range(SWA_KV_HEADS):
        ks = slice(kvh * LANES, (kvh + 1) * LANES)
        vs = slice((SWA_KV_HEADS + kvh) * LANES, (SWA_KV_HEADS + kvh + 1) * LANES)
        kk = jnp.concatenate([kvp_ref[:, ks], kv_ref[:, ks]], axis=0)
        vv = jnp.concatenate([kvp_ref[:, vs], kv_ref[:, vs]], axis=0)
        parts = []
        for pair in range(pairs):
            gp = kvh * pairs + pair
            q2 = q_ref[:, gp * LANES:(gp + 1) * LANES]
            zero = jnp.zeros_like(q2)
            parts += [jnp.where(low, q2, zero), jnp.where(low, zero, q2)]
        s_all = _dot_nt(jnp.concatenate(parts, axis=0), kk)
        ps, dens = [], []
        for g in range(SWA_GROUP):
            s = jnp.where(valid, s_all[g * w:(g + 1) * w, :], NEG_INF)
            sink = sink_ref[kvh * SWA_GROUP + g]
            m = jnp.maximum(jnp.max(s, axis=-1, keepdims=True), sink)
            p = jnp.exp(s - m)
            dens.append(jnp.sum(p, axis=-1, keepdims=True) + jnp.exp(sink - m))
            ps.append(p.astype(bf16))
        o_all = _dot(jnp.concatenate(ps, axis=0), vv)
        for pair in range(pairs):
            gp = kvh * pairs + pair
            o_even = o_all[(2 * pair) * w:(2 * pair + 1) * w, :] / dens[2 * pair]
            o_odd = o_all[(2 * pair + 1) * w:(2 * pair + 2) * w, :] / dens[2 * pair + 1]
            o_ref[:, gp * LANES:(gp + 1) * LANES] = jnp.where(low, o_even, o_odd).astype(bf16)


def _swa(sinks, proj, proj_kv):
    nb = SEQ // WINDOW
    return pl.pallas_call(
        _swa_kernel,
        grid=(BATCH, nb),
        in_specs=[pl.BlockSpec(memory_space=pltpu.SMEM),
                  pl.BlockSpec((WINDOW, SWA_Q_HEADS * HEAD_DIM), lambda b, n: (b * nb + n, 0)),
                  pl.BlockSpec((WINDOW, PROJ_KV_COLS), lambda b, n: (b * nb + n, 0)),
                  pl.BlockSpec((WINDOW, PROJ_KV_COLS), lambda b, n: (jnp.maximum(b * nb + n - 1, 0), 0))],
        out_specs=pl.BlockSpec((WINDOW, SWA_Q_HEADS * HEAD_DIM), lambda b, n: (b * nb + n, 0)),
        out_shape=jax.ShapeDtypeStruct((N_TOK, SWA_Q_HEADS * HEAD_DIM), bf16),
        compiler_params=_cparams(("arbitrary", "arbitrary")),
        name="swa",
    )(sinks, proj, proj_kv, proj_kv)


DIFF_T = 256


def _diff_kernel(lq1_ref, lk1_ref, lq2_ref, lk2_ref, gcol_ref, q_ref, k_ref, v_ref, o_ref, vt_scr):
    t = DIFF_T
    nt = SEQ // t
    for j in range(nt):
        vt_scr[:, j * t:(j + 1) * t] = v_ref[j * t:(j + 1) * t, :].T
    lam = (jnp.exp(jnp.sum(lq1_ref[...] * lk1_ref[...], axis=-1, keepdims=True))
           - jnp.exp(jnp.sum(lq2_ref[...] * lk2_ref[...], axis=-1, keepdims=True)) + LAMBDA_INIT)
    lane = lax.broadcasted_iota(i32, (t, LANES), 1)
    causal = lax.broadcasted_iota(i32, (t, t), 0) <= lax.broadcasted_iota(i32, (t, t), 1)

    def one_map(qz, qi):
        lo = qi * t
        s_d = jnp.where(causal, _dot_nt(k_ref[lo:lo + t, :], qz), NEG_INF)
        m = jnp.max(s_d, axis=0, keepdims=True)
        if qi > 0:
            s_u = _dot_nt(k_ref[0:lo, :], qz)
            m = jnp.maximum(m, jnp.max(s_u, axis=0, keepdims=True))
        p_d = jnp.exp2(s_d - m)
        l = jnp.sum(p_d, axis=0, keepdims=True)
        p_u = None
        if qi > 0:
            p_u = jnp.exp2(s_u - m)
            l = l + jnp.sum(p_u, axis=0, keepdims=True)
        return p_d, p_u, l

    for qi in range(nt):
        lo = qi * t
        q = q_ref[lo:lo + t, :]
        zero = jnp.zeros_like(q)
        pd1, pu1, l1 = one_map(jnp.where(lane < HEAD_DIM, q, zero), qi)
        pd2, pu2, l2 = one_map(jnp.where(lane >= HEAD_DIM, q, zero), qi)
        c2 = lam * l1 / l2
        o = _dot(vt_scr[:, lo:lo + t], (pd1 - pd2 * c2).astype(bf16))
        if qi > 0:
            o = o + _dot(vt_scr[:, 0:lo], (pu1 - pu2 * c2).astype(bf16))
        o = o / l1
        y = o * lax.rsqrt(jnp.mean(o * o, axis=0, keepdims=True) + RMS_EPS) * gcol_ref[...]
        o_ref[qi * t:(qi + 1) * t, :] = (y * (1.0 - LAMBDA_INIT)).T.astype(bf16)


def _diff(lq1, lk1, lq2, lk2, subln_gcol, proj):
    vec = lambda n: pl.BlockSpec((1, n), lambda b, h: (0, 0))
    return pl.pallas_call(
        _diff_kernel,
        grid=(BATCH, DIFF_HEADS),
        in_specs=[vec(HEAD_DIM), vec(HEAD_DIM), vec(HEAD_DIM), vec(HEAD_DIM),
                  pl.BlockSpec((LANES, 1), lambda b, h: (0, 0)),
                  pl.BlockSpec((SEQ, LANES), lambda b, h: (b, COL_QD // LANES + h)),
                  pl.BlockSpec((SEQ, LANES), lambda b, h: (b, COL_KD // LANES + h)),
                  pl.BlockSpec((SEQ, LANES), lambda b, h: (b, COL_VD // LANES + h))],
        out_specs=pl.BlockSpec((SEQ, LANES), lambda b, h: (b, h)),
        out_shape=jax.ShapeDtypeStruct((N_TOK, DIFF_HEADS * LANES), bf16),
        scratch_shapes=[pltpu.VMEM((LANES, SEQ), bf16)],
        compiler_params=_cparams(("arbitrary", "arbitrary")),
        name="diff",
    )(lq1, lk1, lq2, lk2, subln_gcol, proj, proj, proj)


MERGE_TM = 256


def _merge_kernel(oa_ref, ob_ref, ga_ref, gb_ref, x_ref, gt_ref, sc_ref, sh_ref, g_ref,
                  wa_ref, wb_ref, wo_ref, wr_ref, br_ref, x1_ref, hp_ref, lt_ref, y_even, y_odd):
    i = pl.program_id(0)

    @pl.when(i == 0)
    def _():
        y_odd[...] = jnp.zeros_like(y_odd)

    def step(y_cur, y_prev):
        x1 = x_ref[...] + gt_ref[...] * y_prev[...]
        x1_ref[...] = x1
        ms = jnp.mean(x1 * x1, axis=-1, keepdims=True)
        h2 = (x1 * lax.rsqrt(ms + RMS_EPS) * g_ref[...]) * (1.0 + sc_ref[...]) + sh_ref[...]
        hb = lax.bitcast_convert_type(h2.astype(bf16).astype(f32), jnp.uint32)
        half = D_MODEL // 2
        hp_ref[...] = (hb[:, :half] >> 16) | (hb[:, half:] & jnp.uint32(0xFFFF0000))
        lt_ref[...] = _dot_nt(wr_ref[...].astype(bf16), h2.astype(bf16)) + br_ref[...]

        a = _dot(oa_ref[...], wa_ref[...])
        b = _dot(ob_ref[...], wb_ref[...])
        merged = (jax.nn.sigmoid(ga_ref[...].astype(f32)) * a + jax.nn.sigmoid(gb_ref[...].astype(f32)) * b)
        y_cur[...] = _dot(merged.astype(bf16), wo_ref[...])

    @pl.when(i % 2 == 0)
    def _():
        step(y_even, y_odd)

    @pl.when(i % 2 == 1)
    def _():
        step(y_odd, y_even)


def _merge(o_a, o_b, proj, x2, mod3, g_ffn, wa, wb, wo, wr_t, br_col):
    tm = MERGE_TM
    per_b = SEQ // tm
    nt = N_TOK // tm
    cur = lambda i: jnp.minimum(i, nt - 1)
    prev = lambda i: jnp.maximum(i - 1, 0)
    const = lambda shape: pl.BlockSpec(shape, lambda i: (0,) * len(shape), pipeline_mode=pl.Buffered(1))
    modspec = lambda chunk: pl.BlockSpec((None, 1, D_MODEL), lambda i: (prev(i) // per_b, 0, chunk))
    return pl.pallas_call(
        _merge_kernel,
        grid=(nt + 1,),
        in_specs=[pl.BlockSpec((tm, 1024), lambda i: (cur(i), 0)),
                  pl.BlockSpec((tm, 1024), lambda i: (cur(i), 0)),
                  pl.BlockSpec((tm, D_MODEL), lambda i: (cur(i), COL_GA // D_MODEL)),
                  pl.BlockSpec((tm, D_MODEL), lambda i: (cur(i), COL_GB // D_MODEL)),
                  pl.BlockSpec((tm, D_MODEL), lambda i: (prev(i), 0)),
                  modspec(2), modspec(4), modspec(3),
                  const((1, D_MODEL)),
                  const((1024, D_MODEL)), const((1024, D_MODEL)), const((D_MODEL, D_MODEL)),
                  const((ROUTER_ROWS, D_MODEL)), const((ROUTER_ROWS, 1))],
        out_specs=[pl.BlockSpec((tm, D_MODEL), lambda i: (prev(i), 0)),
                   pl.BlockSpec((tm, D_MODEL // 2), lambda i: (prev(i), 0)),
                   pl.BlockSpec((ROUTER_ROWS, tm), lambda i: (0, prev(i)))],
        out_shape=[jax.ShapeDtypeStruct((N_TOK, D_MODEL), f32),
                   jax.ShapeDtypeStruct((N_TOK, D_MODEL // 2), jnp.uint32),
                   jax.ShapeDtypeStruct((ROUTER_ROWS, N_TOK), f32)],
        scratch_shapes=[pltpu.VMEM((tm, D_MODEL), f32), pltpu.VMEM((tm, D_MODEL), f32)],
        compiler_params=_cparams(("arbitrary",)),
        name="merge",
    )(o_a, o_b, proj, proj, x2, mod3, mod3, mod3, g_ffn, wa, wb, wo, wr_t, br_col)


def _first_argmax(v, iota_f):
    vmax = jnp.max(v, axis=0, keepdims=True)
    idx = jnp.min(jnp.where(v == vmax, iota_f, float(v.shape[0])), axis=0, keepdims=True)
    return vmax, idx


def _route_kernel(lt_ref, ints_ref, w_ref, cnt_ref, carry):
    @pl.when(pl.program_id(0) == 0)
    def _():
        carry[...] = jnp.zeros_like(carry)

    cw = 256
    io8 = lax.broadcasted_iota(i32, (N_GROUPS, cw), 0).astype(f32)
    io64 = lax.broadcasted_iota(i32, (N_EXPERTS, cw), 0).astype(f32)
    tri = jnp.where(lax.broadcasted_iota(i32, (cw, cw), 0) < lax.broadcasted_iota(i32, (cw, cw), 1),
                    1.0, 0.0).astype(bf16)
    ints_ref[4:8, :] = jnp.zeros((4, ROUTE_T), i32)
    w_ref[2:8, :] = jnp.zeros((6, ROUTE_T), f32)
    for c in range(ROUTE_T // cw):
        sl = slice(c * cw, (c + 1) * cw)
        gl = lt_ref[0:N_GROUPS, sl]
        gmax, gsel = _first_argmax(gl, io8)
        gw = 1.0 / jnp.sum(jnp.exp(gl - gmax), axis=0, keepdims=True)
        es = jnp.zeros((EXPERTS_PER_GROUP, cw), f32)
        for g in range(N_GROUPS):
            lo = N_GROUPS + g * EXPERTS_PER_GROUP
            es = jnp.where(gsel == float(g), lt_ref[lo:lo + EXPERTS_PER_GROUP, sl], es)
        v1, i1 = _first_argmax(es, io8)
        es2 = jnp.where(io8 == i1, -jnp.inf, es)
        v2, i2 = _first_argmax(es2, io8)
        e = jnp.exp(v2 - v1)
        w_ref[0:1, sl] = gw / (1.0 + e)
        w_ref[1:2, sl] = gw * e / (1.0 + e)
        e1 = gsel * float(EXPERTS_PER_GROUP) + i1
        e2 = gsel * float(EXPERTS_PER_GROUP) + i2
        oh1 = io64 == e1
        oh2 = io64 == e2
        oc = jnp.where(oh1, 1.0, jnp.where(oh2, 1.0, 0.0))
        pre = _dot(oc.astype(bf16), tri) + carry[...]
        ints_ref[0:1, sl] = e1.astype(i32)
        ints_ref[1:2, sl] = e2.astype(i32)
        ints_ref[2:3, sl] = jnp.sum(jnp.where(oh1, pre, 0.0), axis=0, keepdims=True).astype(i32)
        ints_ref[3:4, sl] = jnp.sum(jnp.where(oh2, pre, 0.0), axis=0, keepdims=True).astype(i32)
        carry[...] = carry[...] + jnp.sum(oc, axis=1, keepdims=True)
    cnt_ref[...] = jnp.broadcast_to(carry[...], (N_EXPERTS, LANES)).astype(i32)


def _route(logits_t):
    t = ROUTE_T
    return pl.pallas_call(
        _route_kernel,
        grid=(N_TOK // t,),
        in_specs=[pl.BlockSpec((ROUTER_ROWS, t), lambda c: (0, c))],
        out_specs=[pl.BlockSpec((8, t), lambda c: (0, c)),
                   pl.BlockSpec((8, t), lambda c: (0, c)),
                   pl.BlockSpec((N_EXPERTS, LANES), lambda c: (0, 0))],
        out_shape=[jax.ShapeDtypeStruct((8, N_TOK), i32),
                   jax.ShapeDtypeStruct((8, N_TOK), f32),
                   jax.ShapeDtypeStruct((N_EXPERTS, LANES), i32)],
        scratch_shapes=[pltpu.VMEM((N_EXPERTS, 1), f32)],
        compiler_params=_cparams(("arbitrary",)),
        name="route",
    )(logits_t)


def _plan_kernel(cnt_ref, ints_ref, pos_ref, blk_u_ref, used_e_ref, meta_ref, pstart_s):
    @pl.when(pl.program_id(0) == 0)
    def _():
        def per_expert(e, carry):
            b, u = carry
            nb = (cnt_ref[e] + (MOE_TM - 1)) // MOE_TM
            pstart_s[e] = b * MOE_TM
            used_e_ref[u] = e

            def mark(k, c):
                blk_u_ref[b + k] = u
                return c

            lax.fori_loop(0, nb, mark, 0)
            return b + nb, u + jnp.minimum(nb, 1)

        nb_total, nu = lax.fori_loop(0, N_EXPERTS, per_expert, (0, 0))
        meta_ref[0] = nb_total
        meta_ref[1] = nu
        last_e = used_e_ref[nu - 1]

        def fill_e(k, c):
            used_e_ref[k] = last_e
            return c

        lax.fori_loop(nu, N_EXPERTS, fill_e, 0)

        def fill_b(k, c):
            blk_u_ref[k] = nu - 1
            return c

        lax.fori_loop(nb_total, MOE_NBLK, fill_b, 0)

    t = ROUTE_T
    iocol = lax.broadcasted_iota(i32, (N_EXPERTS, 1), 0)

    def build(e, col):
        return jnp.where(iocol == e, pstart_s[e], col)

    pcol = lax.fori_loop(0, N_EXPERTS, build, jnp.zeros((N_EXPERTS, 1), i32)).astype(f32)
    io64 = lax.broadcasted_iota(i32, (N_EXPERTS, t), 0)
    e1 = ints_ref[0:1, :]
    e2 = ints_ref[1:2, :]
    pos_ref[0:1, :] = (jnp.sum(jnp.where(io64 == e1, pcol, 0.0), axis=0, keepdims=True).astype(i32)
                       + ints_ref[2:3, :])
    pos_ref[1:2, :] = (jnp.sum(jnp.where(io64 == e2, pcol, 0.0), axis=0, keepdims=True).astype(i32)
                       + ints_ref[3:4, :])
    pos_ref[2:8, :] = jnp.zeros((6, t), i32)


def _plan(counts, ints):
    t = ROUTE_T
    return pl.pallas_call(
        _plan_kernel,
        grid_spec=pltpu.PrefetchScalarGridSpec(
            num_scalar_prefetch=1,
            grid=(N_TOK // t,),
            in_specs=[pl.BlockSpec((8, t), lambda c, cnt: (0, c))],
            out_specs=[pl.BlockSpec((8, t), lambda c, cnt: (0, c)),
                       pl.BlockSpec(memory_space=pltpu.SMEM),
                       pl.BlockSpec(memory_space=pltpu.SMEM),
                       pl.BlockSpec(memory_space=pltpu.SMEM),
                       pl.BlockSpec(memory_space=pltpu.SMEM)]),
        out_shape=[jax.ShapeDtypeStruct((8, N_TOK), i32),
                   jax.ShapeDtypeStruct((MOE_NBLK,), i32),
                   jax.ShapeDtypeStruct((N_EXPERTS,), i32),
                   jax.ShapeDtypeStruct((2,), i32),
                   jax.ShapeDtypeStruct((N_EXPERTS,), i32)],
        compiler_params=_cparams(("arbitrary",)),
        name="plan",
    )(counts, ints)


DISP_CHUNK = 512
DMA_UNROLL = 8


PAD_ZERO_ROWS = 128


def _dispatch_kernel(pos_ref, cnt_ref, pstart_ref, hp_ref, xs_ref, zbuf, sem, zsem):
    base = pl.program_id(0) * DISP_CHUNK

    def zero_pads(wait):
        def go(copy, cond):
            @pl.when(cond)
            def _():
                copy.wait() if wait else copy.start()

        def per_expert(e, carry):
            n = cnt_ref[e]
            first = pstart_ref[e] + n
            pad = (-n) & (MOE_TM - 1)
            lead = (-first) & 7
            for k in range(7):
                go(pltpu.make_async_copy(zbuf.at[pl.ds(0, 1), :], xs_ref.at[pl.ds(first + k, 1), :], zsem),
                   (k < lead) & (k < pad))
            off = first + lead
            rest = pad - lead
            size = PAD_ZERO_ROWS
            while size >= 8:
                go(pltpu.make_async_copy(zbuf.at[pl.ds(0, size), :],
                                         xs_ref.at[pl.ds(pl.multiple_of(off, 8), size), :], zsem),
                   (rest & size) != 0)
                off = off + (rest & size)
                size //= 2
            return carry

        lax.fori_loop(0, N_EXPERTS, per_expert, 0)

    @pl.when(pl.program_id(0) == 0)
    def _():
        zbuf[...] = jnp.zeros_like(zbuf)
        zero_pads(wait=False)
        zero_pads(wait=True)

    def row_copy(r, k):
        dst = pos_ref[k * N_TOK + base + r]
        return pltpu.make_async_copy(hp_ref.at[pl.ds(r, 1), :], xs_ref.at[pl.ds(dst, 1), :], sem)

    for r in range(DISP_CHUNK):
        row_copy(r, 0).start()
        row_copy(r, 1).start()

    def drain(g, carry):
        for _ in range(DMA_UNROLL):
            row_copy(0, 0).wait()
            row_copy(0, 0).wait()
        return carry

    lax.fori_loop(0, DISP_CHUNK // DMA_UNROLL, drain, 0)


def _dispatch(pos_flat, counts, pstart, hp):
    return pl.pallas_call(
        _dispatch_kernel,
        grid_spec=pltpu.PrefetchScalarGridSpec(
            num_scalar_prefetch=3,
            grid=(N_TOK // DISP_CHUNK,),
            in_specs=[pl.BlockSpec((DISP_CHUNK, D_MODEL // 2), lambda i, pos, cnt, ps: (i, 0))],
            out_specs=pl.BlockSpec(memory_space=pl.ANY),
            scratch_shapes=[pltpu.VMEM((PAD_ZERO_ROWS, D_MODEL // 2), jnp.uint32),
                            pltpu.SemaphoreType.DMA(()),
                            pltpu.SemaphoreType.DMA(())]),
        out_shape=jax.ShapeDtypeStruct((MOE_ROWS, D_MODEL // 2), jnp.uint32),
        compiler_params=_cparams(("arbitrary",)),
        name="dispatch",
    )(pos_flat, counts, pstart, hp)


MOE_KC = 512
N_GU_CH = D_MODEL // MOE_KC
N_DN_CH = D_EXPERT // MOE_KC
N_CH = N_GU_CH + N_DN_CH


def _moe_kernel(blk_u, used_e, meta, xs_ref, wgu_hbm, wdn_hbm, y_ref, ring, wgu_b, wdn_b, sems):
    i = pl.program_id(0)
    u = blk_u[i]
    first = (i == 0) | (u != blk_u[jnp.maximum(i - 1, 0)])
    active = i < meta[0]

    def chunk_copy(e, c):
        if c < N_GU_CH:
            src = wgu_hbm.at[e, pl.ds(c * MOE_KC, MOE_KC), :]
        else:
            src = wdn_hbm.at[e, pl.ds((c - N_GU_CH) * MOE_KC, MOE_KC), :]
        return pltpu.make_async_copy(src, ring.at[c], sems.at[c])

    @pl.when(i == 0)
    def _():
        for c in range(N_CH):
            chunk_copy(used_e[0], c).start()

    def unpack_x():
        packed = xs_ref[...]
        x_lo = lax.bitcast_convert_type(packed << 16, f32).astype(bf16)
        x_hi = lax.bitcast_convert_type(packed & jnp.uint32(0xFFFF0000), f32).astype(bf16)
        return x_lo, x_hi

    def swiglu(gu):
        g = gu[:, :D_EXPERT]
        return (g * jax.nn.sigmoid(g) * gu[:, D_EXPERT:]).astype(bf16)

    def rows(c):
        return slice(c * MOE_KC, (c + 1) * MOE_KC)

    @pl.when(active & first)
    def _():
        e_next = used_e[jnp.minimum(u + 1, meta[1] - 1)]

        def take(c):
            chunk_copy(e_next, c).wait()
            wb = ring[c].astype(bf16)
            if c < N_GU_CH:
                wgu_b[rows(c), :] = wb
            else:
                wdn_b[rows(c - N_GU_CH), :] = wb
            chunk_copy(e_next, c).start()

        x_lo, x_hi = unpack_x()
        per_half = N_GU_CH // 2
        x_chunks = [xh[:, rows(c)] for xh in (x_lo, x_hi) for c in range(per_half)]
        take(0)
        gu = None
        for c in range(N_GU_CH):
            take(c + 1)
            d = _dot(x_chunks[c], wgu_b[rows(c), :])
            gu = d if gu is None else gu + d
        a = swiglu(gu)
        y = None
        for c in range(N_DN_CH):
            if c + 1 < N_DN_CH:
                take(N_GU_CH + c + 1)
            d = _dot(a[:, rows(c)], wdn_b[rows(c), :])
            y = d if y is None else y + d
        y_ref[...] = y

    @pl.when(active & jnp.logical_not(first))
    def _():
        x_lo, x_hi = unpack_x()
        half = D_MODEL // 2
        gu = _dot(x_lo, wgu_b[0:half, :]) + _dot(x_hi, wgu_b[half:, :])
        y_ref[...] = _dot(swiglu(gu), wdn_b[...])

    @pl.when(i == pl.num_programs(0) - 1)
    def _():
        for c in range(N_CH):
            chunk_copy(used_e[0], c).wait()


def _moe(blk_u, used_e, meta, xs, w_gu, w_dn):
    tm = MOE_TM
    blk = lambda i, bu, ue, mt: (jnp.minimum(i, mt[0] - 1), 0)
    return pl.pallas_call(
        _moe_kernel,
        grid_spec=pltpu.PrefetchScalarGridSpec(
            num_scalar_prefetch=3,
            grid=(MOE_NBLK,),
            in_specs=[pl.BlockSpec((tm, D_MODEL // 2), blk),
                      pl.BlockSpec(memory_space=pl.ANY),
                      pl.BlockSpec(memory_space=pl.ANY)],
            out_specs=pl.BlockSpec((tm, D_MODEL), blk),
            scratch_shapes=[pltpu.VMEM((N_CH, MOE_KC, D_MODEL), f32),
                            pltpu.VMEM((D_MODEL, 2 * D_EXPERT), bf16),
                            pltpu.VMEM((D_EXPERT, D_MODEL), bf16),
                            pltpu.SemaphoreType.DMA((N_CH,))]),
        out_shape=jax.ShapeDtypeStruct((MOE_ROWS, D_MODEL), f32),
        compiler_params=_cparams(("arbitrary",)),
        name="moe",
    )(blk_u, used_e, meta, xs, w_gu, w_dn)


COMB_TM = 256


def _combine_kernel(pos_ref, x1_ref, gt_ref, wc_ref, g_ref, y_ref, o_ref, ybuf_even, ybuf_odd, sems):
    i = pl.program_id(0)
    last = pl.num_programs(0) - 1
    nxt = jnp.minimum(i + 1, last)
    bufs = (ybuf_even, ybuf_odd)

    def row_copy(tile, r, k, s):
        src = pos_ref[k * N_TOK + tile * COMB_TM + r]
        return pltpu.make_async_copy(y_ref.at[pl.ds(src, 1), :], bufs[s].at[k, pl.ds(r, 1), :], sems.at[s])

    def drain(s):
        def body(g, carry):
            for _ in range(DMA_UNROLL):
                row_copy(0, 0, 0, s).wait()
                row_copy(0, 0, 0, s).wait()
            return carry
        lax.fori_loop(0, COMB_TM // DMA_UNROLL, body, 0)

    @pl.when(i == 0)
    def _():
        def body(g, carry):
            for u in range(DMA_UNROLL):
                r = g * DMA_UNROLL + u
                row_copy(0, r, 0, 0).start()
                row_copy(0, r, 1, 0).start()
            return carry
        lax.fori_loop(0, COMB_TM // DMA_UNROLL, body, 0)

    def step(s):
        drain(s)
        for r in range(COMB_TM):
            row_copy(nxt, r, 0, 1 - s).start()
            row_copy(nxt, r, 1, 1 - s).start()
        ffn = wc_ref[:, 0:1] * bufs[s][0] + wc_ref[:, 1:2] * bufs[s][1]
        xo = x1_ref[...] + gt_ref[...] * ffn
        ms = jnp.mean(xo * xo, axis=-1, keepdims=True)
        o_ref[...] = xo * lax.rsqrt(ms + RMS_EPS) * g_ref[...]

        @pl.when(i == last)
        def _():
            drain(1 - s)

    @pl.when(i % 2 == 0)
    def _():
        step(0)

    @pl.when(i % 2 == 1)
    def _():
        step(1)


def _combine(pos_flat, x1, mod3, w_col, g_final, y):
    tm = COMB_TM
    per_b = SEQ // tm
    return pl.pallas_call(
        _combine_kernel,
        grid_spec=pltpu.PrefetchScalarGridSpec(
            num_scalar_prefetch=1,
            grid=(N_TOK // tm,),
            in_specs=[pl.BlockSpec((tm, D_MODEL), lambda i, pos: (i, 0)),
                      pl.BlockSpec((None, 1, D_MODEL), lambda i, pos: (i // per_b, 0, 5)),
                      pl.BlockSpec((tm, 8), lambda i, pos: (i, 0)),
                      pl.BlockSpec((1, D_MODEL), lambda i, pos: (0, 0)),
                      pl.BlockSpec(memory_space=pl.ANY)],
            out_specs=pl.BlockSpec((tm, D_MODEL), lambda i, pos: (i, 0)),
            scratch_shapes=[pltpu.VMEM((TOP_K, tm, D_MODEL), f32),
                            pltpu.VMEM((TOP_K, tm, D_MODEL), f32),
                            pltpu.SemaphoreType.DMA((2,))]),
        out_shape=jax.ShapeDtypeStruct((N_TOK, D_MODEL), f32),
        compiler_params=_cparams(("arbitrary",)),
        name="combine",
    )(pos_flat, x1, mod3, w_col, g_final, y)


def kernel(x, c, positions, w_ada, b_ada, g_mix, w_in, swa_sinks, diff_lambda_q1, diff_lambda_k1, diff_lambda_q2, diff_lambda_k2, diff_subln_g, w_branch_a, w_branch_b, w_out, g_ffn, w_router_group, b_router_group, w_router_expert, b_router_expert, w_exp_gate_up, w_exp_down, g_final):
    layer = 0
    x2 = x.reshape(N_TOK, D_MODEL)

    mod = _ada(c, w_ada[layer], b_ada[layer])
    mod3 = mod.reshape(BATCH, 1, 6 * D_MODEL)

    w_bf = w_in[layer].astype(bf16)
    s_qa = SWA_Q_HEADS * HEAD_DIM
    kv_heads = [w_bf[:, s_qa + h * HEAD_DIM:s_qa + (h + 1) * HEAD_DIM] for h in range(2 * SWA_KV_HEADS)]
    w_kv = jnp.concatenate([h for h in kv_heads for _ in range(2)], axis=1)
    half = HEAD_DIM // 2
    inv_freq = ROPE_THETA ** (-jnp.arange(0, HEAD_DIM, 2, dtype=f32) / HEAD_DIM)
    invf = jnp.tile(inv_freq, LANES // half).reshape(1, LANES)
    pos4 = jnp.repeat(positions.reshape(N_TOK // 4, 4).astype(i32), half, axis=1)
    cos4, sin4 = _rope_tables(pos4, invf)
    cos_t = jnp.tile(cos4.reshape(N_TOK, half), (1, LANES // half))
    sin_t = jnp.tile(sin4.reshape(N_TOK, half), (1, LANES // half))

    proj, proj_kv = _proj(x2, mod3, g_mix[layer].reshape(1, D_MODEL), cos_t, sin_t, w_bf, w_kv)
    o_a = _swa(swa_sinks[layer], proj, proj_kv)
    o_b = _diff(diff_lambda_q1[layer].reshape(1, HEAD_DIM), diff_lambda_k1[layer].reshape(1, HEAD_DIM),
                diff_lambda_q2[layer].reshape(1, HEAD_DIM), diff_lambda_k2[layer].reshape(1, HEAD_DIM),
                diff_subln_g[layer].reshape(LANES, 1), proj)

    wr = jnp.concatenate([w_router_group[layer], w_router_expert[layer]], axis=1)
    wr_t = jnp.pad(wr.T, ((0, ROUTER_ROWS - wr.shape[1]), (0, 0)))
    br = jnp.concatenate([b_router_group[layer], b_router_expert[layer]])
    br_col = jnp.pad(br, (0, ROUTER_ROWS - br.shape[0])).reshape(ROUTER_ROWS, 1)
    x1, hp, logits_t = _merge(o_a, o_b, proj, x2, mod3, g_ffn[layer].reshape(1, D_MODEL),
                              w_branch_a[layer].astype(bf16), w_branch_b[layer].astype(bf16),
                              w_out[layer].astype(bf16), wr_t, br_col)

    ints, wts, counts = _route(logits_t)
    cnt = counts[:, 0]
    pos, blk_u, used_e, meta, pstart = _plan(cnt, ints)
    pos_flat = pos[0:TOP_K].reshape(TOP_K * N_TOK)
    xs = _dispatch(pos_flat, cnt, pstart, hp)
    y = _moe(blk_u, used_e, meta, xs, w_exp_gate_up[layer], w_exp_down[layer])
    out = _combine(pos_flat, x1, mod3, wts.T, g_final.reshape(1, D_MODEL), y)
    return out.reshape(BATCH, SEQ, D_MODEL)
```

```python
import math

import jax
import jax.numpy as jnp
from jax import lax
from jax.experimental import pallas as pl
from jax.experimental.pallas import tpu as pltpu

D_MODEL = 2048
BATCH = 8
SEQ = 2048
N_TOK = BATCH * SEQ
HEAD_DIM = 64
ROPE_THETA = 10000.0
RMS_EPS = 1e-6
NEG_INF = -1e30
SWA_Q_HEADS = 16
SWA_KV_HEADS = 2
SWA_GROUP = SWA_Q_HEADS // SWA_KV_HEADS
WINDOW = 128
DIFF_HEADS = 8
N_GROUPS = 8
EXPERTS_PER_GROUP = 8
N_EXPERTS = N_GROUPS * EXPERTS_PER_GROUP
TOP_K = 2
D_EXPERT = D_MODEL // 2
LAMBDA_INIT = 0.8 - 0.6 * math.exp(-0.3 * 0)
LOG2_E = math.log2(math.e)

LANES = 128
VMEM_LIMIT = 56 * 1024 * 1024

PROJ_TN = 1024
PROJ_Q = 256
PROJ_MAIN_TILES = 8
PROJ_MAIN_COLS = PROJ_MAIN_TILES * PROJ_TN
PROJ_KV_COLS = 512
COL_QA, COL_QD, COL_KD, COL_VD, COL_GA, COL_GB = 0, 1024, 2048, 3072, 4096, 6144

MOE_TM = 256
MOE_ROWS = N_TOK * TOP_K + N_EXPERTS * MOE_TM
MOE_NBLK = MOE_ROWS // MOE_TM
ROUTE_T = 2048
ROUTER_ROWS = 128

f32 = jnp.float32
bf16 = jnp.bfloat16
i32 = jnp.int32


def _cparams(sem, vmem=VMEM_LIMIT):
    return pltpu.CompilerParams(dimension_semantics=sem, vmem_limit_bytes=vmem)


def _dot(a, b):
    return jnp.dot(a, b, preferred_element_type=f32)


def _dot_nt(a, b):
    return lax.dot_general(a, b, (((1,), (1,)), ((), ())), preferred_element_type=f32)


def _pack_bf16_pairs(v):
    bits = lax.bitcast_convert_type(v.astype(bf16).astype(f32), jnp.uint32)
    half = v.shape[1] // 2
    return (bits[:, :half] >> 16) | (bits[:, half:] & jnp.uint32(0xFFFF0000))


def _unpack_bf16_pairs(p):
    lo = lax.bitcast_convert_type(p << 16, f32)
    hi = lax.bitcast_convert_type(p & jnp.uint32(0xFFFF0000), f32)
    return jnp.concatenate([lo, hi], axis=1)


def _ada_kernel(c_ref, w_ref, b_ref, o_ref):
    c = c_ref[...]
    ca = (c * jax.nn.sigmoid(c)).astype(bf16)
    o_ref[...] = _dot(ca, w_ref[...].astype(bf16)) + b_ref[...]


def _ada(c, w_ada, b_ada):
    tn = 512
    n = w_ada.shape[1]
    return pl.pallas_call(
        _ada_kernel,
        grid=(n // tn,),
        in_specs=[pl.BlockSpec((BATCH, D_MODEL), lambda j: (0, 0)),
                  pl.BlockSpec((D_MODEL, tn), lambda j: (0, j)),
                  pl.BlockSpec((1, tn), lambda j: (0, j))],
        out_specs=pl.BlockSpec((BATCH, tn), lambda j: (0, j)),
        out_shape=jax.ShapeDtypeStruct((BATCH, n), f32),
        compiler_params=_cparams(("arbitrary",)),
        name="ada",
    )(c, w_ada, b_ada.reshape(1, n))


PROJ_TM = 1024


def _rope128(xc, cos, sin_signed, first_half):
    r_prev = pltpu.roll(xc, 32, 1)
    r_next = pltpu.roll(xc, 96, 1)
    return xc * cos + jnp.where(first_half, r_next, r_prev) * sin_signed


def _rope_table_kernel(pos_ref, invf_ref, cos_ref, sin_ref):
    ang = pos_ref[...].astype(f32) * invf_ref[...]
    cos_ref[...] = jnp.cos(ang)
    sin_ref[...] = jnp.sin(ang)


def _rope_tables(pos4, invf):
    rows = pos4.shape[0]
    tr = 512
    spec = pl.BlockSpec((tr, LANES), lambda i: (i, 0))
    return pl.pallas_call(
        _rope_table_kernel,
        grid=(rows // tr,),
        in_specs=[spec, pl.BlockSpec((1, LANES), lambda i: (0, 0))],
        out_specs=[spec, spec],
        out_shape=[jax.ShapeDtypeStruct((rows, LANES), f32)] * 2,
        compiler_params=_cparams(("arbitrary",)),
        name="rope_tables",
    )(pos4, invf)


def _proj_kernel(x_ref, sc_ref, sh_ref, g_ref, cos_ref, sin_ref, w0_ref, w1_ref, w2_ref, w3_ref, wkv_ref,
                 o_ref, okv_ref, h_scr, sin_scr):
    j = pl.program_id(1)
    main = (w0_ref, w1_ref, w2_ref, w3_ref)
    lane = lax.broadcasted_iota(i32, (1, LANES), 1)
    first_half = (lane % HEAD_DIM) < (HEAD_DIM // 2)

    @pl.when(j == 0)
    def _():
        rc = 256
        for r in range(PROJ_TM // rc):
            x = x_ref[r * rc:(r + 1) * rc, :]
            ms = jnp.mean(x * x, axis=-1, keepdims=True)
            y = x * lax.rsqrt(ms + RMS_EPS) * g_ref[...]
            h = y * (1.0 + sc_ref[...]) + sh_ref[...]
            h_scr[r * rc:(r + 1) * rc, :] = h.astype(bf16)
        sin_scr[...] = sin_ref[...] * jnp.where(first_half, -1.0, 1.0)

    qw = PROJ_Q

    def rope_q(out, acc, off, scale):
        for c in range(qw // LANES):
            xc = acc[:, c * LANES:(c + 1) * LANES]
            out[:, off + c * LANES:off + (c + 1) * LANES] = (
                _rope128(xc, cos_ref[...], sin_scr[...], first_half) * scale).astype(bf16)

    def cast_q(out, acc, off, scale):
        out[:, off:off + qw] = acc.astype(bf16)

    def tile(out, quarters, epilogues, scale=1.0):
        for s, (w, epilogue) in enumerate(zip(quarters, epilogues)):
            epilogue(out, _dot(h_scr[...], w[...]), s * qw, scale)

    @pl.when(j == 0)
    def _():
        tile(o_ref, main, (rope_q,) * 4, 1.0 / math.sqrt(HEAD_DIM))

    @pl.when(j == 1)
    def _():
        tile(o_ref, main, (rope_q,) * 4, LOG2_E / math.sqrt(HEAD_DIM))

    @pl.when(j == 2)
    def _():
        tile(o_ref, main, (rope_q,) * 4)

    @pl.when((j >= 3) & (j < PROJ_MAIN_TILES))
    def _():
        tile(o_ref, main, (cast_q,) * 4)

    @pl.when(j == PROJ_MAIN_TILES)
    def _():
        tile(okv_ref, (wkv_ref.at[:, 0:qw], wkv_ref.at[:, qw:2 * qw]), (rope_q, cast_q))


def _proj(x2, mod3, g_mix, cos_t, sin_t, w_bf, w_kv):
    tm = PROJ_TM
    per_b = SEQ // tm
    per_tile = PROJ_TN // PROJ_Q
    kv_blocks = 2 * SWA_KV_HEADS * HEAD_DIM // PROJ_Q

    def wcol(quarter):
        def index(i, j):
            jj = jnp.minimum(j, PROJ_MAIN_TILES - 1)
            return 0, per_tile * jj + quarter + jnp.where(jj >= SWA_Q_HEADS * HEAD_DIM // PROJ_TN, kv_blocks, 0)
        return index

    main_blk = lambda i, j: (i, jnp.minimum(j, PROJ_MAIN_TILES - 1))
    return pl.pallas_call(
        _proj_kernel,
        grid=(N_TOK // tm, PROJ_MAIN_TILES + 1),
        in_specs=[pl.BlockSpec((tm, D_MODEL), lambda i, j: (i, 0)),
                  pl.BlockSpec((None, 1, D_MODEL), lambda i, j: (i // per_b, 0, 1)),
                  pl.BlockSpec((None, 1, D_MODEL), lambda i, j: (i // per_b, 0, 0)),
                  pl.BlockSpec((1, D_MODEL), lambda i, j: (0, 0)),
                  pl.BlockSpec((tm, LANES), lambda i, j: (i, 0)),
                  pl.BlockSpec((tm, LANES), lambda i, j: (i, 0))]
                 + [pl.BlockSpec((D_MODEL, PROJ_Q), wcol(q)) for q in range(per_tile)]
                 + [pl.BlockSpec((D_MODEL, PROJ_KV_COLS), lambda i, j: (0, 0))],
        out_specs=[pl.BlockSpec((tm, PROJ_TN), main_blk),
                   pl.BlockSpec((tm, PROJ_KV_COLS), lambda i, j: (i, 0))],
        out_shape=[jax.ShapeDtypeStruct((N_TOK, PROJ_MAIN_COLS), bf16),
                   jax.ShapeDtypeStruct((N_TOK, PROJ_KV_COLS), bf16)],
        scratch_shapes=[pltpu.VMEM((tm, D_MODEL), bf16),
                        pltpu.VMEM((tm, LANES), f32)],
        compiler_params=_cparams(("arbitrary", "arbitrary")),
        name="proj",
    )(x2, mod3, mod3, g_mix, cos_t, sin_t, w_bf, w_bf, w_bf, w_bf, w_kv)


def _swa_kernel(sink_ref, q_ref, kv_ref, kvp_ref, o_ref):
    n = pl.program_id(1)
    w = WINDOW
    qi = lax.broadcasted_iota(i32, (w, 2 * w), 0)
    kj = lax.broadcasted_iota(i32, (w, 2 * w), 1)
    rel = qi + w - kj
    first_key = jnp.where(n > 0, 0, w)
    valid = (rel >= 0) & (rel < w) & (kj >= first_key)
    lane = lax.broadcasted_iota(i32, (w, LANES), 1)
    low = lane < HEAD_DIM
    pairs = SWA_GROUP // 2
    for kvh in range(SWA_KV_HEADS):
        ks = slice(kvh * LANES, (kvh + 1) * LANES)
        vs = slice((SWA_KV_HEADS + kvh) * LANES, (SWA_KV_HEADS + kvh + 1) * LANES)
        kk = jnp.concatenate([kvp_ref[:, ks], kv_ref[:, ks]], axis=0)
        vv = jnp.concatenate([kvp_ref[:, vs], kv_ref[:, vs]], axis=0)
        parts = []
        for pair in range(pairs):
            gp = kvh * pairs + pair
            q2 = q_ref[:, gp * LANES:(gp + 1) * LANES]
            zero = jnp.zeros_like(q2)
            parts += [jnp.where(low, q2, zero), jnp.where(low, zero, q2)]
        s_all = _dot_nt(jnp.concatenate(parts, axis=0), kk)
        ps, dens = [], []
        for g in range(SWA_GROUP):
            s = jnp.where(valid, s_all[g * w:(g + 1) * w, :], NEG_INF)
            sink = sink_ref[kvh * SWA_GROUP + g]
            m = jnp.maximum(jnp.max(s, axis=-1, keepdims=True), sink)
            p = jnp.exp(s - m)
            dens.append(jnp.sum(p, axis=-1, keepdims=True) + jnp.exp(sink - m))
            ps.append(p.astype(bf16))
        o_all = _dot(jnp.concatenate(ps, axis=0), vv)
        for pair in range(pairs):
            gp = kvh * pairs + pair
            o_even = o_all[(2 * pair) * w:(2 * pair + 1) * w, :] / dens[2 * pair]
            o_odd = o_all[(2 * pair + 1) * w:(2 * pair + 2) * w, :] / dens[2 * pair + 1]
            o_ref[:, gp * LANES:(gp + 1) * LANES] = jnp.where(low, o_even, o_odd).astype(bf16)


def _swa(sinks, proj, proj_kv):
    nb = SEQ // WINDOW
    return pl.pallas_call(
        _swa_kernel,
        grid=(BATCH, nb),
        in_specs=[pl.BlockSpec(memory_space=pltpu.SMEM),
                  pl.BlockSpec((WINDOW, SWA_Q_HEADS * HEAD_DIM), lambda b, n: (b * nb + n, 0)),
                  pl.BlockSpec((WINDOW, PROJ_KV_COLS), lambda b, n: (b * nb + n, 0)),
                  pl.BlockSpec((WINDOW, PROJ_KV_COLS), lambda b, n: (jnp.maximum(b * nb + n - 1, 0), 0))],
        out_specs=pl.BlockSpec((WINDOW, SWA_Q_HEADS * HEAD_DIM), lambda b, n: (b * nb + n, 0)),
        out_shape=jax.ShapeDtypeStruct((N_TOK, SWA_Q_HEADS * HEAD_DIM), bf16),
        compiler_params=_cparams(("arbitrary", "arbitrary")),
        name="swa",
    )(sinks, proj, proj_kv, proj_kv)


DIFF_T = 256


def _diff_kernel(lq1_ref, lk1_ref, lq2_ref, lk2_ref, gcol_ref, q_ref, k_ref, v_ref, o_ref, vt_scr):
    t = DIFF_T
    nt = SEQ // t
    for j in range(nt):
        vt_scr[:, j * t:(j + 1) * t] = v_ref[j * t:(j + 1) * t, :].T
    lam = (jnp.exp(jnp.sum(lq1_ref[...] * lk1_ref[...], axis=-1, keepdims=True))
           - jnp.exp(jnp.sum(lq2_ref[...] * lk2_ref[...], axis=-1, keepdims=True)) + LAMBDA_INIT)
    lane = lax.broadcasted_iota(i32, (t, LANES), 1)
    causal = lax.broadcasted_iota(i32, (t, t), 0) <= lax.broadcasted_iota(i32, (t, t), 1)

    def one_map(qz, qi):
        lo = qi * t
        s_d = jnp.where(causal, _dot_nt(k_ref[lo:lo + t, :], qz), NEG_INF)
        m = jnp.max(s_d, axis=0, keepdims=True)
        if qi > 0:
            s_u = _dot_nt(k_ref[0:lo, :], qz)
            m = jnp.maximum(m, jnp.max(s_u, axis=0, keepdims=True))
        p_d = jnp.exp2(s_d - m)
        l = jnp.sum(p_d, axis=0, keepdims=True)
        p_u = None
        if qi > 0:
            p_u = jnp.exp2(s_u - m)
            l = l + jnp.sum(p_u, axis=0, keepdims=True)
        return p_d, p_u, l

    for qi in range(nt):
        lo = qi * t
        q = q_ref[lo:lo + t, :]
        zero = jnp.zeros_like(q)
        pd1, pu1, l1 = one_map(jnp.where(lane < HEAD_DIM, q, zero), qi)
        pd2, pu2, l2 = one_map(jnp.where(lane >= HEAD_DIM, q, zero), qi)
        c2 = lam * l1 / l2
        o = _dot(vt_scr[:, lo:lo + t], (pd1 - pd2 * c2).astype(bf16))
        if qi > 0:
            o = o + _dot(vt_scr[:, 0:lo], (pu1 - pu2 * c2).astype(bf16))
        o = o / l1
        y = o * lax.rsqrt(jnp.mean(o * o, axis=0, keepdims=True) + RMS_EPS) * gcol_ref[...]
        o_ref[qi * t:(qi + 1) * t, :] = (y * (1.0 - LAMBDA_INIT)).T.astype(bf16)


def _diff(lq1, lk1, lq2, lk2, subln_gcol, proj):
    vec = lambda n: pl.BlockSpec((1, n), lambda b, h: (0, 0))
    return pl.pallas_call(
        _diff_kernel,
        grid=(BATCH, DIFF_HEADS),
        in_specs=[vec(HEAD_DIM), vec(HEAD_DIM), vec(HEAD_DIM), vec(HEAD_DIM),
                  pl.BlockSpec((LANES, 1), lambda b, h: (0, 0)),
                  pl.BlockSpec((SEQ, LANES), lambda b, h: (b, COL_QD // LANES + h)),
                  pl.BlockSpec((SEQ, LANES), lambda b, h: (b, COL_KD // LANES + h)),
                  pl.BlockSpec((SEQ, LANES), lambda b, h: (b, COL_VD // LANES + h))],
        out_specs=pl.BlockSpec((SEQ, LANES), lambda b, h: (b, h)),
        out_shape=jax.ShapeDtypeStruct((N_TOK, DIFF_HEADS * LANES), bf16),
        scratch_shapes=[pltpu.VMEM((LANES, SEQ), bf16)],
        compiler_params=_cparams(("arbitrary", "arbitrary")),
        name="diff",
    )(lq1, lk1, lq2, lk2, subln_gcol, proj, proj, proj)


MERGE_TM = 256


def _merge_kernel(oa_ref, ob_ref, ga_ref, gb_ref, x_ref, gt_ref, sc_ref, sh_ref, g_ref,
                  wa_ref, wb_ref, wo_ref, wr_ref, br_ref, x1_ref, hp_ref, lt_ref, y_even, y_odd):
    i = pl.program_id(0)

    @pl.when(i == 0)
    def _():
        y_odd[...] = jnp.zeros_like(y_odd)

    def step(y_cur, y_prev):
        x1 = x_ref[...] + gt_ref[...] * y_prev[...]
        x1_ref[...] = x1
        ms = jnp.mean(x1 * x1, axis=-1, keepdims=True)
        h2 = (x1 * lax.rsqrt(ms + RMS_EPS) * g_ref[...]) * (1.0 + sc_ref[...]) + sh_ref[...]
        hp_ref[...] = _pack_bf16_pairs(h2)
        lt_ref[...] = _dot_nt(wr_ref[...].astype(bf16), h2.astype(bf16)) + br_ref[...]

        a = _dot(oa_ref[...], wa_ref[...])
        b = _dot(ob_ref[...], wb_ref[...])
        merged = (jax.nn.sigmoid(ga_ref[...].astype(f32)) * a + jax.nn.sigmoid(gb_ref[...].astype(f32)) * b)
        y_cur[...] = _dot(merged.astype(bf16), wo_ref[...])

    @pl.when(i % 2 == 0)
    def _():
        step(y_even, y_odd)

    @pl.when(i % 2 == 1)
    def _():
        step(y_odd, y_even)


def _merge(o_a, o_b, proj, x2, mod3, g_ffn, wa, wb, wo, wr_t, br_col):
    tm = MERGE_TM
    per_b = SEQ // tm
    nt = N_TOK // tm
    cur = lambda i: jnp.minimum(i, nt - 1)
    prev = lambda i: jnp.maximum(i - 1, 0)
    const = lambda shape: pl.BlockSpec(shape, lambda i: (0,) * len(shape), pipeline_mode=pl.Buffered(1))
    modspec = lambda chunk: pl.BlockSpec((None, 1, D_MODEL), lambda i: (prev(i) // per_b, 0, chunk))
    return pl.pallas_call(
        _merge_kernel,
        grid=(nt + 1,),
        in_specs=[pl.BlockSpec((tm, 1024), lambda i: (cur(i), 0)),
                  pl.BlockSpec((tm, 1024), lambda i: (cur(i), 0)),
                  pl.BlockSpec((tm, D_MODEL), lambda i: (cur(i), COL_GA // D_MODEL)),
                  pl.BlockSpec((tm, D_MODEL), lambda i: (cur(i), COL_GB // D_MODEL)),
                  pl.BlockSpec((tm, D_MODEL), lambda i: (prev(i), 0)),
                  modspec(2), modspec(4), modspec(3),
                  const((1, D_MODEL)),
                  const((1024, D_MODEL)), const((1024, D_MODEL)), const((D_MODEL, D_MODEL)),
                  const((ROUTER_ROWS, D_MODEL)), const((ROUTER_ROWS, 1))],
        out_specs=[pl.BlockSpec((tm, D_MODEL), lambda i: (prev(i), 0)),
                   pl.BlockSpec((tm, D_MODEL // 2), lambda i: (prev(i), 0)),
                   pl.BlockSpec((ROUTER_ROWS, tm), lambda i: (0, prev(i)))],
        out_shape=[jax.ShapeDtypeStruct((N_TOK, D_MODEL), f32),
                   jax.ShapeDtypeStruct((N_TOK, D_MODEL // 2), jnp.uint32),
                   jax.ShapeDtypeStruct((ROUTER_ROWS, N_TOK), f32)],
        scratch_shapes=[pltpu.VMEM((tm, D_MODEL), f32), pltpu.VMEM((tm, D_MODEL), f32)],
        compiler_params=_cparams(("arbitrary",)),
        name="merge",
    )(o_a, o_b, proj, proj, x2, mod3, mod3, mod3, g_ffn, wa, wb, wo, wr_t, br_col)


def _first_argmax(v, iota_f):
    vmax = jnp.max(v, axis=0, keepdims=True)
    idx = jnp.min(jnp.where(v == vmax, iota_f, float(v.shape[0])), axis=0, keepdims=True)
    return vmax, idx


def _route_kernel(lt_ref, ints_ref, w_ref, cnt_ref, carry):
    @pl.when(pl.program_id(0) == 0)
    def _():
        carry[...] = jnp.zeros_like(carry)

    cw = 256
    io8 = lax.broadcasted_iota(i32, (N_GROUPS, cw), 0).astype(f32)
    io64 = lax.broadcasted_iota(i32, (N_EXPERTS, cw), 0).astype(f32)
    tri = jnp.where(lax.broadcasted_iota(i32, (cw, cw), 0) < lax.broadcasted_iota(i32, (cw, cw), 1),
                    1.0, 0.0).astype(bf16)
    ints_ref[4:8, :] = jnp.zeros((4, ROUTE_T), i32)
    w_ref[2:8, :] = jnp.zeros((6, ROUTE_T), f32)
    for c in range(ROUTE_T // cw):
        sl = slice(c * cw, (c + 1) * cw)
        gl = lt_ref[0:N_GROUPS, sl]
        gmax, gsel = _first_argmax(gl, io8)
        gw = 1.0 / jnp.sum(jnp.exp(gl - gmax), axis=0, keepdims=True)
        es = jnp.zeros((EXPERTS_PER_GROUP, cw), f32)
        for g in range(N_GROUPS):
            lo = N_GROUPS + g * EXPERTS_PER_GROUP
            es = jnp.where(gsel == float(g), lt_ref[lo:lo + EXPERTS_PER_GROUP, sl], es)
        v1, i1 = _first_argmax(es, io8)
        es2 = jnp.where(io8 == i1, -jnp.inf, es)
        v2, i2 = _first_argmax(es2, io8)
        e = jnp.exp(v2 - v1)
        w_ref[0:1, sl] = gw / (1.0 + e)
        w_ref[1:2, sl] = gw * e / (1.0 + e)
        e1 = gsel * float(EXPERTS_PER_GROUP) + i1
        e2 = gsel * float(EXPERTS_PER_GROUP) + i2
        oh1 = io64 == e1
        oh2 = io64 == e2
        oc = jnp.where(oh1, 1.0, jnp.where(oh2, 1.0, 0.0))
        pre = _dot(oc.astype(bf16), tri) + carry[...]
        ints_ref[0:1, sl] = e1.astype(i32)
        ints_ref[1:2, sl] = e2.astype(i32)
        ints_ref[2:3, sl] = jnp.sum(jnp.where(oh1, pre, 0.0), axis=0, keepdims=True).astype(i32)
        ints_ref[3:4, sl] = jnp.sum(jnp.where(oh2, pre, 0.0), axis=0, keepdims=True).astype(i32)
        carry[...] = carry[...] + jnp.sum(oc, axis=1, keepdims=True)
    cnt_ref[...] = jnp.broadcast_to(carry[...], (N_EXPERTS, LANES)).astype(i32)


def _route(logits_t):
    t = ROUTE_T
    return pl.pallas_call(
        _route_kernel,
        grid=(N_TOK // t,),
        in_specs=[pl.BlockSpec((ROUTER_ROWS, t), lambda c: (0, c))],
        out_specs=[pl.BlockSpec((8, t), lambda c: (0, c)),
                   pl.BlockSpec((8, t), lambda c: (0, c)),
                   pl.BlockSpec((N_EXPERTS, LANES), lambda c: (0, 0))],
        out_shape=[jax.ShapeDtypeStruct((8, N_TOK), i32),
                   jax.ShapeDtypeStruct((8, N_TOK), f32),
                   jax.ShapeDtypeStruct((N_EXPERTS, LANES), i32)],
        scratch_shapes=[pltpu.VMEM((N_EXPERTS, 1), f32)],
        compiler_params=_cparams(("arbitrary",)),
        name="route",
    )(logits_t)


def _plan_kernel(cnt_ref, ints_ref, pos_ref, blk_u_ref, used_e_ref, meta_ref, pstart_s):
    @pl.when(pl.program_id(0) == 0)
    def _():
        def per_expert(e, carry):
            b, u = carry
            nb = (cnt_ref[e] + (MOE_TM - 1)) // MOE_TM
            pstart_s[e] = b * MOE_TM
            used_e_ref[u] = e

            def mark(k, c):
                blk_u_ref[b + k] = u
                return c

            lax.fori_loop(0, nb, mark, 0)
            return b + nb, u + jnp.minimum(nb, 1)

        nb_total, nu = lax.fori_loop(0, N_EXPERTS, per_expert, (0, 0))
        meta_ref[0] = nb_total
        meta_ref[1] = nu
        last_e = used_e_ref[nu - 1]

        def fill_e(k, c):
            used_e_ref[k] = last_e
            return c

        lax.fori_loop(nu, N_EXPERTS, fill_e, 0)

        def fill_b(k, c):
            blk_u_ref[k] = nu - 1
            return c

        lax.fori_loop(nb_total, MOE_NBLK, fill_b, 0)

    t = ROUTE_T
    iocol = lax.broadcasted_iota(i32, (N_EXPERTS, 1), 0)

    def build(e, col):
        return jnp.where(iocol == e, pstart_s[e], col)

    pcol = lax.fori_loop(0, N_EXPERTS, build, jnp.zeros((N_EXPERTS, 1), i32)).astype(f32)
    io64 = lax.broadcasted_iota(i32, (N_EXPERTS, t), 0)
    e1 = ints_ref[0:1, :]
    e2 = ints_ref[1:2, :]
    pos_ref[0:1, :] = (jnp.sum(jnp.where(io64 == e1, pcol, 0.0), axis=0, keepdims=True).astype(i32)
                       + ints_ref[2:3, :])
    pos_ref[1:2, :] = (jnp.sum(jnp.where(io64 == e2, pcol, 0.0), axis=0, keepdims=True).astype(i32)
                       + ints_ref[3:4, :])
    pos_ref[2:8, :] = jnp.zeros((6, t), i32)


def _plan(counts, ints):
    t = ROUTE_T
    return pl.pallas_call(
        _plan_kernel,
        grid_spec=pltpu.PrefetchScalarGridSpec(
            num_scalar_prefetch=1,
            grid=(N_TOK // t,),
            in_specs=[pl.BlockSpec((8, t), lambda c, cnt: (0, c))],
            out_specs=[pl.BlockSpec((8, t), lambda c, cnt: (0, c)),
                       pl.BlockSpec(memory_space=pltpu.SMEM),
                       pl.BlockSpec(memory_space=pltpu.SMEM),
                       pl.BlockSpec(memory_space=pltpu.SMEM),
                       pl.BlockSpec(memory_space=pltpu.SMEM)]),
        out_shape=[jax.ShapeDtypeStruct((8, N_TOK), i32),
                   jax.ShapeDtypeStruct((MOE_NBLK,), i32),
                   jax.ShapeDtypeStruct((N_EXPERTS,), i32),
                   jax.ShapeDtypeStruct((2,), i32),
                   jax.ShapeDtypeStruct((N_EXPERTS,), i32)],
        compiler_params=_cparams(("arbitrary",)),
        name="plan",
    )(counts, ints)


DISP_CHUNK = 512
DMA_UNROLL = 8


PAD_ZERO_ROWS = 128


def _dispatch_kernel(pos_ref, cnt_ref, pstart_ref, hp_ref, xs_ref, zbuf, sem, zsem):
    base = pl.program_id(0) * DISP_CHUNK

    def zero_pads(wait):
        def go(copy, cond):
            @pl.when(cond)
            def _():
                copy.wait() if wait else copy.start()

        def per_expert(e, carry):
            n = cnt_ref[e]
            first = pstart_ref[e] + n
            pad = (-n) & (MOE_TM - 1)
            lead = (-first) & 7
            for k in range(7):
                go(pltpu.make_async_copy(zbuf.at[pl.ds(0, 1), :], xs_ref.at[pl.ds(first + k, 1), :], zsem),
                   (k < lead) & (k < pad))
            off = first + lead
            rest = pad - lead
            size = PAD_ZERO_ROWS
            while size >= 8:
                go(pltpu.make_async_copy(zbuf.at[pl.ds(0, size), :],
                                         xs_ref.at[pl.ds(pl.multiple_of(off, 8), size), :], zsem),
                   (rest & size) != 0)
                off = off + (rest & size)
                size //= 2
            return carry

        lax.fori_loop(0, N_EXPERTS, per_expert, 0)

    @pl.when(pl.program_id(0) == 0)
    def _():
        zbuf[...] = jnp.zeros_like(zbuf)
        zero_pads(wait=False)
        zero_pads(wait=True)

    def row_copy(r, k):
        dst = pos_ref[k * N_TOK + base + r]
        return pltpu.make_async_copy(hp_ref.at[pl.ds(r, 1), :], xs_ref.at[pl.ds(dst, 1), :], sem)

    for r in range(DISP_CHUNK):
        row_copy(r, 0).start()
        row_copy(r, 1).start()

    def drain(g, carry):
        for _ in range(DMA_UNROLL):
            row_copy(0, 0).wait()
            row_copy(0, 0).wait()
        return carry

    lax.fori_loop(0, DISP_CHUNK // DMA_UNROLL, drain, 0)


def _dispatch(pos_flat, counts, pstart, hp):
    return pl.pallas_call(
        _dispatch_kernel,
        grid_spec=pltpu.PrefetchScalarGridSpec(
            num_scalar_prefetch=3,
            grid=(N_TOK // DISP_CHUNK,),
            in_specs=[pl.BlockSpec((DISP_CHUNK, D_MODEL // 2), lambda i, pos, cnt, ps: (i, 0))],
            out_specs=pl.BlockSpec(memory_space=pl.ANY),
            scratch_shapes=[pltpu.VMEM((PAD_ZERO_ROWS, D_MODEL // 2), jnp.uint32),
                            pltpu.SemaphoreType.DMA(()),
                            pltpu.SemaphoreType.DMA(())]),
        out_shape=jax.ShapeDtypeStruct((MOE_ROWS, D_MODEL // 2), jnp.uint32),
        compiler_params=_cparams(("arbitrary",)),
        name="dispatch",
    )(pos_flat, counts, pstart, hp)


MOE_KC = 512
N_GU_CH = D_MODEL // MOE_KC
N_DN_CH = D_EXPERT // MOE_KC
N_CH = N_GU_CH + N_DN_CH


def _moe_kernel(blk_u, used_e, meta, xs_ref, wgu_hbm, wdn_hbm, y_ref, ring, wgu_b, wdn_b, sems):
    i = pl.program_id(0)
    u = blk_u[i]
    first = (i == 0) | (u != blk_u[jnp.maximum(i - 1, 0)])
    active = i < meta[0]

    def chunk_copy(e, c):
        if c < N_GU_CH:
            src = wgu_hbm.at[e, pl.ds(c * MOE_KC, MOE_KC), :]
        else:
            src = wdn_hbm.at[e, pl.ds((c - N_GU_CH) * MOE_KC, MOE_KC), :]
        return pltpu.make_async_copy(src, ring.at[c], sems.at[c])

    @pl.when(i == 0)
    def _():
        for c in range(N_CH):
            chunk_copy(used_e[0], c).start()

    def unpack_x():
        packed = xs_ref[...]
        x_lo = lax.bitcast_convert_type(packed << 16, f32).astype(bf16)
        x_hi = lax.bitcast_convert_type(packed & jnp.uint32(0xFFFF0000), f32).astype(bf16)
        return x_lo, x_hi

    def swiglu(gu):
        g = gu[:, :D_EXPERT]
        return (g * jax.nn.sigmoid(g) * gu[:, D_EXPERT:]).astype(bf16)

    def rows(c):
        return slice(c * MOE_KC, (c + 1) * MOE_KC)

    @pl.when(active & first)
    def _():
        e_next = used_e[jnp.minimum(u + 1, meta[1] - 1)]

        def take(c):
            chunk_copy(e_next, c).wait()
            wb = ring[c].astype(bf16)
            if c < N_GU_CH:
                wgu_b[rows(c), :] = wb
            else:
                wdn_b[rows(c - N_GU_CH), :] = wb
            chunk_copy(e_next, c).start()

        x_lo, x_hi = unpack_x()
        per_half = N_GU_CH // 2
        x_chunks = [xh[:, rows(c)] for xh in (x_lo, x_hi) for c in range(per_half)]
        take(0)
        gu = None
        for c in range(N_GU_CH):
            take(c + 1)
            d = _dot(x_chunks[c], wgu_b[rows(c), :])
            gu = d if gu is None else gu + d
        a = swiglu(gu)
        y = None
        for c in range(N_DN_CH):
            if c + 1 < N_DN_CH:
                take(N_GU_CH + c + 1)
            d = _dot(a[:, rows(c)], wdn_b[rows(c), :])
            y = d if y is None else y + d
        y_ref[...] = _pack_bf16_pairs(y)

    @pl.when(active & jnp.logical_not(first))
    def _():
        x_lo, x_hi = unpack_x()
        half = D_MODEL // 2
        gu = _dot(x_lo, wgu_b[0:half, :]) + _dot(x_hi, wgu_b[half:, :])
        y_ref[...] = _pack_bf16_pairs(_dot(swiglu(gu), wdn_b[...]))

    @pl.when(i == pl.num_programs(0) - 1)
    def _():
        for c in range(N_CH):
            chunk_copy(used_e[0], c).wait()


def _moe(blk_u, used_e, meta, xs, w_gu, w_dn):
    tm = MOE_TM
    blk = lambda i, bu, ue, mt: (jnp.minimum(i, mt[0] - 1), 0)
    return pl.pallas_call(
        _moe_kernel,
        grid_spec=pltpu.PrefetchScalarGridSpec(
            num_scalar_prefetch=3,
            grid=(MOE_NBLK,),
            in_specs=[pl.BlockSpec((tm, D_MODEL // 2), blk),
                      pl.BlockSpec(memory_space=pl.ANY),
                      pl.BlockSpec(memory_space=pl.ANY)],
            out_specs=pl.BlockSpec((tm, D_MODEL // 2), blk),
            scratch_shapes=[pltpu.VMEM((N_CH, MOE_KC, D_MODEL), f32),
                            pltpu.VMEM((D_MODEL, 2 * D_EXPERT), bf16),
                            pltpu.VMEM((D_EXPERT, D_MODEL), bf16),
                            pltpu.SemaphoreType.DMA((N_CH,))]),
        out_shape=jax.ShapeDtypeStruct((MOE_ROWS, D_MODEL // 2), jnp.uint32),
        compiler_params=_cparams(("arbitrary",)),
        name="moe",
    )(blk_u, used_e, meta, xs, w_gu, w_dn)


COMB_TM = 256


def _combine_kernel(pos_ref, x1_ref, gt_ref, wc_ref, g_ref, y_ref, o_ref, ybuf_even, ybuf_odd, sems):
    i = pl.program_id(0)
    last = pl.num_programs(0) - 1
    nxt = jnp.minimum(i + 1, last)
    bufs = (ybuf_even, ybuf_odd)

    def row_copy(tile, r, k, s):
        src = pos_ref[k * N_TOK + tile * COMB_TM + r]
        return pltpu.make_async_copy(y_ref.at[pl.ds(src, 1), :], bufs[s].at[k, pl.ds(r, 1), :], sems.at[s])

    def drain(s):
        def body(g, carry):
            for _ in range(DMA_UNROLL):
                row_copy(0, 0, 0, s).wait()
                row_copy(0, 0, 0, s).wait()
            return carry
        lax.fori_loop(0, COMB_TM // DMA_UNROLL, body, 0)

    @pl.when(i == 0)
    def _():
        def body(g, carry):
            for u in range(DMA_UNROLL):
                r = g * DMA_UNROLL + u
                row_copy(0, r, 0, 0).start()
                row_copy(0, r, 1, 0).start()
            return carry
        lax.fori_loop(0, COMB_TM // DMA_UNROLL, body, 0)

    def step(s):
        drain(s)
        for r in range(COMB_TM):
            row_copy(nxt, r, 0, 1 - s).start()
            row_copy(nxt, r, 1, 1 - s).start()
        ffn = wc_ref[:, 0:1] * _unpack_bf16_pairs(bufs[s][0]) + wc_ref[:, 1:2] * _unpack_bf16_pairs(bufs[s][1])
        xo = x1_ref[...] + gt_ref[...] * ffn
        ms = jnp.mean(xo * xo, axis=-1, keepdims=True)
        o_ref[...] = xo * lax.rsqrt(ms + RMS_EPS) * g_ref[...]

        @pl.when(i == last)
        def _():
            drain(1 - s)

    @pl.when(i % 2 == 0)
    def _():
        step(0)

    @pl.when(i % 2 == 1)
    def _():
        step(1)


def _combine(pos_flat, x1, mod3, w_col, g_final, y):
    tm = COMB_TM
    per_b = SEQ // tm
    return pl.pallas_call(
        _combine_kernel,
        grid_spec=pltpu.PrefetchScalarGridSpec(
            num_scalar_prefetch=1,
            grid=(N_TOK // tm,),
            in_specs=[pl.BlockSpec((tm, D_MODEL), lambda i, pos: (i, 0)),
                      pl.BlockSpec((None, 1, D_MODEL), lambda i, pos: (i // per_b, 0, 5)),
                      pl.BlockSpec((tm, 8), lambda i, pos: (i, 0)),
                      pl.BlockSpec((1, D_MODEL), lambda i, pos: (0, 0)),
                      pl.BlockSpec(memory_space=pl.ANY)],
            out_specs=pl.BlockSpec((tm, D_MODEL), lambda i, pos: (i, 0)),
            scratch_shapes=[pltpu.VMEM((TOP_K, tm, D_MODEL // 2), jnp.uint32),
                            pltpu.VMEM((TOP_K, tm, D_MODEL // 2), jnp.uint32),
                            pltpu.SemaphoreType.DMA((2,))]),
        out_shape=jax.ShapeDtypeStruct((N_TOK, D_MODEL), f32),
        compiler_params=_cparams(("arbitrary",)),
        name="combine",
    )(pos_flat, x1, mod3, w_col, g_final, y)


def kernel(x, c, positions, w_ada, b_ada, g_mix, w_in, swa_sinks, diff_lambda_q1, diff_lambda_k1, diff_lambda_q2, diff_lambda_k2, diff_subln_g, w_branch_a, w_branch_b, w_out, g_ffn, w_router_group, b_router_group, w_router_expert, b_router_expert, w_exp_gate_up, w_exp_down, g_final):
    layer = 0
    x2 = x.reshape(N_TOK, D_MODEL)

    mod = _ada(c, w_ada[layer], b_ada[layer])
    mod3 = mod.reshape(BATCH, 1, 6 * D_MODEL)

    w_bf = w_in[layer].astype(bf16)
    s_qa = SWA_Q_HEADS * HEAD_DIM
    kv_heads = [w_bf[:, s_qa + h * HEAD_DIM:s_qa + (h + 1) * HEAD_DIM] for h in range(2 * SWA_KV_HEADS)]
    w_kv = jnp.concatenate([h for h in kv_heads for _ in range(2)], axis=1)
    half = HEAD_DIM // 2
    inv_freq = ROPE_THETA ** (-jnp.arange(0, HEAD_DIM, 2, dtype=f32) / HEAD_DIM)
    invf = jnp.tile(inv_freq, LANES // half).reshape(1, LANES)
    pos4 = jnp.repeat(positions.reshape(N_TOK // 4, 4).astype(i32), half, axis=1)
    cos4, sin4 = _rope_tables(pos4, invf)
    cos_t = jnp.tile(cos4.reshape(N_TOK, half), (1, LANES // half))
    sin_t = jnp.tile(sin4.reshape(N_TOK, half), (1, LANES // half))

    proj, proj_kv = _proj(x2, mod3, g_mix[layer].reshape(1, D_MODEL), cos_t, sin_t, w_bf, w_kv)
    o_a = _swa(swa_sinks[layer], proj, proj_kv)
    o_b = _diff(diff_lambda_q1[layer].reshape(1, HEAD_DIM), diff_lambda_k1[layer].reshape(1, HEAD_DIM),
                diff_lambda_q2[layer].reshape(1, HEAD_DIM), diff_lambda_k2[layer].reshape(1, HEAD_DIM),
                diff_subln_g[layer].reshape(LANES, 1), proj)

    wr = jnp.concatenate([w_router_group[layer], w_router_expert[layer]], axis=1)
    wr_t = jnp.pad(wr.T, ((0, ROUTER_ROWS - wr.shape[1]), (0, 0)))
    br = jnp.concatenate([b_router_group[layer], b_router_expert[layer]])
    br_col = jnp.pad(br, (0, ROUTER_ROWS - br.shape[0])).reshape(ROUTER_ROWS, 1)
    x1, hp, logits_t = _merge(o_a, o_b, proj, x2, mod3, g_ffn[layer].reshape(1, D_MODEL),
                              w_branch_a[layer].astype(bf16), w_branch_b[layer].astype(bf16),
                              w_out[layer].astype(bf16), wr_t, br_col)

    ints, wts, counts = _route(logits_t)
    cnt = counts[:, 0]
    pos, blk_u, used_e, meta, pstart = _plan(cnt, ints)
    pos_flat = pos[0:TOP_K].reshape(TOP_K * N_TOK)
    xs = _dispatch(pos_flat, cnt, pstart, hp)
    y = _moe(blk_u, used_e, meta, xs, w_exp_gate_up[layer], w_exp_down[layer])
    out = _combine(pos_flat, x1, mod3, wts.T, g_final.reshape(1, D_MODEL), y)
    return out.reshape(BATCH, SEQ, D_MODEL)
```
